```python
import math
import jax, jax.numpy as jnp
from jax import lax
import numpy as np

D_MODEL = 2048
BATCH = 1
SEQ = 8192
DEPTH = 4
DEC_BATCH = 1
DEC_SEQ = 16384
PAST_LEN = 128

GRID_W = 64
D_CONV = 512
CONV_WIDTH = 3
N_HEADS = 8
N_KV_HEADS = 2
HEAD_DIM = 128
D_ATTN = N_HEADS * HEAD_DIM
D_KV = N_KV_HEADS * HEAD_DIM
Q_BLOCK = 128
ROPE_THETA = 10000.0
N_FOURIER_GROUPS = 4
FOURIER_GROUP = 128
D_FOURIER = N_FOURIER_GROUPS * FOURIER_GROUP
N_BRANCHES = 3
D_IN = 3 * D_CONV + D_ATTN + 2 * D_KV + D_FOURIER + N_BRANCHES * D_MODEL
D_FF = 5632
LN_EPS = 1e-5
QK_EPS = 1e-6
DEEPNORM_ALPHA = (2 * DEPTH) ** 0.25
DEEPNORM_BETA = (8 * DEPTH) ** -0.25

kernel_name = 'hybrid_gated_conv_gqa_fourier_encoder'


def _split_points():
    sizes = [D_CONV, D_CONV, D_CONV, D_ATTN, D_KV, D_KV, D_FOURIER]
    pts, acc = [], 0
    for s in sizes:
        acc += s
        pts.append(acc)
    return pts


def layer_norm(x, g, b):
    xf = x.astype(jnp.float32)
    mu = jnp.mean(xf, axis=-1, keepdims=True)
    xc = xf - mu
    var = jnp.mean(xc * xc, axis=-1, keepdims=True)
    y = xc * lax.rsqrt(var + LN_EPS) * g.astype(jnp.float32) + b.astype(jnp.float32)
    return y.astype(x.dtype)


def rms_norm(x, g):
    xf = x.astype(jnp.float32)
    y = xf * lax.rsqrt(jnp.mean(xf * xf, axis=-1, keepdims=True) + QK_EPS) * g.astype(jnp.float32)
    return y.astype(x.dtype)


def dwconv3(x, w):
    xp = jnp.pad(x, ((0, 0), (1, 1), (0, 0)))
    return xp[:, :-2] * w[0] + xp[:, 1:-1] * w[1] + xp[:, 2:] * w[2]


def axial_rope_tables(seq_len, dtype):
    rows = seq_len // GRID_W
    row = jnp.repeat(jnp.arange(rows, dtype=jnp.float32), GRID_W)
    col = jnp.tile(jnp.arange(GRID_W, dtype=jnp.float32), rows)
    half = HEAD_DIM // 2
    inv_freq = ROPE_THETA ** (-jnp.arange(0, half, 2, dtype=jnp.float32) / half)
    ang = jnp.stack([row[:, None] * inv_freq, col[:, None] * inv_freq], axis=1)
    ang = ang[:, None, :, None, :]
    return jnp.cos(ang).astype(dtype), jnp.sin(ang).astype(dtype)


def apply_axial_rope(x, cos, sin):
    shp = x.shape
    xr = x.reshape(shp[:-1] + (2, 2, HEAD_DIM // 4))
    rot = jnp.stack([-xr[..., 1, :], xr[..., 0, :]], axis=-2)
    return (xr * cos + rot * sin).reshape(shp)


def block_attention(q, k, v):
    b, s = q.shape[0], q.shape[1]
    n_blk = s // Q_BLOCK
    grp = N_HEADS // N_KV_HEADS
    qb = (q * (HEAD_DIM ** -0.5)).reshape(b, n_blk, Q_BLOCK, N_KV_HEADS, grp, HEAD_DIM)
    qb = qb.transpose(1, 0, 2, 3, 4, 5)

    def one_block(qblk):
        sc = jnp.einsum('bqkgd,bskd->bkgqs', qblk, k, preferred_element_type=jnp.float32)
        p = jax.nn.softmax(sc, axis=-1).astype(v.dtype)
        return jnp.einsum('bkgqs,bskd->bqkgd', p, v)

    ob = lax.map(one_block, qb)
    return ob.transpose(1, 0, 2, 3, 4, 5).reshape(b, s, D_ATTN)


def token_mixer(x, w_in, conv_w, q_gain, k_gain, w_conv_out, w_attn_out, w_fourier_out, w_o, cos, sin):
    b, s, _ = x.shape
    u = x @ w_in
    cb, cc, cx, q, k, v, f, g = jnp.split(u, _split_points(), axis=-1)
    y_a = (cb * dwconv3(cc * cx, conv_w)) @ w_conv_out
    q = apply_axial_rope(rms_norm(q.reshape(b, s, N_HEADS, HEAD_DIM), q_gain), cos, sin)
    k = apply_axial_rope(rms_norm(k.reshape(b, s, N_KV_HEADS, HEAD_DIM), k_gain), cos, sin)
    v = v.reshape(b, s, N_KV_HEADS, HEAD_DIM)
    y_b = block_attention(q, k, v) @ w_attn_out
    ff = f.reshape(b, s, N_FOURIER_GROUPS, FOURIER_GROUP).astype(jnp.float32)
    fr = jnp.fft.fft2(ff, axes=(1, 3), norm='ortho').real.astype(x.dtype).reshape(b, s, D_FOURIER)
    y_c = fr @ w_fourier_out
    gates = jax.nn.sigmoid(g.astype(jnp.float32)).astype(x.dtype).reshape(b, s, N_BRANCHES, D_MODEL)
    merged = gates[:, :, 0] * y_a + gates[:, :, 1] * y_b + gates[:, :, 2] * y_c
    return merged @ w_o


def channel_mixer(x, w_up, ffn_conv_w, w_down):
    hg, hv = jnp.split(x @ w_up, 2, axis=-1)
    h = jax.nn.silu(dwconv3(hg, ffn_conv_w)) * hv
    return h @ w_down


def trunk(x, w_in, conv_w, q_gain, k_gain, w_conv_out, w_attn_out, w_fourier_out, w_o,
          ln1_g, ln1_b, w_up, ffn_conv_w, w_down, ln2_g, ln2_b):
    cos, sin = axial_rope_tables(x.shape[1], x.dtype)
    for l in range(DEPTH):
        t = token_mixer(x, w_in[l], conv_w[l], q_gain[l], k_gain[l], w_conv_out[l],
                        w_attn_out[l], w_fourier_out[l], w_o[l], cos, sin)
        x = layer_norm(DEEPNORM_ALPHA * x + t, ln1_g[l], ln1_b[l])
        c = channel_mixer(x, w_up[l], ffn_conv_w[l], w_down[l])
        x = layer_norm(DEEPNORM_ALPHA * x + c, ln2_g[l], ln2_b[l])
    return x


def _normal(k, shape, scale):
    return jax.random.normal(k, shape, jnp.float32) * scale


def setup_inputs(seed: int = 0) -> dict:
    key = jax.random.key(seed)
    ks = jax.random.split(key, 20)
    L = DEPTH
    return {
        'x_prompt': _normal(ks[0], (BATCH, SEQ, D_MODEL), 1.0),
        'x_sample': _normal(ks[1], (DEC_BATCH, DEC_SEQ, D_MODEL), 1.0),
        'w_in': _normal(ks[2], (L, D_MODEL, D_IN), D_MODEL ** -0.5),
        'conv_w': _normal(ks[3], (L, CONV_WIDTH, D_CONV), CONV_WIDTH ** -0.5),
        'q_gain': 1.0 + _normal(ks[4], (L, HEAD_DIM), 0.02),
        'k_gain': 1.0 + _normal(ks[5], (L, HEAD_DIM), 0.02),
        'w_conv_out': _normal(ks[6], (L, D_CONV, D_MODEL), D_CONV ** -0.5),
        'w_attn_out': _normal(ks[7], (L, D_ATTN, D_MODEL), D_ATTN ** -0.5),
        'w_fourier_out': _normal(ks[8], (L, D_FOURIER, D_MODEL), D_FOURIER ** -0.5),
        'w_o': _normal(ks[9], (L, D_MODEL, D_MODEL), DEEPNORM_BETA * D_MODEL ** -0.5),
        'ln1_g': 1.0 + _normal(ks[10], (L, D_MODEL), 0.02),
        'ln1_b': _normal(ks[11], (L, D_MODEL), 0.02),
        'w_up': _normal(ks[12], (L, D_MODEL, 2 * D_FF), D_MODEL ** -0.5),
        'ffn_conv_w': _normal(ks[13], (L, CONV_WIDTH, D_FF), CONV_WIDTH ** -0.5),
        'w_down': _normal(ks[14], (L, D_FF, D_MODEL), DEEPNORM_BETA * D_FF ** -0.5),
        'ln2_g': 1.0 + _normal(ks[15], (L, D_MODEL), 0.02),
        'ln2_b': _normal(ks[16], (L, D_MODEL), 0.02),
    }


def reference(x_prompt, x_sample, w_in, conv_w, q_gain, k_gain, w_conv_out, w_attn_out,
              w_fourier_out, w_o, ln1_g, ln1_b, w_up, ffn_conv_w, w_down, ln2_g, ln2_b):
    y_prompt = trunk(x_prompt, w_in, conv_w, q_gain, k_gain, w_conv_out, w_attn_out,
                     w_fourier_out, w_o, ln1_g, ln1_b, w_up, ffn_conv_w, w_down, ln2_g, ln2_b)
    y_sample = trunk(x_sample, w_in, conv_w, q_gain, k_gain, w_conv_out, w_attn_out,
                     w_fourier_out, w_o, ln1_g, ln1_b, w_up, ffn_conv_w, w_down, ln2_g, ln2_b)
    return (y_prompt, y_sample)
```

```python
import functools
import math

import jax
import jax.numpy as jnp
from jax import lax
from jax.experimental import pallas as pl
from jax.experimental.pallas import tpu as pltpu

D_MODEL = 2048
DEPTH = 4
GRID_W = 64
D_CONV = 512
N_HEADS = 8
N_KV_HEADS = 2
HEAD_DIM = 128
GROUP = N_HEADS // N_KV_HEADS
D_ATTN = N_HEADS * HEAD_DIM
D_KV = N_KV_HEADS * HEAD_DIM
ROPE_THETA = 10000.0
N_FOURIER_GROUPS = 4
FOURIER_GROUP = 128
D_FOURIER = N_FOURIER_GROUPS * FOURIER_GROUP
N_BRANCHES = 3
D_FF = 5632
LN_EPS = 1e-5
QK_EPS = 1e-6
DEEPNORM_ALPHA = (2 * DEPTH) ** 0.25

_C_CONV = 0
_C_QKVF = 3 * D_CONV
_C_GATE = _C_QKVF + D_ATTN + 2 * D_KV + D_FOURIER
_W_QKVF = _C_GATE - _C_QKVF

_HALO = 16
_DFT_N1 = 128
_VMEM_LIMIT = 58 * 1024 * 1024

_F32 = jnp.float32
_BF16 = jnp.bfloat16
_HI = lax.Precision.HIGHEST


def _dot(a, b):
    return jnp.dot(a, b, preferred_element_type=_F32)


def _dot_hi(a, b):
    return jnp.dot(a, b, precision=_HI, preferred_element_type=_F32)


def _layer_norm(y, g, b):
    mu = jnp.mean(y, axis=-1, keepdims=True)
    yc = y - mu
    var = jnp.mean(yc * yc, axis=-1, keepdims=True)
    return yc * lax.rsqrt(var + LN_EPS) * g + b


def _params(sem):
    return pltpu.CompilerParams(dimension_semantics=sem, vmem_limit_bytes=_VMEM_LIMIT)


def _fill_halo_lhs(xs_ref, xm_ref, xp_ref, xn_ref, first, last):
    tm = xm_ref.shape[0]
    half = _HALO // 2
    pv = jnp.where(first, 0.0, xp_ref[...].astype(_F32))
    nx = jnp.where(last, 0.0, xn_ref[...].astype(_F32))
    xs_ref[0:tm, :] = xm_ref[...]
    xs_ref[tm:tm + _HALO, :] = jnp.concatenate([nx[0:half], pv[half:_HALO]], axis=0).astype(_BF16)


def _dwconv3_rows(p_ext, cw, tm):
    n = p_ext.shape[0]
    prev = pltpu.roll(p_ext, 1, 0)[0:tm]
    nxt = pltpu.roll(p_ext, n - 1, 0)[0:tm]
    return prev * cw[0:1, :] + p_ext[0:tm] * cw[1:2, :] + nxt * cw[2:3, :]


def _halo_specs(tm, n_rows, width, ngrid):
    hb = tm // _HALO
    last_blk = n_rows // _HALO - 1
    if ngrid == 1:
        return [
            pl.BlockSpec((tm, width), lambda i: (i, 0)),
            pl.BlockSpec((_HALO, width), lambda i: (jnp.maximum(i * hb - 1, 0), 0)),
            pl.BlockSpec((_HALO, width), lambda i: (jnp.minimum((i + 1) * hb, last_blk), 0)),
        ]
    return [
        pl.BlockSpec((tm, width), lambda i, j: (i, 0)),
        pl.BlockSpec((_HALO, width), lambda i, j: (jnp.maximum(i * hb - 1, 0), 0)),
        pl.BlockSpec((_HALO, width), lambda i, j: (jnp.minimum((i + 1) * hb, last_blk), 0)),
    ]


def _conv_in_kernel(xm_ref, xp_ref, xn_ref, w_ref, cw_ref, a_ref, xs_ref):
    i = pl.program_id(0)
    tm = xm_ref.shape[0]
    _fill_halo_lhs(xs_ref, xm_ref, xp_ref, xn_ref, i == 0, i == pl.num_programs(0) - 1)
    u = _dot(xs_ref[...], w_ref[...])
    p = u[:, D_CONV:2 * D_CONV] * u[:, 2 * D_CONV:3 * D_CONV]
    conv = _dwconv3_rows(p, cw_ref[...], tm)
    a_ref[...] = (u[0:tm, 0:D_CONV] * conv).astype(_BF16)


def _conv_in(xb, w_c3, conv_w, tm):
    s = xb.shape[0]
    return pl.pallas_call(
        _conv_in_kernel,
        grid=(s // tm,),
        in_specs=_halo_specs(tm, s, D_MODEL, 1) + [
            pl.BlockSpec((D_MODEL, 3 * D_CONV), lambda i: (0, 0)),
            pl.BlockSpec((3, D_CONV), lambda i: (0, 0)),
        ],
        out_specs=pl.BlockSpec((tm, D_CONV), lambda i: (i, 0)),
        out_shape=jax.ShapeDtypeStruct((s, D_CONV), _BF16),
        scratch_shapes=[pltpu.VMEM((tm + _HALO, D_MODEL), _BF16)],
        compiler_params=_params(("arbitrary",)),
        name="conv_in",
    )(xb, xb, xb, w_c3, conv_w)


def _qkvf_kernel(x_ref, w_ref, qg_ref, kg_ref, cos_ref, sin_ref, dft_ref,
                 q_ref, k_ref, v_ref, fr_ref, fi_ref):
    tm = x_ref.shape[0]
    u = _dot(x_ref[...], w_ref[...])
    cos = cos_ref[...]
    sin = sin_ref[...]
    lane = lax.broadcasted_iota(jnp.int32, (tm, HEAD_DIM), 1)
    first_of_pair = (lane & (HEAD_DIM // 4)) == 0

    def norm_rope(xh, gain):
        ms = jnp.mean(xh * xh, axis=-1, keepdims=True)
        y = xh * lax.rsqrt(ms + QK_EPS) * gain
        partner = jnp.where(first_of_pair,
                            pltpu.roll(y, HEAD_DIM - HEAD_DIM // 4, 1),
                            pltpu.roll(y, HEAD_DIM // 4, 1))
        return y * cos + partner * sin

    qg = qg_ref[...]
    kg = kg_ref[...]
    scale = HEAD_DIM ** -0.5
    for h in range(N_HEADS):
        sl = slice(h * HEAD_DIM, (h + 1) * HEAD_DIM)
        q_ref[:, sl] = (norm_rope(u[:, sl], qg) * scale).astype(_BF16)
    for h in range(N_KV_HEADS):
        sl = slice(h * HEAD_DIM, (h + 1) * HEAD_DIM)
        k_ref[:, sl] = norm_rope(u[:, D_ATTN + h * HEAD_DIM:D_ATTN + (h + 1) * HEAD_DIM], kg).astype(_BF16)
    v_ref[...] = u[:, D_ATTN + D_KV:D_ATTN + 2 * D_KV].astype(_BF16)
    f0 = D_ATTN + 2 * D_KV
    dft = dft_ref[...]
    for g in range(N_FOURIER_GROUPS):
        sl = slice(g * FOURIER_GROUP, (g + 1) * FOURIER_GROUP)
        z = _dot_hi(u[:, f0 + g * FOURIER_GROUP:f0 + (g + 1) * FOURIER_GROUP], dft)
        fr_ref[:, sl] = z[:, 0:FOURIER_GROUP]
        fi_ref[:, sl] = z[:, FOURIER_GROUP:2 * FOURIER_GROUP]


def _qkvf_in(xb, w_qkvf, q_gain, k_gain, cos, sin, dft, tm):
    s = xb.shape[0]
    row = lambda w: pl.BlockSpec((tm, w), lambda i: (i, 0))
    full = lambda a: pl.BlockSpec(a.shape, lambda i: (0,) * a.ndim)
    return pl.pallas_call(
        _qkvf_kernel,
        grid=(s // tm,),
        in_specs=[row(D_MODEL), full(w_qkvf), full(q_gain), full(k_gain),
                  row(HEAD_DIM), row(HEAD_DIM), full(dft)],
        out_specs=[row(D_ATTN), row(D_KV), row(D_KV), row(D_FOURIER), row(D_FOURIER)],
        out_shape=[jax.ShapeDtypeStruct((s, D_ATTN), _BF16),
                   jax.ShapeDtypeStruct((s, D_KV), _BF16),
                   jax.ShapeDtypeStruct((s, D_KV), _BF16),
                   jax.ShapeDtypeStruct((s, D_FOURIER), _F32),
                   jax.ShapeDtypeStruct((s, D_FOURIER), _F32)],
        compiler_params=_params(("arbitrary",)),
        name="qkvf_in",
    )(xb, w_qkvf, q_gain, k_gain, cos, sin, dft)


def _four_a_kernel(fr_ref, fi_ref, c_ref, s_ref, yr_ref, yi_ref):
    fr = fr_ref[...]
    fi = fi_ref[...]
    c = c_ref[...]
    s = s_ref[...]
    yr_ref[...] = _dot_hi(c, fr) + _dot_hi(s, fi)
    yi_ref[...] = _dot_hi(c, fi) - _dot_hi(s, fr)


def _four_a(fr2, fi2, c2, s2, tn):
    n2, width = fr2.shape
    col = pl.BlockSpec((n2, tn), lambda j: (0, j))
    mat = pl.BlockSpec((n2, n2), lambda j: (0, 0))
    return pl.pallas_call(
        _four_a_kernel,
        grid=(width // tn,),
        in_specs=[col, col, mat, mat],
        out_specs=[col, col],
        out_shape=[jax.ShapeDtypeStruct((n2, width), _F32)] * 2,
        compiler_params=_params(("arbitrary",)),
        name="four_a",
    )(fr2, fi2, c2, s2)


def _four_c_kernel(yr_ref, yi_ref, mc_ref, ms_ref, o_ref):
    for b in range(yr_ref.shape[0]):
        o = _dot_hi(mc_ref[b], yr_ref[b]) + _dot_hi(ms_ref[b], yi_ref[b])
        o_ref[:, b * D_FOURIER:(b + 1) * D_FOURIER] = o.astype(_BF16)


def _four_c(yr3, yi3, mc, ms, sb):
    n2, n1, _ = yr3.shape
    dat = pl.BlockSpec((sb, n1, D_FOURIER), lambda j: (j, 0, 0))
    mat = pl.BlockSpec((sb, n1, n1), lambda j: (j, 0, 0))
    return pl.pallas_call(
        _four_c_kernel,
        grid=(n2 // sb,),
        in_specs=[dat, dat, mat, mat],
        out_specs=pl.BlockSpec((n1, sb * D_FOURIER), lambda j: (0, j)),
        out_shape=jax.ShapeDtypeStruct((n1, n2 * D_FOURIER), _BF16),
        compiler_params=_params(("arbitrary",)),
        name="four_c",
    )(yr3, yi3, mc, ms)


def _dft_tables(s):
    n1 = _DFT_N1
    n2 = s // n1
    two_pi = 2.0 * math.pi
    i2 = jnp.arange(n2, dtype=jnp.int32)
    ang2 = ((i2[:, None] * i2[None, :]) % n2).astype(_F32) * (two_pi / n2)
    c2 = jnp.cos(ang2) * (n2 ** -0.5)
    s2 = jnp.sin(ang2) * (n2 ** -0.5)
    i1 = jnp.arange(n1, dtype=jnp.int32)
    sidx = n2 * i1[None, :, None] + i2[:, None, None]
    ang1 = ((sidx * i1[None, None, :]) % s).astype(_F32) * (two_pi / s)
    mc = jnp.cos(ang1) * (n1 ** -0.5)
    ms = jnp.sin(ang1) * (n1 ** -0.5)
    return c2, s2, mc, ms


def _channel_dft_matrix():
    i = jnp.arange(FOURIER_GROUP, dtype=jnp.int32)
    ang = ((i[:, None] * i[None, :]) % FOURIER_GROUP).astype(_F32) * (2.0 * math.pi / FOURIER_GROUP)
    scale = FOURIER_GROUP ** -0.5
    return jnp.concatenate([jnp.cos(ang) * scale, -jnp.sin(ang) * scale], axis=1)


def _rope_tables(s):
    rows = s // GRID_W
    row = jnp.repeat(jnp.arange(rows, dtype=_F32), GRID_W)
    col = jnp.tile(jnp.arange(GRID_W, dtype=_F32), rows)
    half = HEAD_DIM // 2
    inv_freq = ROPE_THETA ** (-jnp.arange(0, half, 2, dtype=_F32) / half)
    ar = row[:, None] * inv_freq
    ac = col[:, None] * inv_freq
    cos = jnp.concatenate([jnp.cos(ar), jnp.cos(ar), jnp.cos(ac), jnp.cos(ac)], axis=1)
    sin = jnp.concatenate([-jnp.sin(ar), jnp.sin(ar), -jnp.sin(ac), jnp.sin(ac)], axis=1)
    return cos, sin


def _attn_kernel(q_ref, k_ref, v_ref, o_ref, *, tk):
    tq = q_ref.shape[0]
    s_len = k_ref.shape[0]
    q4 = jnp.concatenate([q_ref[:, g * HEAD_DIM:(g + 1) * HEAD_DIM] for g in range(GROUP)], axis=0)

    def body(j, carry):
        m, l, acc = carry
        start = pl.multiple_of(j * tk, tk)
        kc = k_ref[pl.ds(start, tk), :]
        vc = v_ref[pl.ds(start, tk), :]
        sc = lax.dot_general(q4, kc, (((1,), (1,)), ((), ())), preferred_element_type=_F32)
        m_new = jnp.maximum(m, jnp.max(sc, axis=-1, keepdims=True))
        alpha = jnp.exp(m - m_new)
        p = jnp.exp(sc - m_new)
        l = alpha * l + jnp.sum(p, axis=-1, keepdims=True)
        acc = alpha * acc + _dot(p.astype(_BF16), vc)
        return m_new, l, acc

    m0 = jnp.full((GROUP * tq, 1), -jnp.inf, _F32)
    l0 = jnp.zeros((GROUP * tq, 1), _F32)
    a0 = jnp.zeros((GROUP * tq, HEAD_DIM), _F32)
    _, l, acc = lax.fori_loop(0, s_len // tk, body, (m0, l0, a0))
    out = acc / l
    for g in range(GROUP):
        o_ref[:, g * HEAD_DIM:(g + 1) * HEAD_DIM] = out[g * tq:(g + 1) * tq].astype(_BF16)


def _attn(q, k, v, tq, tk):
    s = q.shape[0]
    qspec = pl.BlockSpec((tq, GROUP * HEAD_DIM), lambda h, i: (i, h))
    kvspec = pl.BlockSpec((s, HEAD_DIM), lambda h, i: (0, h))
    return pl.pallas_call(
        functools.partial(_attn_kernel, tk=tk),
        grid=(N_KV_HEADS, s // tq),
        in_specs=[qspec, kvspec, kvspec],
        out_specs=qspec,
        out_shape=jax.ShapeDtypeStruct((s, D_ATTN), _BF16),
        compiler_params=_params(("arbitrary", "arbitrary")),
        name="attn",
    )(q, k, v)


def _merge_kernel(xb_ref, xf_ref, a_ref, o_ref, fr_ref, wg0_ref, wg1_ref, wg2_ref,
                  wco_ref, wao_ref, wfo_ref, wo_ref, g_ref, b_ref, outf_ref, outb_ref):
    c = pl.program_id(1)
    xb = xb_ref[...]
    m = jax.nn.sigmoid(_dot(xb, wg0_ref[...])) * _dot(a_ref[...], wco_ref[...])
    m = m + jax.nn.sigmoid(_dot(xb, wg1_ref[...])) * _dot(o_ref[...], wao_ref[...])
    m = m + jax.nn.sigmoid(_dot(xb, wg2_ref[...])) * _dot(fr_ref[...], wfo_ref[...])

    @pl.when(c == 0)
    def _():
        outf_ref[...] = jnp.zeros_like(outf_ref)

    outf_ref[...] += _dot(m.astype(_BF16), wo_ref[...])

    @pl.when(c == pl.num_programs(1) - 1)
    def _():
        y = _layer_norm(DEEPNORM_ALPHA * xf_ref[...] + outf_ref[...], g_ref[...], b_ref[...])
        outf_ref[...] = y
        outb_ref[...] = y.astype(_BF16)


def _merge(xb, xf, a, o, fr, w_g, w_co, w_ao, w_fo, w_o, ln_g, ln_b, tm, tn):
    s = xb.shape[0]
    nc = D_MODEL // tn
    row = lambda w: pl.BlockSpec((tm, w), lambda i, c: (i, 0))
    colw = lambda k: pl.BlockSpec((k, tn), lambda i, c: (0, c))
    gate = lambda b: pl.BlockSpec((D_MODEL, tn), lambda i, c: (0, b * nc + c))
    vec = pl.BlockSpec((1, D_MODEL), lambda i, c: (0, 0))
    return pl.pallas_call(
        _merge_kernel,
        grid=(s // tm, nc),
        in_specs=[row(D_MODEL), row(D_MODEL), row(D_CONV), row(D_ATTN), row(D_FOURIER),
                  gate(0), gate(1), gate(2),
                  colw(D_CONV), colw(D_ATTN), colw(D_FOURIER),
                  pl.BlockSpec((tn, D_MODEL), lambda i, c: (c, 0)), vec, vec],
        out_specs=[row(D_MODEL), row(D_MODEL)],
        out_shape=[jax.ShapeDtypeStruct((s, D_MODEL), _F32), jax.ShapeDtypeStruct((s, D_MODEL), _BF16)],
        compiler_params=_params(("arbitrary", "arbitrary")),
        name="merge",
    )(xb, xf, a, o, fr, w_g, w_g, w_g, w_co, w_ao, w_fo, w_o, ln_g, ln_b)


def _ffn_kernel(xm_ref, xp_ref, xn_ref, xf_ref, wg_ref, wv_ref, cw_ref, wd_ref, g_ref, b_ref,
                outf_ref, outb_ref, xs_ref):
    i = pl.program_id(0)
    j = pl.program_id(1)
    tm = xm_ref.shape[0]

    @pl.when(j == 0)
    def _():
        _fill_halo_lhs(xs_ref, xm_ref, xp_ref, xn_ref, i == 0, i == pl.num_programs(0) - 1)
        outf_ref[...] = jnp.zeros_like(outf_ref)

    hg = _dot(xs_ref[...], wg_ref[...])
    hv = _dot(xs_ref[0:tm, :], wv_ref[...])
    h = jax.nn.silu(_dwconv3_rows(hg, cw_ref[...], tm)) * hv
    outf_ref[...] += _dot(h.astype(_BF16), wd_ref[...])

    @pl.when(j == pl.num_programs(1) - 1)
    def _():
        y = _layer_norm(DEEPNORM_ALPHA * xf_ref[...] + outf_ref[...], g_ref[...], b_ref[...])
        outf_ref[...] = y
        outb_ref[...] = y.astype(_BF16)


def _ffn(xb, xf, w_up, ffn_conv_w, w_down, ln_g, ln_b, tm, tn):
    s = xb.shape[0]
    nj = D_FF // tn
    row = pl.BlockSpec((tm, D_MODEL), lambda i, j: (i, 0))
    vec = pl.BlockSpec((1, D_MODEL), lambda i, j: (0, 0))
    return pl.pallas_call(
        _ffn_kernel,
        grid=(s // tm, nj),
        in_specs=_halo_specs(tm, s, D_MODEL, 2) + [
            row,
            pl.BlockSpec((D_MODEL, tn), lambda i, j: (0, j)),
            pl.BlockSpec((D_MODEL, tn), lambda i, j: (0, nj + j)),
            pl.BlockSpec((3, tn), lambda i, j: (0, j)),
            pl.BlockSpec((tn, D_MODEL), lambda i, j: (j, 0)),
            vec, vec],
        out_specs=[row, row],
        out_shape=[jax.ShapeDtypeStruct((s, D_MODEL), _F32), jax.ShapeDtypeStruct((s, D_MODEL), _BF16)],
        scratch_shapes=[pltpu.VMEM((tm + _HALO, D_MODEL), _BF16)],
        compiler_params=_params(("arbitrary", "arbitrary")),
        name="ffn",
    )(xb, xb, xb, xf, w_up, w_up, ffn_conv_w, w_down, ln_g, ln_b)


def _prep_weights(w_in, conv_w, q_gain, k_gain, w_conv_out, w_attn_out, w_fourier_out, w_o,
                  ln1_g, ln1_b, w_up, ffn_conv_w, w_down, ln2_g, ln2_b, depth):
    layers = []
    for l in range(depth):
        layers.append(dict(
            w_c3=w_in[l, :, _C_CONV:_C_QKVF].astype(_BF16),
            w_qkvf=w_in[l, :, _C_QKVF:_C_GATE].astype(_BF16),
            w_g=w_in[l, :, _C_GATE:].astype(_BF16),
            conv_w=conv_w[l],
            q_gain=q_gain[l].reshape(1, HEAD_DIM),
            k_gain=k_gain[l].reshape(1, HEAD_DIM),
            w_co=w_conv_out[l].astype(_BF16),
            w_ao=w_attn_out[l].astype(_BF16),
            w_fo=w_fourier_out[l].astype(_BF16),
            w_o=w_o[l].astype(_BF16),
            ln1_g=ln1_g[l].reshape(1, D_MODEL), ln1_b=ln1_b[l].reshape(1, D_MODEL),
            w_up=w_up[l].astype(_BF16),
            ffn_conv_w=ffn_conv_w[l],
            w_down=w_down[l].astype(_BF16),
            ln2_g=ln2_g[l].reshape(1, D_MODEL), ln2_b=ln2_b[l].reshape(1, D_MODEL),
        ))
    return layers


def _trunk(x, layers):
    s = x.shape[0]
    tm = min(512, s)
    n1 = _DFT_N1
    n2 = s // n1
    cos, sin = _rope_tables(s)
    c2, s2, mc, ms = _dft_tables(s)
    dft = _channel_dft_matrix()
    xf = x
    xb = x.astype(_BF16)
    for w in layers:
        a = _conv_in(xb, w["w_c3"], w["conv_w"], tm)
        q, k, v, fr, fi = _qkvf_in(xb, w["w_qkvf"], w["q_gain"], w["k_gain"], cos, sin, dft, tm)
        yr, yi = _four_a(fr.reshape(n2, n1 * D_FOURIER), fi.reshape(n2, n1 * D_FOURIER), c2, s2,
                         min(2048, n1 * D_FOURIER))
        fmix = _four_c(yr.reshape(n2, n1, D_FOURIER), yi.reshape(n2, n1, D_FOURIER), mc, ms,
                       min(8, n2)).reshape(s, D_FOURIER)
        o = _attn(q, k, v, min(256, s), min(512, s))
        xf, xb = _merge(xb, xf, a, o, fmix, w["w_g"], w["w_co"], w["w_ao"], w["w_fo"], w["w_o"],
                        w["ln1_g"], w["ln1_b"], tm, 512)
        xf, xb = _ffn(xb, xf, w["w_up"], w["ffn_conv_w"], w["w_down"], w["ln2_g"], w["ln2_b"], tm, 512)
    return xf


def kernel(x_prompt, x_sample, w_in, conv_w, q_gain, k_gain, w_conv_out, w_attn_out, w_fourier_out,
           w_o, ln1_g, ln1_b, w_up, ffn_conv_w, w_down, ln2_g, ln2_b):
    layers = _prep_weights(w_in, conv_w, q_gain, k_gain, w_conv_out, w_attn_out, w_fourier_out, w_o,
                           ln1_g, ln1_b, w_up, ffn_conv_w, w_down, ln2_g, ln2_b, DEPTH)
    outs = []
    for x in (x_prompt, x_sample):
        b, s, d = x.shape
        y = jnp.stack([_trunk(x[bi], layers) for bi in range(b)], axis=0)
        outs.append(y)
    return tuple(outs)
```

```python
import functools
import math

import jax
import jax.numpy as jnp
from jax import lax
from jax.experimental import pallas as pl
from jax.experimental.pallas import tpu as pltpu

D_MODEL = 2048
DEPTH = 4
GRID_W = 64
D_CONV = 512
N_HEADS = 8
N_KV_HEADS = 2
HEAD_DIM = 128
GROUP = N_HEADS // N_KV_HEADS
D_ATTN = N_HEADS * HEAD_DIM
D_KV = N_KV_HEADS * HEAD_DIM
ROPE_THETA = 10000.0
N_FOURIER_GROUPS = 4
FOURIER_GROUP = 128
D_FOURIER = N_FOURIER_GROUPS * FOURIER_GROUP
N_BRANCHES = 3
D_FF = 5632
LN_EPS = 1e-5
QK_EPS = 1e-6
DEEPNORM_ALPHA = (2 * DEPTH) ** 0.25

_C_CONV = 0
_C_QKVF = 3 * D_CONV
_C_GATE = _C_QKVF + D_ATTN + 2 * D_KV + D_FOURIER
_W_QKVF = _C_GATE - _C_QKVF

_HALO = 16
_DFT_N1 = 128
_VMEM_LIMIT = 58 * 1024 * 1024

_F32 = jnp.float32
_BF16 = jnp.bfloat16
_HI = lax.Precision.HIGHEST


def _dot(a, b):
    return jnp.dot(a, b, preferred_element_type=_F32)


def _dot_hi(a, b):
    return jnp.dot(a, b, precision=_HI, preferred_element_type=_F32)


def _layer_norm(y, g, b):
    mu = jnp.mean(y, axis=-1, keepdims=True)
    yc = y - mu
    var = jnp.mean(yc * yc, axis=-1, keepdims=True)
    return yc * lax.rsqrt(var + LN_EPS) * g + b


def _params(sem):
    return pltpu.CompilerParams(dimension_semantics=sem, vmem_limit_bytes=_VMEM_LIMIT)


def _fill_halo_lhs(xs_ref, xm_ref, xp_ref, xn_ref, first, last):
    tm = xm_ref.shape[0]
    half = _HALO // 2
    pv = jnp.where(first, 0.0, xp_ref[...].astype(_F32))
    nx = jnp.where(last, 0.0, xn_ref[...].astype(_F32))
    xs_ref[0:tm, :] = xm_ref[...]
    xs_ref[tm:tm + _HALO, :] = jnp.concatenate([nx[0:half], pv[half:_HALO]], axis=0).astype(_BF16)


def _dwconv3_rows(p_ext, cw, tm):
    n = p_ext.shape[0]
    prev = pltpu.roll(p_ext, 1, 0)[0:tm]
    nxt = pltpu.roll(p_ext, n - 1, 0)[0:tm]
    return prev * cw[0:1, :] + p_ext[0:tm] * cw[1:2, :] + nxt * cw[2:3, :]


def _halo_specs(tm, n_rows, width, ngrid):
    hb = tm // _HALO
    last_blk = n_rows // _HALO - 1
    if ngrid == 1:
        return [
            pl.BlockSpec((tm, width), lambda i: (i, 0)),
            pl.BlockSpec((_HALO, width), lambda i: (jnp.maximum(i * hb - 1, 0), 0)),
            pl.BlockSpec((_HALO, width), lambda i: (jnp.minimum((i + 1) * hb, last_blk), 0)),
        ]
    return [
        pl.BlockSpec((tm, width), lambda i, j: (i, 0)),
        pl.BlockSpec((_HALO, width), lambda i, j: (jnp.maximum(i * hb - 1, 0), 0)),
        pl.BlockSpec((_HALO, width), lambda i, j: (jnp.minimum((i + 1) * hb, last_blk), 0)),
    ]


def _conv_in_kernel(xm_ref, xp_ref, xn_ref, w_ref, cw_ref, a_ref, xs_ref):
    i = pl.program_id(0)
    tm = xm_ref.shape[0]
    _fill_halo_lhs(xs_ref, xm_ref, xp_ref, xn_ref, i == 0, i == pl.num_programs(0) - 1)
    u = _dot(xs_ref[...], w_ref[...])
    p = u[:, D_CONV:2 * D_CONV] * u[:, 2 * D_CONV:3 * D_CONV]
    conv = _dwconv3_rows(p, cw_ref[...], tm)
    a_ref[...] = (u[0:tm, 0:D_CONV] * conv).astype(_BF16)


def _conv_in(xb, w_c3, conv_w, tm):
    s = xb.shape[0]
    return pl.pallas_call(
        _conv_in_kernel,
        grid=(s // tm,),
        in_specs=_halo_specs(tm, s, D_MODEL, 1) + [
            pl.BlockSpec((D_MODEL, 3 * D_CONV), lambda i: (0, 0)),
            pl.BlockSpec((3, D_CONV), lambda i: (0, 0)),
        ],
        out_specs=pl.BlockSpec((tm, D_CONV), lambda i: (i, 0)),
        out_shape=jax.ShapeDtypeStruct((s, D_CONV), _BF16),
        scratch_shapes=[pltpu.VMEM((tm + _HALO, D_MODEL), _BF16)],
        compiler_params=_params(("arbitrary",)),
        name="conv_in",
    )(xb, xb, xb, w_c3, conv_w)


def _qkvf_kernel(x_ref, w_ref, qg_ref, kg_ref, cos_ref, sin_ref, dft_ref,
                 qt_ref, k_ref, vt_ref, fr_ref, fi_ref):
    tm = x_ref.shape[0]
    u = _dot(x_ref[...], w_ref[...])
    cos = cos_ref[...]
    sin = sin_ref[...]
    lane = lax.broadcasted_iota(jnp.int32, (tm, HEAD_DIM), 1)
    first_of_pair = (lane & (HEAD_DIM // 4)) == 0

    def norm_rope(xh, gain):
        ms = jnp.mean(xh * xh, axis=-1, keepdims=True)
        y = xh * lax.rsqrt(ms + QK_EPS) * gain
        partner = jnp.where(first_of_pair,
                            pltpu.roll(y, HEAD_DIM - HEAD_DIM // 4, 1),
                            pltpu.roll(y, HEAD_DIM // 4, 1))
        return y * cos + partner * sin

    qg = qg_ref[...]
    kg = kg_ref[...]
    scale = HEAD_DIM ** -0.5 * math.log2(math.e)
    for h in range(N_HEADS):
        sl = slice(h * HEAD_DIM, (h + 1) * HEAD_DIM)
        qt_ref[sl, :] = (norm_rope(u[:, sl], qg) * scale).T.astype(_BF16)
    for h in range(N_KV_HEADS):
        sl = slice(h * HEAD_DIM, (h + 1) * HEAD_DIM)
        k_ref[:, sl] = norm_rope(u[:, D_ATTN + h * HEAD_DIM:D_ATTN + (h + 1) * HEAD_DIM], kg).astype(_BF16)
        vt_ref[sl, :] = u[:, D_ATTN + D_KV + h * HEAD_DIM:D_ATTN + D_KV + (h + 1) * HEAD_DIM].T.astype(_BF16)
    f0 = D_ATTN + 2 * D_KV
    dft = dft_ref[...]
    for g in range(N_FOURIER_GROUPS):
        sl = slice(g * FOURIER_GROUP, (g + 1) * FOURIER_GROUP)
        z = _dot_hi(u[:, f0 + g * FOURIER_GROUP:f0 + (g + 1) * FOURIER_GROUP], dft)
        fr_ref[:, sl] = z[:, 0:FOURIER_GROUP]
        fi_ref[:, sl] = z[:, FOURIER_GROUP:2 * FOURIER_GROUP]


def _qkvf_in(xb, w_qkvf, q_gain, k_gain, cos, sin, dft, tm):
    s = xb.shape[0]
    row = lambda w: pl.BlockSpec((tm, w), lambda i: (i, 0))
    colt = lambda w: pl.BlockSpec((w, tm), lambda i: (0, i))
    full = lambda a: pl.BlockSpec(a.shape, lambda i: (0,) * a.ndim)
    return pl.pallas_call(
        _qkvf_kernel,
        grid=(s // tm,),
        in_specs=[row(D_MODEL), full(w_qkvf), full(q_gain), full(k_gain),
                  row(HEAD_DIM), row(HEAD_DIM), full(dft)],
        out_specs=[colt(D_ATTN), row(D_KV), colt(D_KV), row(D_FOURIER), row(D_FOURIER)],
        out_shape=[jax.ShapeDtypeStruct((D_ATTN, s), _BF16),
                   jax.ShapeDtypeStruct((s, D_KV), _BF16),
                   jax.ShapeDtypeStruct((D_KV, s), _BF16),
                   jax.ShapeDtypeStruct((s, D_FOURIER), _F32),
                   jax.ShapeDtypeStruct((s, D_FOURIER), _F32)],
        compiler_params=_params(("arbitrary",)),
        name="qkvf_in",
    )(xb, w_qkvf, q_gain, k_gain, cos, sin, dft)


def _four_a_kernel(fr_ref, fi_ref, c_ref, s_ref, yr_ref, yi_ref):
    fr = fr_ref[...]
    fi = fi_ref[...]
    c = c_ref[...]
    s = s_ref[...]
    yr_ref[...] = _dot_hi(c, fr) + _dot_hi(s, fi)
    yi_ref[...] = _dot_hi(c, fi) - _dot_hi(s, fr)


def _four_a(fr2, fi2, c2, s2, tn):
    n2, width = fr2.shape
    col = pl.BlockSpec((n2, tn), lambda j: (0, j))
    mat = pl.BlockSpec((n2, n2), lambda j: (0, 0))
    return pl.pallas_call(
        _four_a_kernel,
        grid=(width // tn,),
        in_specs=[col, col, mat, mat],
        out_specs=[col, col],
        out_shape=[jax.ShapeDtypeStruct((n2, width), _F32)] * 2,
        compiler_params=_params(("arbitrary",)),
        name="four_a",
    )(fr2, fi2, c2, s2)


def _four_c_kernel(yr_ref, yi_ref, mc_ref, ms_ref, o_ref):
    for b in range(yr_ref.shape[0]):
        o = _dot_hi(mc_ref[b], yr_ref[b]) + _dot_hi(ms_ref[b], yi_ref[b])
        o_ref[:, b * D_FOURIER:(b + 1) * D_FOURIER] = o.astype(_BF16)


def _four_c(yr3, yi3, mc, ms, sb):
    n2, n1, _ = yr3.shape
    dat = pl.BlockSpec((sb, n1, D_FOURIER), lambda j: (j, 0, 0))
    mat = pl.BlockSpec((sb, n1, n1), lambda j: (j, 0, 0))
    return pl.pallas_call(
        _four_c_kernel,
        grid=(n2 // sb,),
        in_specs=[dat, dat, mat, mat],
        out_specs=pl.BlockSpec((n1, sb * D_FOURIER), lambda j: (0, j)),
        out_shape=jax.ShapeDtypeStruct((n1, n2 * D_FOURIER), _BF16),
        compiler_params=_params(("arbitrary",)),
        name="four_c",
    )(yr3, yi3, mc, ms)


def _dft_tables(s):
    n1 = _DFT_N1
    n2 = s // n1
    two_pi = 2.0 * math.pi
    i2 = jnp.arange(n2, dtype=jnp.int32)
    ang2 = ((i2[:, None] * i2[None, :]) % n2).astype(_F32) * (two_pi / n2)
    c2 = jnp.cos(ang2) * (n2 ** -0.5)
    s2 = jnp.sin(ang2) * (n2 ** -0.5)
    i1 = jnp.arange(n1, dtype=jnp.int32)
    sidx = n2 * i1[None, :, None] + i2[:, None, None]
    ang1 = ((sidx * i1[None, None, :]) % s).astype(_F32) * (two_pi / s)
    mc = jnp.cos(ang1) * (n1 ** -0.5)
    ms = jnp.sin(ang1) * (n1 ** -0.5)
    return c2, s2, mc, ms


def _channel_dft_matrix():
    i = jnp.arange(FOURIER_GROUP, dtype=jnp.int32)
    ang = ((i[:, None] * i[None, :]) % FOURIER_GROUP).astype(_F32) * (2.0 * math.pi / FOURIER_GROUP)
    scale = FOURIER_GROUP ** -0.5
    return jnp.concatenate([jnp.cos(ang) * scale, -jnp.sin(ang) * scale], axis=1)


def _rope_tables(s):
    rows = s // GRID_W
    row = jnp.repeat(jnp.arange(rows, dtype=_F32), GRID_W)
    col = jnp.tile(jnp.arange(GRID_W, dtype=_F32), rows)
    half = HEAD_DIM // 2
    inv_freq = ROPE_THETA ** (-jnp.arange(0, half, 2, dtype=_F32) / half)
    ar = row[:, None] * inv_freq
    ac = col[:, None] * inv_freq
    cos = jnp.concatenate([jnp.cos(ar), jnp.cos(ar), jnp.cos(ac), jnp.cos(ac)], axis=1)
    sin = jnp.concatenate([-jnp.sin(ar), jnp.sin(ar), -jnp.sin(ac), jnp.sin(ac)], axis=1)
    return cos, sin


def _attn_kernel(qt_ref, k_ref, vt_ref, o_ref, s0_ref, s1_ref, p0_ref, p1_ref, acc_ref, *, tk):
    tq = qt_ref.shape[1]
    n_chunks = k_ref.shape[0] // tk

    def scores(j, s_ref):
        kc = k_ref[pl.ds(pl.multiple_of(j * tk, tk), tk), :]
        for g in range(GROUP):
            s_ref[:, g * tq:(g + 1) * tq] = _dot(kc, qt_ref[g * HEAD_DIM:(g + 1) * HEAD_DIM, :])

    def softmax(s_ref, p_ref, m, l):
        st = s_ref[...]
        m_new = jnp.maximum(m, jnp.max(st, axis=0, keepdims=True))
        alpha = jnp.exp2(m - m_new)
        p = jnp.exp2(st - m_new)
        p_ref[...] = p.astype(_BF16)
        return m_new, alpha * l + jnp.sum(p, axis=0, keepdims=True), alpha

    def accumulate(j, p_ref, alpha):
        vc = vt_ref[:, pl.ds(pl.multiple_of(j * tk, tk), tk)]
        acc_ref[...] = alpha * acc_ref[...] + _dot(vc, p_ref[...])

    scores(0, s0_ref)
    p1_ref[...] = jnp.zeros_like(p1_ref)
    acc_ref[...] = jnp.zeros_like(acc_ref)

    def body(i, carry):
        m, l, alpha_prev = carry
        j = 2 * i
        scores(j + 1, s1_ref)
        m, l, alpha_even = softmax(s0_ref, p0_ref, m, l)
        accumulate(jnp.maximum(j - 1, 0), p1_ref, alpha_prev)
        scores(jnp.minimum(j + 2, n_chunks - 1), s0_ref)
        m, l, alpha_odd = softmax(s1_ref, p1_ref, m, l)
        accumulate(j, p0_ref, alpha_even)
        return m, l, alpha_odd

    m0 = jnp.full((1, GROUP * tq), -jnp.inf, _F32)
    l0 = jnp.zeros((1, GROUP * tq), _F32)
    a0 = jnp.ones((1, GROUP * tq), _F32)
    _, l, alpha_last = lax.fori_loop(0, n_chunks // 2, body, (m0, l0, a0))
    accumulate(n_chunks - 1, p1_ref, alpha_last)
    out = acc_ref[...] / l
    for g in range(GROUP):
        o_ref[:, g * HEAD_DIM:(g + 1) * HEAD_DIM] = out[:, g * tq:(g + 1) * tq].T.astype(_BF16)


def _attn(qt, k, vt, tq, tk):
    s = k.shape[0]
    assert s % (2 * tk) == 0 and s % tq == 0
    mq = GROUP * tq
    return pl.pallas_call(
        functools.partial(_attn_kernel, tk=tk),
        grid=(N_KV_HEADS, s // tq),
        in_specs=[pl.BlockSpec((GROUP * HEAD_DIM, tq), lambda h, i: (h, i)),
                  pl.BlockSpec((s, HEAD_DIM), lambda h, i: (0, h)),
                  pl.BlockSpec((HEAD_DIM, s), lambda h, i: (h, 0))],
        out_specs=pl.BlockSpec((tq, GROUP * HEAD_DIM), lambda h, i: (i, h)),
        out_shape=jax.ShapeDtypeStruct((s, D_ATTN), _BF16),
        scratch_shapes=[pltpu.VMEM((tk, mq), _F32), pltpu.VMEM((tk, mq), _F32),
                        pltpu.VMEM((tk, mq), _BF16), pltpu.VMEM((tk, mq), _BF16),
                        pltpu.VMEM((HEAD_DIM, mq), _F32)],
        compiler_params=_params(("arbitrary", "arbitrary")),
        name="attn",
    )(qt, k, vt)


def _merge_kernel(xb_ref, xf_ref, a_ref, o_ref, fr_ref, wg0_ref, wg1_ref, wg2_ref,
                  wco_ref, wao_ref, wfo_ref, wo_ref, g_ref, b_ref, outf_ref, outb_ref):
    c = pl.program_id(1)
    xb = xb_ref[...]
    m = jax.nn.sigmoid(_dot(xb, wg0_ref[...])) * _dot(a_ref[...], wco_ref[...])
    m = m + jax.nn.sigmoid(_dot(xb, wg1_ref[...])) * _dot(o_ref[...], wao_ref[...])
    m = m + jax.nn.sigmoid(_dot(xb, wg2_ref[...])) * _dot(fr_ref[...], wfo_ref[...])

    @pl.when(c == 0)
    def _():
        outf_ref[...] = jnp.zeros_like(outf_ref)

    outf_ref[...] += _dot(m.astype(_BF16), wo_ref[...])

    @pl.when(c == pl.num_programs(1) - 1)
    def _():
        y = _layer_norm(DEEPNORM_ALPHA * xf_ref[...] + outf_ref[...], g_ref[...], b_ref[...])
        outf_ref[...] = y
        outb_ref[...] = y.astype(_BF16)


def _merge(xb, xf, a, o, fr, w_g, w_co, w_ao, w_fo, w_o, ln_g, ln_b, tm, tn):
    s = xb.shape[0]
    nc = D_MODEL // tn
    row = lambda w: pl.BlockSpec((tm, w), lambda i, c: (i, 0))
    colw = lambda k: pl.BlockSpec((k, tn), lambda i, c: (0, c))
    gate = lambda b: pl.BlockSpec((D_MODEL, tn), lambda i, c: (0, b * nc + c))
    vec = pl.BlockSpec((1, D_MODEL), lambda i, c: (0, 0))
    return pl.pallas_call(
        _merge_kernel,
        grid=(s // tm, nc),
        in_specs=[row(D_MODEL), row(D_MODEL), row(D_CONV), row(D_ATTN), row(D_FOURIER),
                  gate(0), gate(1), gate(2),
                  colw(D_CONV), colw(D_ATTN), colw(D_FOURIER),
                  pl.BlockSpec((tn, D_MODEL), lambda i, c: (c, 0)), vec, vec],
        out_specs=[row(D_MODEL), row(D_MODEL)],
        out_shape=[jax.ShapeDtypeStruct((s, D_MODEL), _F32), jax.ShapeDtypeStruct((s, D_MODEL), _BF16)],
        compiler_params=_params(("arbitrary", "arbitrary")),
        name="merge",
    )(xb, xf, a, o, fr, w_g, w_g, w_g, w_co, w_ao, w_fo, w_o, ln_g, ln_b)


def _ffn_kernel(xm_ref, xp_ref, xn_ref, xf_ref, wg_ref, wv_ref, cw_ref, wd_ref, g_ref, b_ref,
                outf_ref, outb_ref, xs_ref):
    i = pl.program_id(0)
    j = pl.program_id(1)
    tm = xm_ref.shape[0]

    @pl.when(j == 0)
    def _():
        _fill_halo_lhs(xs_ref, xm_ref, xp_ref, xn_ref, i == 0, i == pl.num_programs(0) - 1)
        outf_ref[...] = jnp.zeros_like(outf_ref)

    hg = _dot(xs_ref[...], wg_ref[...])
    hv = _dot(xs_ref[0:tm, :], wv_ref[...])
    h = jax.nn.silu(_dwconv3_rows(hg, cw_ref[...], tm)) * hv
    outf_ref[...] += _dot(h.astype(_BF16), wd_ref[...])

    @pl.when(j == pl.num_programs(1) - 1)
    def _():
        y = _layer_norm(DEEPNORM_ALPHA * xf_ref[...] + outf_ref[...], g_ref[...], b_ref[...])
        outf_ref[...] = y
        outb_ref[...] = y.astype(_BF16)


def _ffn(xb, xf, w_up, ffn_conv_w, w_down, ln_g, ln_b, tm, tn):
    s = xb.shape[0]
    nj = D_FF // tn
    row = pl.BlockSpec((tm, D_MODEL), lambda i, j: (i, 0))
    vec = pl.BlockSpec((1, D_MODEL), lambda i, j: (0, 0))
    return pl.pallas_call(
        _ffn_kernel,
        grid=(s // tm, nj),
        in_specs=_halo_specs(tm, s, D_MODEL, 2) + [
            row,
            pl.BlockSpec((D_MODEL, tn), lambda i, j: (0, j)),
            pl.BlockSpec((D_MODEL, tn), lambda i, j: (0, nj + j)),
            pl.BlockSpec((3, tn), lambda i, j: (0, j)),
            pl.BlockSpec((tn, D_MODEL), lambda i, j: (j, 0)),
            vec, vec],
        out_specs=[row, row],
        out_shape=[jax.ShapeDtypeStruct((s, D_MODEL), _F32), jax.ShapeDtypeStruct((s, D_MODEL), _BF16)],
        scratch_shapes=[pltpu.VMEM((tm + _HALO, D_MODEL), _BF16)],
        compiler_params=_params(("arbitrary", "arbitrary")),
        name="ffn",
    )(xb, xb, xb, xf, w_up, w_up, ffn_conv_w, w_down, ln_g, ln_b)


def _prep_weights(w_in, conv_w, q_gain, k_gain, w_conv_out, w_attn_out, w_fourier_out, w_o,
                  ln1_g, ln1_b, w_up, ffn_conv_w, w_down, ln2_g, ln2_b, depth):
    layers = []
    for l in range(depth):
        layers.append(dict(
            w_c3=w_in[l, :, _C_CONV:_C_QKVF].astype(_BF16),
            w_qkvf=w_in[l, :, _C_QKVF:_C_GATE].astype(_BF16),
            w_g=w_in[l, :, _C_GATE:].astype(_BF16),
            conv_w=conv_w[l],
            q_gain=q_gain[l].reshape(1, HEAD_DIM),
            k_gain=k_gain[l].reshape(1, HEAD_DIM),
            w_co=w_conv_out[l].astype(_BF16),
            w_ao=w_attn_out[l].astype(_BF16),
            w_fo=w_fourier_out[l].astype(_BF16),
            w_o=w_o[l].astype(_BF16),
            ln1_g=ln1_g[l].reshape(1, D_MODEL), ln1_b=ln1_b[l].reshape(1, D_MODEL),
            w_up=w_up[l].astype(_BF16),
            ffn_conv_w=ffn_conv_w[l],
            w_down=w_down[l].astype(_BF16),
            ln2_g=ln2_g[l].reshape(1, D_MODEL), ln2_b=ln2_b[l].reshape(1, D_MODEL),
        ))
    return layers


def _trunk(x, layers):
    s = x.shape[0]
    tm = min(512, s)
    n1 = _DFT_N1
    n2 = s // n1
    cos, sin = _rope_tables(s)
    c2, s2, mc, ms = _dft_tables(s)
    dft = _channel_dft_matrix()
    xf = x
    xb = x.astype(_BF16)
    for w in layers:
        a = _conv_in(xb, w["w_c3"], w["conv_w"], tm)
        qt, k, vt, fr, fi = _qkvf_in(xb, w["w_qkvf"], w["q_gain"], w["k_gain"], cos, sin, dft, tm)
        yr, yi = _four_a(fr.reshape(n2, n1 * D_FOURIER), fi.reshape(n2, n1 * D_FOURIER), c2, s2,
                         min(2048, n1 * D_FOURIER))
        fmix = _four_c(yr.reshape(n2, n1, D_FOURIER), yi.reshape(n2, n1, D_FOURIER), mc, ms,
                       min(8, n2)).reshape(s, D_FOURIER)
        o = _attn(qt, k, vt, min(256, s), min(512, s))
        xf, xb = _merge(xb, xf, a, o, fmix, w["w_g"], w["w_co"], w["w_ao"], w["w_fo"], w["w_o"],
                        w["ln1_g"], w["ln1_b"], tm, 512)
        xf, xb = _ffn(xb, xf, w["w_up"], w["ffn_conv_w"], w["w_down"], w["ln2_g"], w["ln2_b"], tm, 512)
    return xf


def kernel(x_prompt, x_sample, w_in, conv_w, q_gain, k_gain, w_conv_out, w_attn_out, w_fourier_out,
           w_o, ln1_g, ln1_b, w_up, ffn_conv_w, w_down, ln2_g, ln2_b):
    layers = _prep_weights(w_in, conv_w, q_gain, k_gain, w_conv_out, w_attn_out, w_fourier_out, w_o,
                           ln1_g, ln1_b, w_up, ffn_conv_w, w_down, ln2_g, ln2_b, DEPTH)
    outs = []
    for x in (x_prompt, x_sample):
        b, s, d = x.shape
        y = jnp.stack([_trunk(x[bi], layers) for bi in range(b)], axis=0)
        outs.append(y)
    return tuple(outs)
```

```python
import functools
import math

import jax
import jax.numpy as jnp
from jax import lax
from jax.experimental import pallas as pl
from jax.experimental.pallas import tpu as pltpu

D_MODEL = 2048
DEPTH = 4
GRID_W = 64
D_CONV = 512
N_HEADS = 8
N_KV_HEADS = 2
HEAD_DIM = 128
GROUP = N_HEADS // N_KV_HEADS
D_ATTN = N_HEADS * HEAD_DIM
D_KV = N_KV_HEADS * HEAD_DIM
ROPE_THETA = 10000.0
N_FOURIER_GROUPS = 4
FOURIER_GROUP = 128
D_FOURIER = N_FOURIER_GROUPS * FOURIER_GROUP
N_BRANCHES = 3
D_FF = 5632
LN_EPS = 1e-5
QK_EPS = 1e-6
DEEPNORM_ALPHA = (2 * DEPTH) ** 0.25

_C_CONV = 0
_C_QKVF = 3 * D_CONV
_C_GATE = _C_QKVF + D_ATTN + 2 * D_KV + D_FOURIER
_W_QKVF = _C_GATE - _C_QKVF

_HALO = 16
_DFT_N1 = 128
_VMEM_LIMIT = 58 * 1024 * 1024

_F32 = jnp.float32
_BF16 = jnp.bfloat16
_HI = lax.Precision.HIGHEST


def _dot(a, b):
    return jnp.dot(a, b, preferred_element_type=_F32)


def _dot_hi(a, b):
    return jnp.dot(a, b, precision=_HI, preferred_element_type=_F32)


def _layer_norm(y, g, b):
    mu = jnp.mean(y, axis=-1, keepdims=True)
    yc = y - mu
    var = jnp.mean(yc * yc, axis=-1, keepdims=True)
    return yc * lax.rsqrt(var + LN_EPS) * g + b


def _params(sem):
    return pltpu.CompilerParams(dimension_semantics=sem, vmem_limit_bytes=_VMEM_LIMIT)


def _fill_halo_lhs(xs_ref, xm_ref, xp_ref, xn_ref, first, last):
    tm = xm_ref.shape[0]
    half = _HALO // 2
    pv = jnp.where(first, 0.0, xp_ref[...].astype(_F32))
    nx = jnp.where(last, 0.0, xn_ref[...].astype(_F32))
    xs_ref[0:tm, :] = xm_ref[...]
    xs_ref[tm:tm + _HALO, :] = jnp.concatenate([nx[0:half], pv[half:_HALO]], axis=0).astype(_BF16)


def _dwconv3_rows(p_ext, cw, tm):
    n = p_ext.shape[0]
    prev = pltpu.roll(p_ext, 1, 0)[0:tm]
    nxt = pltpu.roll(p_ext, n - 1, 0)[0:tm]
    return prev * cw[0:1, :] + p_ext[0:tm] * cw[1:2, :] + nxt * cw[2:3, :]


def _halo_specs(tm, n_rows, width, ngrid):
    hb = tm // _HALO
    last_blk = n_rows // _HALO - 1
    if ngrid == 1:
        return [
            pl.BlockSpec((tm, width), lambda i: (i, 0)),
            pl.BlockSpec((_HALO, width), lambda i: (jnp.maximum(i * hb - 1, 0), 0)),
            pl.BlockSpec((_HALO, width), lambda i: (jnp.minimum((i + 1) * hb, last_blk), 0)),
        ]
    return [
        pl.BlockSpec((tm, width), lambda i, j: (i, 0)),
        pl.BlockSpec((_HALO, width), lambda i, j: (jnp.maximum(i * hb - 1, 0), 0)),
        pl.BlockSpec((_HALO, width), lambda i, j: (jnp.minimum((i + 1) * hb, last_blk), 0)),
    ]


def _conv_in_kernel(xm_ref, xp_ref, xn_ref, w_ref, cw_ref, a_ref, xs_ref):
    i = pl.program_id(0)
    tm = xm_ref.shape[0]
    _fill_halo_lhs(xs_ref, xm_ref, xp_ref, xn_ref, i == 0, i == pl.num_programs(0) - 1)
    u = _dot(xs_ref[...], w_ref[...])
    p = u[:, D_CONV:2 * D_CONV] * u[:, 2 * D_CONV:3 * D_CONV]
    conv = _dwconv3_rows(p, cw_ref[...], tm)
    a_ref[...] = (u[0:tm, 0:D_CONV] * conv).astype(_BF16)


def _conv_in(xb, w_c3, conv_w, tm):
    s = xb.shape[0]
    return pl.pallas_call(
        _conv_in_kernel,
        grid=(s // tm,),
        in_specs=_halo_specs(tm, s, D_MODEL, 1) + [
            pl.BlockSpec((D_MODEL, 3 * D_CONV), lambda i: (0, 0)),
            pl.BlockSpec((3, D_CONV), lambda i: (0, 0)),
        ],
        out_specs=pl.BlockSpec((tm, D_CONV), lambda i: (i, 0)),
        out_shape=jax.ShapeDtypeStruct((s, D_CONV), _BF16),
        scratch_shapes=[pltpu.VMEM((tm + _HALO, D_MODEL), _BF16)],
        compiler_params=_params(("arbitrary",)),
        name="conv_in",
    )(xb, xb, xb, w_c3, conv_w)


def _qkvf_kernel(x_ref, w_ref, qg_ref, kg_ref, cos_ref, sin_ref, dft_ref,
                 qt_ref, k_ref, vt_ref, fr_ref, fi_ref):
    tm = x_ref.shape[0]
    u = _dot(x_ref[...], w_ref[...])
    cos = cos_ref[...]
    sin = sin_ref[...]
    lane = lax.broadcasted_iota(jnp.int32, (tm, HEAD_DIM), 1)
    first_of_pair = (lane & (HEAD_DIM // 4)) == 0

    def norm_rope(xh, gain):
        ms = jnp.mean(xh * xh, axis=-1, keepdims=True)
        y = xh * lax.rsqrt(ms + QK_EPS) * gain
        partner = jnp.where(first_of_pair,
                            pltpu.roll(y, HEAD_DIM - HEAD_DIM // 4, 1),
                            pltpu.roll(y, HEAD_DIM // 4, 1))
        return y * cos + partner * sin

    qg = qg_ref[...]
    kg = kg_ref[...]
    scale = HEAD_DIM ** -0.5 * math.log2(math.e)
    for h in range(N_HEADS):
        sl = slice(h * HEAD_DIM, (h + 1) * HEAD_DIM)
        qt_ref[sl, :] = (norm_rope(u[:, sl], qg) * scale).T.astype(_BF16)
    for h in range(N_KV_HEADS):
        sl = slice(h * HEAD_DIM, (h + 1) * HEAD_DIM)
        k_ref[:, sl] = norm_rope(u[:, D_ATTN + h * HEAD_DIM:D_ATTN + (h + 1) * HEAD_DIM], kg).astype(_BF16)
        vt_ref[sl, :] = u[:, D_ATTN + D_KV + h * HEAD_DIM:D_ATTN + D_KV + (h + 1) * HEAD_DIM].T.astype(_BF16)
    f0 = D_ATTN + 2 * D_KV
    dft = dft_ref[...]
    for g in range(N_FOURIER_GROUPS):
        sl = slice(g * FOURIER_GROUP, (g + 1) * FOURIER_GROUP)
        z = _dot_hi(u[:, f0 + g * FOURIER_GROUP:f0 + (g + 1) * FOURIER_GROUP], dft)
        fr_ref[:, sl] = z[:, 0:FOURIER_GROUP]
        fi_ref[:, sl] = z[:, FOURIER_GROUP:2 * FOURIER_GROUP]


def _qkvf_in(xb, w_qkvf, q_gain, k_gain, cos, sin, dft, tm):
    s = xb.shape[0]
    row = lambda w: pl.BlockSpec((tm, w), lambda i: (i, 0))
    colt = lambda w: pl.BlockSpec((w, tm), lambda i: (0, i))
    full = lambda a: pl.BlockSpec(a.shape, lambda i: (0,) * a.ndim)
    return pl.pallas_call(
        _qkvf_kernel,
        grid=(s // tm,),
        in_specs=[row(D_MODEL), full(w_qkvf), full(q_gain), full(k_gain),
                  row(HEAD_DIM), row(HEAD_DIM), full(dft)],
        out_specs=[colt(D_ATTN), row(D_KV), colt(D_KV), row(D_FOURIER), row(D_FOURIER)],
        out_shape=[jax.ShapeDtypeStruct((D_ATTN, s), _BF16),
                   jax.ShapeDtypeStruct((s, D_KV), _BF16),
                   jax.ShapeDtypeStruct((D_KV, s), _BF16),
                   jax.ShapeDtypeStruct((s, D_FOURIER), _F32),
                   jax.ShapeDtypeStruct((s, D_FOURIER), _F32)],
        compiler_params=_params(("arbitrary",)),
        name="qkvf_in",
    )(xb, w_qkvf, q_gain, k_gain, cos, sin, dft)


def _four_a_kernel(fr_ref, fi_ref, c_ref, s_ref, yr_ref, yi_ref):
    fr = fr_ref[...]
    fi = fi_ref[...]
    c = c_ref[...]
    s = s_ref[...]
    yr_ref[...] = _dot_hi(c, fr) + _dot_hi(s, fi)
    yi_ref[...] = _dot_hi(c, fi) - _dot_hi(s, fr)


def _four_a(fr2, fi2, c2, s2, tn):
    n2, width = fr2.shape
    col = pl.BlockSpec((n2, tn), lambda j: (0, j))
    mat = pl.BlockSpec((n2, n2), lambda j: (0, 0))
    return pl.pallas_call(
        _four_a_kernel,
        grid=(width // tn,),
        in_specs=[col, col, mat, mat],
        out_specs=[col, col],
        out_shape=[jax.ShapeDtypeStruct((n2, width), _F32)] * 2,
        compiler_params=_params(("arbitrary",)),
        name="four_a",
    )(fr2, fi2, c2, s2)


def _four_c_kernel(yr_ref, yi_ref, mc_ref, ms_ref, o_ref):
    for b in range(yr_ref.shape[0]):
        o = _dot_hi(mc_ref[b], yr_ref[b]) + _dot_hi(ms_ref[b], yi_ref[b])
        o_ref[:, b * D_FOURIER:(b + 1) * D_FOURIER] = o.astype(_BF16)


def _four_c(yr3, yi3, mc, ms, sb):
    n2, n1, _ = yr3.shape
    dat = pl.BlockSpec((sb, n1, D_FOURIER), lambda j: (j, 0, 0))
    mat = pl.BlockSpec((sb, n1, n1), lambda j: (j, 0, 0))
    return pl.pallas_call(
        _four_c_kernel,
        grid=(n2 // sb,),
        in_specs=[dat, dat, mat, mat],
        out_specs=pl.BlockSpec((n1, sb * D_FOURIER), lambda j: (0, j)),
        out_shape=jax.ShapeDtypeStruct((n1, n2 * D_FOURIER), _BF16),
        compiler_params=_params(("arbitrary",)),
        name="four_c",
    )(yr3, yi3, mc, ms)


def _dft_tables(s):
    n1 = _DFT_N1
    n2 = s // n1
    two_pi = 2.0 * math.pi
    i2 = jnp.arange(n2, dtype=jnp.int32)
    ang2 = ((i2[:, None] * i2[None, :]) % n2).astype(_F32) * (two_pi / n2)
    c2 = jnp.cos(ang2) * (n2 ** -0.5)
    s2 = jnp.sin(ang2) * (n2 ** -0.5)
    i1 = jnp.arange(n1, dtype=jnp.int32)
    sidx = n2 * i1[None, :, None] + i2[:, None, None]
    ang1 = ((sidx * i1[None, None, :]) % s).astype(_F32) * (two_pi / s)
    mc = jnp.cos(ang1) * (n1 ** -0.5)
    ms = jnp.sin(ang1) * (n1 ** -0.5)
    return c2, s2, mc, ms


def _channel_dft_matrix():
    i = jnp.arange(FOURIER_GROUP, dtype=jnp.int32)
    ang = ((i[:, None] * i[None, :]) % FOURIER_GROUP).astype(_F32) * (2.0 * math.pi / FOURIER_GROUP)
    scale = FOURIER_GROUP ** -0.5
    return jnp.concatenate([jnp.cos(ang) * scale, -jnp.sin(ang) * scale], axis=1)


def _rope_tables(s):
    rows = s // GRID_W
    row = jnp.repeat(jnp.arange(rows, dtype=_F32), GRID_W)
    col = jnp.tile(jnp.arange(GRID_W, dtype=_F32), rows)
    half = HEAD_DIM // 2
    inv_freq = ROPE_THETA ** (-jnp.arange(0, half, 2, dtype=_F32) / half)
    ar = row[:, None] * inv_freq
    ac = col[:, None] * inv_freq
    cos = jnp.concatenate([jnp.cos(ar), jnp.cos(ar), jnp.cos(ac), jnp.cos(ac)], axis=1)
    sin = jnp.concatenate([-jnp.sin(ar), jnp.sin(ar), -jnp.sin(ac), jnp.sin(ac)], axis=1)
    return cos, sin


def _attn_kernel(qt_ref, k_ref, vt_ref, o_ref, s0_ref, s1_ref, p0_ref, p1_ref, acc_ref, *, tk):
    tq = qt_ref.shape[1]
    n_chunks = k_ref.shape[0] // tk

    def scores(j, s_ref):
        kc = k_ref[pl.ds(pl.multiple_of(j * tk, tk), tk), :]
        cmax = []
        for g in range(GROUP):
            st = _dot(kc, qt_ref[g * HEAD_DIM:(g + 1) * HEAD_DIM, :])
            s_ref[:, g * tq:(g + 1) * tq] = st
            cmax.append(jnp.max(st, axis=0, keepdims=True))
        return jnp.concatenate(cmax, axis=1)

    def softmax(s_ref, p_ref, m, l, cmax):
        m_new = jnp.maximum(m, cmax)
        alpha = jnp.exp2(m - m_new)
        p = jnp.exp2(s_ref[...] - m_new)
        p_ref[...] = p.astype(_BF16)
        return m_new, alpha * l + jnp.sum(p, axis=0, keepdims=True), alpha

    def accumulate(j, p_ref, alpha):
        vc = vt_ref[:, pl.ds(pl.multiple_of(j * tk, tk), tk)]
        acc_ref[...] = alpha * acc_ref[...] + _dot(vc, p_ref[...])

    cmax0 = scores(0, s0_ref)
    p1_ref[...] = jnp.zeros_like(p1_ref)
    acc_ref[...] = jnp.zeros_like(acc_ref)

    def body(i, carry):
        m, l, alpha_prev, cmax_even = carry
        j = 2 * i
        cmax_odd = scores(j + 1, s1_ref)
        m, l, alpha_even = softmax(s0_ref, p0_ref, m, l, cmax_even)
        accumulate(jnp.maximum(j - 1, 0), p1_ref, alpha_prev)
        cmax_even = scores(jnp.minimum(j + 2, n_chunks - 1), s0_ref)
        m, l, alpha_odd = softmax(s1_ref, p1_ref, m, l, cmax_odd)
        accumulate(j, p0_ref, alpha_even)
        return m, l, alpha_odd, cmax_even

    m0 = jnp.full((1, GROUP * tq), -jnp.inf, _F32)
    l0 = jnp.zeros((1, GROUP * tq), _F32)
    a0 = jnp.ones((1, GROUP * tq), _F32)
    _, l, alpha_last, _ = lax.fori_loop(0, n_chunks // 2, body, (m0, l0, a0, cmax0))
    accumulate(n_chunks - 1, p1_ref, alpha_last)
    out = acc_ref[...] / l
    for g in range(GROUP):
        o_ref[:, g * HEAD_DIM:(g + 1) * HEAD_DIM] = out[:, g * tq:(g + 1) * tq].T.astype(_BF16)


def _attn(qt, k, vt, tq, tk):
    s = k.shape[0]
    assert s % (2 * tk) == 0 and s % tq == 0
    mq = GROUP * tq
    return pl.pallas_call(
        functools.partial(_attn_kernel, tk=tk),
        grid=(N_KV_HEADS, s // tq),
        in_specs=[pl.BlockSpec((GROUP * HEAD_DIM, tq), lambda h, i: (h, i)),
                  pl.BlockSpec((s, HEAD_DIM), lambda h, i: (0, h)),
                  pl.BlockSpec((HEAD_DIM, s), lambda h, i: (h, 0))],
        out_specs=pl.BlockSpec((tq, GROUP * HEAD_DIM), lambda h, i: (i, h)),
        out_shape=jax.ShapeDtypeStruct((s, D_ATTN), _BF16),
        scratch_shapes=[pltpu.VMEM((tk, mq), _F32), pltpu.VMEM((tk, mq), _F32),
                        pltpu.VMEM((tk, mq), _BF16), pltpu.VMEM((tk, mq), _BF16),
                        pltpu.VMEM((HEAD_DIM, mq), _F32)],
        compiler_params=_params(("arbitrary", "arbitrary")),
        name="attn",
    )(qt, k, vt)


def _merge_kernel(xb_ref, xf_ref, a_ref, o_ref, fr_ref, wg0_ref, wg1_ref, wg2_ref,
                  wco_ref, wao_ref, wfo_ref, wo_ref, g_ref, b_ref, outf_ref, outb_ref):
    c = pl.program_id(1)
    xb = xb_ref[...]
    m = jax.nn.sigmoid(_dot(xb, wg0_ref[...])) * _dot(a_ref[...], wco_ref[...])
    m = m + jax.nn.sigmoid(_dot(xb, wg1_ref[...])) * _dot(o_ref[...], wao_ref[...])
    m = m + jax.nn.sigmoid(_dot(xb, wg2_ref[...])) * _dot(fr_ref[...], wfo_ref[...])

    @pl.when(c == 0)
    def _():
        outf_ref[...] = jnp.zeros_like(outf_ref)

    outf_ref[...] += _dot(m.astype(_BF16), wo_ref[...])

    @pl.when(c == pl.num_programs(1) - 1)
    def _():
        y = _layer_norm(DEEPNORM_ALPHA * xf_ref[...] + outf_ref[...], g_ref[...], b_ref[...])
        outf_ref[...] = y
        outb_ref[...] = y.astype(_BF16)


def _merge(xb, xf, a, o, fr, w_g, w_co, w_ao, w_fo, w_o, ln_g, ln_b, tm, tn):
    s = xb.shape[0]
    nc = D_MODEL // tn
    row = lambda w: pl.BlockSpec((tm, w), lambda i, c: (i, 0))
    colw = lambda k: pl.BlockSpec((k, tn), lambda i, c: (0, c))
    gate = lambda b: pl.BlockSpec((D_MODEL, tn), lambda i, c: (0, b * nc + c))
    vec = pl.BlockSpec((1, D_MODEL), lambda i, c: (0, 0))
    return pl.pallas_call(
        _merge_kernel,
        grid=(s // tm, nc),
        in_specs=[row(D_MODEL), row(D_MODEL), row(D_CONV), row(D_ATTN), row(D_FOURIER),
                  gate(0), gate(1), gate(2),
                  colw(D_CONV), colw(D_ATTN), colw(D_FOURIER),
                  pl.BlockSpec((tn, D_MODEL), lambda i, c: (c, 0)), vec, vec],
        out_specs=[row(D_MODEL), row(D_MODEL)],
        out_shape=[jax.ShapeDtypeStruct((s, D_MODEL), _F32), jax.ShapeDtypeStruct((s, D_MODEL), _BF16)],
        compiler_params=_params(("arbitrary", "arbitrary")),
        name="merge",
    )(xb, xf, a, o, fr, w_g, w_g, w_g, w_co, w_ao, w_fo, w_o, ln_g, ln_b)


def _ffn_kernel(xm_ref, xp_ref, xn_ref, xf_ref, wg_ref, wv_ref, cw_ref, wd_ref, g_ref, b_ref,
                outf_ref, outb_ref, xs_ref):
    i = pl.program_id(0)
    j = pl.program_id(1)
    tm = xm_ref.shape[0]

    @pl.when(j == 0)
    def _():
        _fill_halo_lhs(xs_ref, xm_ref, xp_ref, xn_ref, i == 0, i == pl.num_programs(0) - 1)
        outf_ref[...] = jnp.zeros_like(outf_ref)

    hg = _dot(xs_ref[...], wg_ref[...])
    hv = _dot(xs_ref[0:tm, :], wv_ref[...])
    h = jax.nn.silu(_dwconv3_rows(hg, cw_ref[...], tm)) * hv
    outf_ref[...] += _dot(h.astype(_BF16), wd_ref[...])

    @pl.when(j == pl.num_programs(1) - 1)
    def _():
        y = _layer_norm(DEEPNORM_ALPHA * xf_ref[...] + outf_ref[...], g_ref[...], b_ref[...])
        outf_ref[...] = y
        outb_ref[...] = y.astype(_BF16)


def _ffn(xb, xf, w_up, ffn_conv_w, w_down, ln_g, ln_b, tm, tn):
    s = xb.shape[0]
    nj = D_FF // tn
    row = pl.BlockSpec((tm, D_MODEL), lambda i, j: (i, 0))
    vec = pl.BlockSpec((1, D_MODEL), lambda i, j: (0, 0))
    return pl.pallas_call(
        _ffn_kernel,
        grid=(s // tm, nj),
        in_specs=_halo_specs(tm, s, D_MODEL, 2) + [
            row,
            pl.BlockSpec((D_MODEL, tn), lambda i, j: (0, j)),
            pl.BlockSpec((D_MODEL, tn), lambda i, j: (0, nj + j)),
            pl.BlockSpec((3, tn), lambda i, j: (0, j)),
            pl.BlockSpec((tn, D_MODEL), lambda i, j: (j, 0)),
            vec, vec],
        out_specs=[row, row],
        out_shape=[jax.ShapeDtypeStruct((s, D_MODEL), _F32), jax.ShapeDtypeStruct((s, D_MODEL), _BF16)],
        scratch_shapes=[pltpu.VMEM((tm + _HALO, D_MODEL), _BF16)],
        compiler_params=_params(("arbitrary", "arbitrary")),
        name="ffn",
    )(xb, xb, xb, xf, w_up, w_up, ffn_conv_w, w_down, ln_g, ln_b)


def _prep_weights(w_in, conv_w, q_gain, k_gain, w_conv_out, w_attn_out, w_fourier_out, w_o,
                  ln1_g, ln1_b, w_up, ffn_conv_w, w_down, ln2_g, ln2_b, depth):
    layers = []
    for l in range(depth):
        layers.append(dict(
            w_c3=w_in[l, :, _C_CONV:_C_QKVF].astype(_BF16),
            w_qkvf=w_in[l, :, _C_QKVF:_C_GATE].astype(_BF16),
            w_g=w_in[l, :, _C_GATE:].astype(_BF16),
            conv_w=conv_w[l],
            q_gain=q_gain[l].reshape(1, HEAD_DIM),
            k_gain=k_gain[l].reshape(1, HEAD_DIM),
            w_co=w_conv_out[l].astype(_BF16),
            w_ao=w_attn_out[l].astype(_BF16),
            w_fo=w_fourier_out[l].astype(_BF16),
            w_o=w_o[l].astype(_BF16),
            ln1_g=ln1_g[l].reshape(1, D_MODEL), ln1_b=ln1_b[l].reshape(1, D_MODEL),
            w_up=w_up[l].astype(_BF16),
            ffn_conv_w=ffn_conv_w[l],
            w_down=w_down[l].astype(_BF16),
            ln2_g=ln2_g[l].reshape(1, D_MODEL), ln2_b=ln2_b[l].reshape(1, D_MODEL),
        ))
    return layers


def _trunk(x, layers):
    s = x.shape[0]
    tm = min(512, s)
    n1 = _DFT_N1
    n2 = s // n1
    cos, sin = _rope_tables(s)
    c2, s2, mc, ms = _dft_tables(s)
    dft = _channel_dft_matrix()
    xf = x
    xb = x.astype(_BF16)
    for w in layers:
        a = _conv_in(xb, w["w_c3"], w["conv_w"], tm)
        qt, k, vt, fr, fi = _qkvf_in(xb, w["w_qkvf"], w["q_gain"], w["k_gain"], cos, sin, dft, tm)
        yr, yi = _four_a(fr.reshape(n2, n1 * D_FOURIER), fi.reshape(n2, n1 * D_FOURIER), c2, s2,
                         min(2048, n1 * D_FOURIER))
        fmix = _four_c(yr.reshape(n2, n1, D_FOURIER), yi.reshape(n2, n1, D_FOURIER), mc, ms,
                       min(8, n2)).reshape(s, D_FOURIER)
        o = _attn(qt, k, vt, min(256, s), min(1024, s // 2))
        xf, xb = _merge(xb, xf, a, o, fmix, w["w_g"], w["w_co"], w["w_ao"], w["w_fo"], w["w_o"],
                        w["ln1_g"], w["ln1_b"], tm, 512)
        xf, xb = _ffn(xb, xf, w["w_up"], w["ffn_conv_w"], w["w_down"], w["ln2_g"], w["ln2_b"], tm, 512)
    return xf


def kernel(x_prompt, x_sample, w_in, conv_w, q_gain, k_gain, w_conv_out, w_attn_out, w_fourier_out,
           w_o, ln1_g, ln1_b, w_up, ffn_conv_w, w_down, ln2_g, ln2_b):
    layers = _prep_weights(w_in, conv_w, q_gain, k_gain, w_conv_out, w_attn_out, w_fourier_out, w_o,
                           ln1_g, ln1_b, w_up, ffn_conv_w, w_down, ln2_g, ln2_b, DEPTH)
    outs = []
    for x in (x_prompt, x_sample):
        b, s, d = x.shape
        y = jnp.stack([_trunk(x[bi], layers) for bi in range(b)], axis=0)
        outs.append(y)
    return tuple(outs)
```

```python
import functools
import math

import jax
import jax.numpy as jnp
from jax import lax
from jax.experimental import pallas as pl
from jax.experimental.pallas import tpu as pltpu

D_MODEL = 2048
DEPTH = 4
GRID_W = 64
D_CONV = 512
N_HEADS = 8
N_KV_HEADS = 2
HEAD_DIM = 128
GROUP = N_HEADS // N_KV_HEADS
D_ATTN = N_HEADS * HEAD_DIM
D_KV = N_KV_HEADS * HEAD_DIM
ROPE_THETA = 10000.0
N_FOURIER_GROUPS = 4
FOURIER_GROUP = 128
D_FOURIER = N_FOURIER_GROUPS * FOURIER_GROUP
N_BRANCHES = 3
D_FF = 5632
LN_EPS = 1e-5
QK_EPS = 1e-6
DEEPNORM_ALPHA = (2 * DEPTH) ** 0.25

_C_CONV = 0
_C_QKVF = 3 * D_CONV
_C_GATE = _C_QKVF + D_ATTN + 2 * D_KV + D_FOURIER
_W_QKVF = _C_GATE - _C_QKVF

_HALO = 16
_DFT_N1 = 128
_VT_ROWS = HEAD_DIM + _HALO
_VMEM_LIMIT = 58 * 1024 * 1024

_F32 = jnp.float32
_BF16 = jnp.bfloat16


def _dot(a, b):
    return jnp.dot(a, b, preferred_element_type=_F32)


def _layer_norm(y, g, b):
    mu = jnp.mean(y, axis=-1, keepdims=True)
    yc = y - mu
    var = jnp.mean(yc * yc, axis=-1, keepdims=True)
    return yc * lax.rsqrt(var + LN_EPS) * g + b


def _params(sem):
    return pltpu.CompilerParams(dimension_semantics=sem, vmem_limit_bytes=_VMEM_LIMIT)


def _fill_halo_lhs(xs_ref, xm_ref, xp_ref, xn_ref, first, last):
    tm = xm_ref.shape[0]
    half = _HALO // 2
    pv = jnp.where(first, 0.0, xp_ref[...].astype(_F32))
    nx = jnp.where(last, 0.0, xn_ref[...].astype(_F32))
    xs_ref[0:tm, :] = xm_ref[...]
    xs_ref[tm:tm + _HALO, :] = jnp.concatenate([nx[0:half], pv[half:_HALO]], axis=0).astype(_BF16)


def _dwconv3_rows(p_ext, cw, tm):
    n = p_ext.shape[0]
    prev = pltpu.roll(p_ext, 1, 0)[0:tm]
    nxt = pltpu.roll(p_ext, n - 1, 0)[0:tm]
    return prev * cw[0:1, :] + p_ext[0:tm] * cw[1:2, :] + nxt * cw[2:3, :]


def _halo_specs(tm, n_rows, width, ngrid):
    hb = tm // _HALO
    last_blk = n_rows // _HALO - 1
    if ngrid == 1:
        return [
            pl.BlockSpec((tm, width), lambda i: (i, 0)),
            pl.BlockSpec((_HALO, width), lambda i: (jnp.maximum(i * hb - 1, 0), 0)),
            pl.BlockSpec((_HALO, width), lambda i: (jnp.minimum((i + 1) * hb, last_blk), 0)),
        ]
    return [
        pl.BlockSpec((tm, width), lambda i, j: (i, 0)),
        pl.BlockSpec((_HALO, width), lambda i, j: (jnp.maximum(i * hb - 1, 0), 0)),
        pl.BlockSpec((_HALO, width), lambda i, j: (jnp.minimum((i + 1) * hb, last_blk), 0)),
    ]


def _conv_in_kernel(xm_ref, xp_ref, xn_ref, w_ref, cw_ref, a_ref, xs_ref):
    i = pl.program_id(0)
    tm = xm_ref.shape[0]
    _fill_halo_lhs(xs_ref, xm_ref, xp_ref, xn_ref, i == 0, i == pl.num_programs(0) - 1)
    u = _dot(xs_ref[...], w_ref[...])
    p = u[:, D_CONV:2 * D_CONV] * u[:, 2 * D_CONV:3 * D_CONV]
    conv = _dwconv3_rows(p, cw_ref[...], tm)
    a_ref[...] = (u[0:tm, 0:D_CONV] * conv).astype(_BF16)


def _conv_in(xb, w_c3, conv_w, tm):
    s = xb.shape[0]
    return pl.pallas_call(
        _conv_in_kernel,
        grid=(s // tm,),
        in_specs=_halo_specs(tm, s, D_MODEL, 1) + [
            pl.BlockSpec((D_MODEL, 3 * D_CONV), lambda i: (0, 0)),
            pl.BlockSpec((3, D_CONV), lambda i: (0, 0)),
        ],
        out_specs=pl.BlockSpec((tm, D_CONV), lambda i: (i, 0)),
        out_shape=jax.ShapeDtypeStruct((s, D_CONV), _BF16),
        scratch_shapes=[pltpu.VMEM((tm + _HALO, D_MODEL), _BF16)],
        compiler_params=_params(("arbitrary",)),
        name="conv_in",
    )(xb, xb, xb, w_c3, conv_w)


def _qkvf_kernel(x_ref, w_ref, qg_ref, kg_ref, cos_ref, sin_ref, dft_ref,
                 qt_ref, k_ref, vt_ref, fr_ref, fi_ref):
    tm = x_ref.shape[0]
    u = _dot(x_ref[...], w_ref[...])
    cos = cos_ref[...]
    sin = sin_ref[...]
    lane = lax.broadcasted_iota(jnp.int32, (tm, HEAD_DIM), 1)
    first_of_pair = (lane & (HEAD_DIM // 4)) == 0

    def norm_rope(xh, gain):
        ms = jnp.mean(xh * xh, axis=-1, keepdims=True)
        y = xh * lax.rsqrt(ms + QK_EPS) * gain
        partner = jnp.where(first_of_pair,
                            pltpu.roll(y, HEAD_DIM - HEAD_DIM // 4, 1),
                            pltpu.roll(y, HEAD_DIM // 4, 1))
        return y * cos + partner * sin

    qg = qg_ref[...]
    kg = kg_ref[...]
    scale = HEAD_DIM ** -0.5 * math.log2(math.e)
    for h in range(N_HEADS):
        sl = slice(h * HEAD_DIM, (h + 1) * HEAD_DIM)
        qt_ref[sl, :] = (norm_rope(u[:, sl], qg) * scale).T.astype(_BF16)
    for h in range(N_KV_HEADS):
        sl = slice(h * HEAD_DIM, (h + 1) * HEAD_DIM)
        k_ref[:, sl] = norm_rope(u[:, D_ATTN + h * HEAD_DIM:D_ATTN + (h + 1) * HEAD_DIM], kg).astype(_BF16)
        v0 = h * _VT_ROWS
        vt_ref[v0:v0 + HEAD_DIM, :] = (
            u[:, D_ATTN + D_KV + h * HEAD_DIM:D_ATTN + D_KV + (h + 1) * HEAD_DIM].T.astype(_BF16))
        vt_ref[v0 + HEAD_DIM:v0 + _VT_ROWS, :] = jnp.ones((_HALO, tm), _BF16)
    f0 = D_ATTN + 2 * D_KV
    dft = dft_ref[...]
    for g in range(N_FOURIER_GROUPS):
        sl = slice(g * FOURIER_GROUP, (g + 1) * FOURIER_GROUP)
        fg = u[:, f0 + g * FOURIER_GROUP:f0 + (g + 1) * FOURIER_GROUP].astype(_BF16)
        z = _dot(fg, dft)
        fr_ref[:, sl] = z[:, 0:FOURIER_GROUP].astype(_BF16)
        fi_ref[:, sl] = z[:, FOURIER_GROUP:2 * FOURIER_GROUP].astype(_BF16)


def _qkvf_in(xb, w_qkvf, q_gain, k_gain, cos, sin, dft, tm):
    s = xb.shape[0]
    row = lambda w: pl.BlockSpec((tm, w), lambda i: (i, 0))
    colt = lambda w: pl.BlockSpec((w, tm), lambda i: (0, i))
    full = lambda a: pl.BlockSpec(a.shape, lambda i: (0,) * a.ndim)
    return pl.pallas_call(
        _qkvf_kernel,
        grid=(s // tm,),
        in_specs=[row(D_MODEL), full(w_qkvf), full(q_gain), full(k_gain),
                  row(HEAD_DIM), row(HEAD_DIM), full(dft)],
        out_specs=[colt(D_ATTN), row(D_KV), colt(N_KV_HEADS * _VT_ROWS), row(D_FOURIER), row(D_FOURIER)],
        out_shape=[jax.ShapeDtypeStruct((D_ATTN, s), _BF16),
                   jax.ShapeDtypeStruct((s, D_KV), _BF16),
                   jax.ShapeDtypeStruct((N_KV_HEADS * _VT_ROWS, s), _BF16),
                   jax.ShapeDtypeStruct((s, D_FOURIER), _BF16),
                   jax.ShapeDtypeStruct((s, D_FOURIER), _BF16)],
        compiler_params=_params(("arbitrary",)),
        name="qkvf_in",
    )(xb, w_qkvf, q_gain, k_gain, cos, sin, dft)


def _four_a_kernel(fr_ref, fi_ref, w_ref, yr_ref, yi_ref):
    n2 = fr_ref.shape[0]
    f = jnp.concatenate([fr_ref[...], fi_ref[...]], axis=0)
    y = _dot(w_ref[...], f)
    yr_ref[...] = y[0:n2].astype(_BF16)
    yi_ref[...] = y[n2:2 * n2].astype(_BF16)


def _four_a(fr2, fi2, wa, tn):
    n2, width = fr2.shape
    col = pl.BlockSpec((n2, tn), lambda j: (0, j))
    return pl.pallas_call(
        _four_a_kernel,
        grid=(width // tn,),
        in_specs=[col, col, pl.BlockSpec((2 * n2, 2 * n2), lambda j: (0, 0))],
        out_specs=[col, col],
        out_shape=[jax.ShapeDtypeStruct((n2, width), _BF16)] * 2,
        compiler_params=_params(("arbitrary",)),
        name="four_a",
    )(fr2, fi2, wa)


def _four_c_kernel(yr_ref, yi_ref, m_ref, o_ref):
    for b in range(yr_ref.shape[0]):
        y = jnp.concatenate([yr_ref[b], yi_ref[b]], axis=0)
        o_ref[:, b * D_FOURIER:(b + 1) * D_FOURIER] = _dot(m_ref[b], y).astype(_BF16)


def _four_c(yr3, yi3, mcs, sb):
    n2, n1, _ = yr3.shape
    dat = pl.BlockSpec((sb, n1, D_FOURIER), lambda j: (j, 0, 0))
    return pl.pallas_call(
        _four_c_kernel,
        grid=(n2 // sb,),
        in_specs=[dat, dat, pl.BlockSpec((sb, n1, 2 * n1), lambda j: (j, 0, 0))],
        out_specs=pl.BlockSpec((n1, sb * D_FOURIER), lambda j: (0, j)),
        out_shape=jax.ShapeDtypeStruct((n1, n2 * D_FOURIER), _BF16),
        compiler_params=_params(("arbitrary",)),
        name="four_c",
    )(yr3, yi3, mcs)


def _dft_tables(s):
    n1 = _DFT_N1
    n2 = s // n1
    two_pi = 2.0 * math.pi
    i2 = jnp.arange(n2, dtype=jnp.int32)
    ang2 = ((i2[:, None] * i2[None, :]) % n2).astype(_F32) * (two_pi / n2)
    c2 = jnp.cos(ang2) * (n2 ** -0.5)
    s2 = jnp.sin(ang2) * (n2 ** -0.5)
    wa = jnp.concatenate([jnp.concatenate([c2, s2], axis=1),
                          jnp.concatenate([-s2, c2], axis=1)], axis=0)
    i1 = jnp.arange(n1, dtype=jnp.int32)
    sidx = n2 * i1[None, :, None] + i2[:, None, None]
    ang1 = ((sidx * i1[None, None, :]) % s).astype(_F32) * (two_pi / s)
    mcs = jnp.concatenate([jnp.cos(ang1), jnp.sin(ang1)], axis=2) * (n1 ** -0.5)
    return wa.astype(_BF16), mcs.astype(_BF16)


def _channel_dft_matrix():
    i = jnp.arange(FOURIER_GROUP, dtype=jnp.int32)
    ang = ((i[:, None] * i[None, :]) % FOURIER_GROUP).astype(_F32) * (2.0 * math.pi / FOURIER_GROUP)
    scale = FOURIER_GROUP ** -0.5
    return jnp.concatenate([jnp.cos(ang) * scale, -jnp.sin(ang) * scale], axis=1).astype(_BF16)


def _rope_tables(s):
    rows = s // GRID_W
    row = jnp.repeat(jnp.arange(rows, dtype=_F32), GRID_W)
    col = jnp.tile(jnp.arange(GRID_W, dtype=_F32), rows)
    half = HEAD_DIM // 2
    inv_freq = ROPE_THETA ** (-jnp.arange(0, half, 2, dtype=_F32) / half)
    ar = row[:, None] * inv_freq
    ac = col[:, None] * inv_freq
    cos = jnp.concatenate([jnp.cos(ar), jnp.cos(ar), jnp.cos(ac), jnp.cos(ac)], axis=1)
    sin = jnp.concatenate([-jnp.sin(ar), jnp.sin(ar), -jnp.sin(ac), jnp.sin(ac)], axis=1)
    return cos, sin


def _attn_kernel(qt_ref, k_ref, vt_ref, o_ref, s0_ref, s1_ref, p0_ref, p1_ref, acc_ref, *, tk):
    tq = qt_ref.shape[1]
    n_chunks = k_ref.shape[0] // tk

    def scores(j, s_ref):
        kc = k_ref[pl.ds(pl.multiple_of(j * tk, tk), tk), :]
        cmax = []
        for g in range(GROUP):
            st = _dot(kc, qt_ref[g * HEAD_DIM:(g + 1) * HEAD_DIM, :])
            s_ref[:, g * tq:(g + 1) * tq] = st
            cmax.append(jnp.max(st, axis=0, keepdims=True))
        return jnp.concatenate(cmax, axis=1)

    def softmax(s_ref, p_ref, m, cmax):
        m_new = jnp.maximum(m, cmax)
        alpha = jnp.exp2(m - m_new)
        p_ref[...] = jnp.exp2((s_ref[...] - m_new).astype(_BF16))
        return m_new, alpha

    def accumulate(j, p_ref, alpha):
        vc = vt_ref[:, pl.ds(pl.multiple_of(j * tk, tk), tk)]
        acc_ref[...] = alpha * acc_ref[...] + _dot(vc, p_ref[...])

    cmax0 = scores(0, s0_ref)
    p1_ref[...] = jnp.zeros_like(p1_ref)
    acc_ref[...] = jnp.zeros_like(acc_ref)

    def body(i, carry):
        m, alpha_prev, cmax_even = carry
        j = 2 * i
        cmax_odd = scores(j + 1, s1_ref)
        m, alpha_even = softmax(s0_ref, p0_ref, m, cmax_even)
        accumulate(jnp.maximum(j - 1, 0), p1_ref, alpha_prev)
        cmax_even = scores(jnp.minimum(j + 2, n_chunks - 1), s0_ref)
        m, alpha_odd = softmax(s1_ref, p1_ref, m, cmax_odd)
        accumulate(j, p0_ref, alpha_even)
        return m, alpha_odd, cmax_even

    m0 = jnp.full((1, GROUP * tq), -jnp.inf, _F32)
    a0 = jnp.ones((1, GROUP * tq), _F32)
    _, alpha_last, _ = lax.fori_loop(0, n_chunks // 2, body, (m0, a0, cmax0))
    accumulate(n_chunks - 1, p1_ref, alpha_last)
    out = acc_ref[0:HEAD_DIM, :] / acc_ref[HEAD_DIM:HEAD_DIM + 1, :]
    for g in range(GROUP):
        o_ref[:, g * HEAD_DIM:(g + 1) * HEAD_DIM] = out[:, g * tq:(g + 1) * tq].T.astype(_BF16)


def _attn(qt, k, vt, tq, tk):
    s = k.shape[0]
    assert s % (2 * tk) == 0 and s % tq == 0
    mq = GROUP * tq
    return pl.pallas_call(
        functools.partial(_attn_kernel, tk=tk),
        grid=(N_KV_HEADS, s // tq),
        in_specs=[pl.BlockSpec((GROUP * HEAD_DIM, tq), lambda h, i: (h, i)),
                  pl.BlockSpec((s, HEAD_DIM), lambda h, i: (0, h)),
                  pl.BlockSpec((_VT_ROWS, s), lambda h, i: (h, 0))],
        out_specs=pl.BlockSpec((tq, GROUP * HEAD_DIM), lambda h, i: (i, h)),
        out_shape=jax.ShapeDtypeStruct((s, D_ATTN), _BF16),
        scratch_shapes=[pltpu.VMEM((tk, mq), _F32), pltpu.VMEM((tk, mq), _F32),
                        pltpu.VMEM((tk, mq), _BF16), pltpu.VMEM((tk, mq), _BF16),
                        pltpu.VMEM((_VT_ROWS, mq), _F32)],
        compiler_params=_params(("arbitrary", "arbitrary")),
        name="attn",
    )(qt, k, vt)


def _merge_kernel(xb_ref, xf_ref, a_ref, o_ref, fr_ref, wg0_ref, wg1_ref, wg2_ref,
                  wco_ref, wao_ref, wfo_ref, wo_ref, g_ref, b_ref, outf_ref, outb_ref):
    c = pl.program_id(1)
    xb = xb_ref[...]
    m = jax.nn.sigmoid(_dot(xb, wg0_ref[...])) * _dot(a_ref[...], wco_ref[...])
    m = m + jax.nn.sigmoid(_dot(xb, wg1_ref[...])) * _dot(o_ref[...], wao_ref[...])
    m = m + jax.nn.sigmoid(_dot(xb, wg2_ref[...])) * _dot(fr_ref[...], wfo_ref[...])

    @pl.when(c == 0)
    def _():
        outf_ref[...] = jnp.zeros_like(outf_ref)

    outf_ref[...] += _dot(m.astype(_BF16), wo_ref[...])

    @pl.when(c == pl.num_programs(1) - 1)
    def _():
        y = _layer_norm(DEEPNORM_ALPHA * xf_ref[...] + outf_ref[...], g_ref[...], b_ref[...])
        outf_ref[...] = y
        outb_ref[...] = y.astype(_BF16)


def _merge(xb, xf, a, o, fr, w_g, w_co, w_ao, w_fo, w_o, ln_g, ln_b, tm, tn):
    s = xb.shape[0]
    nc = D_MODEL // tn
    row = lambda w: pl.BlockSpec((tm, w), lambda i, c: (i, 0))
    colw = lambda k: pl.BlockSpec((k, tn), lambda i, c: (0, c))
    gate = lambda b: pl.BlockSpec((D_MODEL, tn), lambda i, c: (0, b * nc + c))
    vec = pl.BlockSpec((1, D_MODEL), lambda i, c: (0, 0))
    return pl.pallas_call(
        _merge_kernel,
        grid=(s // tm, nc),
        in_specs=[row(D_MODEL), row(D_MODEL), row(D_CONV), row(D_ATTN), row(D_FOURIER),
                  gate(0), gate(1), gate(2),
                  colw(D_CONV), colw(D_ATTN), colw(D_FOURIER),
                  pl.BlockSpec((tn, D_MODEL), lambda i, c: (c, 0)), vec, vec],
        out_specs=[row(D_MODEL), row(D_MODEL)],
        out_shape=[jax.ShapeDtypeStruct((s, D_MODEL), _F32), jax.ShapeDtypeStruct((s, D_MODEL), _BF16)],
        compiler_params=_params(("arbitrary", "arbitrary")),
        name="merge",
    )(xb, xf, a, o, fr, w_g, w_g, w_g, w_co, w_ao, w_fo, w_o, ln_g, ln_b)


def _ffn_kernel(xm_ref, xp_ref, xn_ref, xf_ref, wg_ref, wv_ref, cw_ref, wd_ref, g_ref, b_ref,
                outf_ref, outb_ref, xs_ref):
    i = pl.program_id(0)
    j = pl.program_id(1)
    tm = xm_ref.shape[0]

    @pl.when(j == 0)
    def _():
        _fill_halo_lhs(xs_ref, xm_ref, xp_ref, xn_ref, i == 0, i == pl.num_programs(0) - 1)
        outf_ref[...] = jnp.zeros_like(outf_ref)

    hg = _dot(xs_ref[...], wg_ref[...])
    hv = _dot(xs_ref[0:tm, :], wv_ref[...])
    h = jax.nn.silu(_dwconv3_rows(hg, cw_ref[...], tm)) * hv
    outf_ref[...] += _dot(h.astype(_BF16), wd_ref[...])

    @pl.when(j == pl.num_programs(1) - 1)
    def _():
        y = _layer_norm(DEEPNORM_ALPHA * xf_ref[...] + outf_ref[...], g_ref[...], b_ref[...])
        outf_ref[...] = y
        outb_ref[...] = y.astype(_BF16)


def _ffn(xb, xf, w_up, ffn_conv_w, w_down, ln_g, ln_b, tm, tn):
    s = xb.shape[0]
    nj = D_FF // tn
    row = pl.BlockSpec((tm, D_MODEL), lambda i, j: (i, 0))
    vec = pl.BlockSpec((1, D_MODEL), lambda i, j: (0, 0))
    return pl.pallas_call(
        _ffn_kernel,
        grid=(s // tm, nj),
        in_specs=_halo_specs(tm, s, D_MODEL, 2) + [
            row,
            pl.BlockSpec((D_MODEL, tn), lambda i, j: (0, j)),
            pl.BlockSpec((D_MODEL, tn), lambda i, j: (0, nj + j)),
            pl.BlockSpec((3, tn), lambda i, j: (0, j)),
            pl.BlockSpec((tn, D_MODEL), lambda i, j: (j, 0)),
            vec, vec],
        out_specs=[row, row],
        out_shape=[jax.ShapeDtypeStruct((s, D_MODEL), _F32), jax.ShapeDtypeStruct((s, D_MODEL), _BF16)],
        scratch_shapes=[pltpu.VMEM((tm + _HALO, D_MODEL), _BF16)],
        compiler_params=_params(("arbitrary", "arbitrary")),
        name="ffn",
    )(xb, xb, xb, xf, w_up, w_up, ffn_conv_w, w_down, ln_g, ln_b)


def _prep_weights(w_in, conv_w, q_gain, k_gain, w_conv_out, w_attn_out, w_fourier_out, w_o,
                  ln1_g, ln1_b, w_up, ffn_conv_w, w_down, ln2_g, ln2_b, depth):
    layers = []
    for l in range(depth):
        layers.append(dict(
            w_c3=w_in[l, :, _C_CONV:_C_QKVF].astype(_BF16),
            w_qkvf=w_in[l, :, _C_QKVF:_C_GATE].astype(_BF16),
            w_g=w_in[l, :, _C_GATE:].astype(_BF16),
            conv_w=conv_w[l],
            q_gain=q_gain[l].reshape(1, HEAD_DIM),
            k_gain=k_gain[l].reshape(1, HEAD_DIM),
            w_co=w_conv_out[l].astype(_BF16),
            w_ao=w_attn_out[l].astype(_BF16),
            w_fo=w_fourier_out[l].astype(_BF16),
            w_o=w_o[l].astype(_BF16),
            ln1_g=ln1_g[l].reshape(1, D_MODEL), ln1_b=ln1_b[l].reshape(1, D_MODEL),
            w_up=w_up[l].astype(_BF16),
            ffn_conv_w=ffn_conv_w[l],
            w_down=w_down[l].astype(_BF16),
            ln2_g=ln2_g[l].reshape(1, D_MODEL), ln2_b=ln2_b[l].reshape(1, D_MODEL),
        ))
    return layers


def _trunk(x, layers):
    s = x.shape[0]
    tm = min(512, s)
    n1 = _DFT_N1
    n2 = s // n1
    cos, sin = _rope_tables(s)
    wa, mcs = _dft_tables(s)
    dft = _channel_dft_matrix()
    xf = x
    xb = x.astype(_BF16)
    for w in layers:
        a = _conv_in(xb, w["w_c3"], w["conv_w"], tm)
        qt, k, vt, fr, fi = _qkvf_in(xb, w["w_qkvf"], w["q_gain"], w["k_gain"], cos, sin, dft, tm)
        yr, yi = _four_a(fr.reshape(n2, n1 * D_FOURIER), fi.reshape(n2, n1 * D_FOURIER), wa,
                         min(4096, n1 * D_FOURIER))
        fmix = _four_c(yr.reshape(n2, n1, D_FOURIER), yi.reshape(n2, n1, D_FOURIER), mcs,
                       min(8, n2)).reshape(s, D_FOURIER)
        o = _attn(qt, k, vt, min(256, s), min(1024, s // 2))
        xf, xb = _merge(xb, xf, a, o, fmix, w["w_g"], w["w_co"], w["w_ao"], w["w_fo"], w["w_o"],
                        w["ln1_g"], w["ln1_b"], tm, 512)
        xf, xb = _ffn(xb, xf, w["w_up"], w["ffn_conv_w"], w["w_down"], w["ln2_g"], w["ln2_b"], tm, 512)
    return xf


def kernel(x_prompt, x_sample, w_in, conv_w, q_gain, k_gain, w_conv_out, w_attn_out, w_fourier_out,
           w_o, ln1_g, ln1_b, w_up, ffn_conv_w, w_down, ln2_g, ln2_b):
    layers = _prep_weights(w_in, conv_w, q_gain, k_gain, w_conv_out, w_attn_out, w_fourier_out, w_o,
                           ln1_g, ln1_b, w_up, ffn_conv_w, w_down, ln2_g, ln2_b, DEPTH)
    outs = []
    for x in (x_prompt, x_sample):
        b, s, d = x.shape
        y = jnp.stack([_trunk(x[bi], layers) for bi in range(b)], axis=0)
        outs.append(y)
    return tuple(outs)
```

```python
import functools
import math

import jax
import jax.numpy as jnp
from jax import lax
from jax.experimental import pallas as pl
from jax.experimental.pallas import tpu as pltpu

D_MODEL = 2048
DEPTH = 4
GRID_W = 64
D_CONV = 512
N_HEADS = 8
N_KV_HEADS = 2
HEAD_DIM = 128
GROUP = N_HEADS // N_KV_HEADS
D_ATTN = N_HEADS * HEAD_DIM
D_KV = N_KV_HEADS * HEAD_DIM
ROPE_THETA = 10000.0
N_FOURIER_GROUPS = 4
FOURIER_GROUP = 128
D_FOURIER = N_FOURIER_GROUPS * FOURIER_GROUP
N_BRANCHES = 3
D_FF = 5632
LN_EPS = 1e-5
QK_EPS = 1e-6
DEEPNORM_ALPHA = (2 * DEPTH) ** 0.25

_C_CONV = 0
_C_QKVF = 3 * D_CONV
_C_GATE = _C_QKVF + D_ATTN + 2 * D_KV + D_FOURIER
_W_QKVF = _C_GATE - _C_QKVF

_HALO = 16
_DFT_N1 = 128
_VT_ROWS = HEAD_DIM + _HALO
_Q_SCALE = HEAD_DIM ** -0.5 * math.log2(math.e)
_SCORE_BOUND = 60.0
_VMEM_LIMIT = 58 * 1024 * 1024

_F32 = jnp.float32
_BF16 = jnp.bfloat16


def _dot(a, b):
    return jnp.dot(a, b, preferred_element_type=_F32)


def _layer_norm(y, g, b):
    mu = jnp.mean(y, axis=-1, keepdims=True)
    yc = y - mu
    var = jnp.mean(yc * yc, axis=-1, keepdims=True)
    return yc * lax.rsqrt(var + LN_EPS) * g + b


def _params(sem):
    return pltpu.CompilerParams(dimension_semantics=sem, vmem_limit_bytes=_VMEM_LIMIT)


def _fill_halo_lhs(xs_ref, xm_ref, xp_ref, xn_ref, first, last):
    tm = xm_ref.shape[0]
    half = _HALO // 2
    pv = jnp.where(first, 0.0, xp_ref[...].astype(_F32))
    nx = jnp.where(last, 0.0, xn_ref[...].astype(_F32))
    xs_ref[0:tm, :] = xm_ref[...]
    xs_ref[tm:tm + _HALO, :] = jnp.concatenate([nx[0:half], pv[half:_HALO]], axis=0).astype(_BF16)


def _dwconv3_rows(p_ext, cw, tm):
    n = p_ext.shape[0]
    prev = pltpu.roll(p_ext, 1, 0)[0:tm]
    nxt = pltpu.roll(p_ext, n - 1, 0)[0:tm]
    return prev * cw[0:1, :] + p_ext[0:tm] * cw[1:2, :] + nxt * cw[2:3, :]


def _halo_specs(tm, n_rows, width, ngrid):
    hb = tm // _HALO
    last_blk = n_rows // _HALO - 1
    if ngrid == 1:
        return [
            pl.BlockSpec((tm, width), lambda i: (i, 0)),
            pl.BlockSpec((_HALO, width), lambda i: (jnp.maximum(i * hb - 1, 0), 0)),
            pl.BlockSpec((_HALO, width), lambda i: (jnp.minimum((i + 1) * hb, last_blk), 0)),
        ]
    return [
        pl.BlockSpec((tm, width), lambda i, j: (i, 0)),
        pl.BlockSpec((_HALO, width), lambda i, j: (jnp.maximum(i * hb - 1, 0), 0)),
        pl.BlockSpec((_HALO, width), lambda i, j: (jnp.minimum((i + 1) * hb, last_blk), 0)),
    ]


def _conv_in_kernel(xm_ref, xp_ref, xn_ref, w_ref, cw_ref, a_ref, xs_ref):
    i = pl.program_id(0)
    tm = xm_ref.shape[0]
    _fill_halo_lhs(xs_ref, xm_ref, xp_ref, xn_ref, i == 0, i == pl.num_programs(0) - 1)
    u = _dot(xs_ref[...], w_ref[...])
    p = u[:, D_CONV:2 * D_CONV] * u[:, 2 * D_CONV:3 * D_CONV]
    conv = _dwconv3_rows(p, cw_ref[...], tm)
    a_ref[...] = (u[0:tm, 0:D_CONV] * conv).astype(_BF16)


def _conv_in(xb, w_c3, conv_w, tm):
    s = xb.shape[0]
    return pl.pallas_call(
        _conv_in_kernel,
        grid=(s // tm,),
        in_specs=_halo_specs(tm, s, D_MODEL, 1) + [
            pl.BlockSpec((D_MODEL, 3 * D_CONV), lambda i: (0, 0)),
            pl.BlockSpec((3, D_CONV), lambda i: (0, 0)),
        ],
        out_specs=pl.BlockSpec((tm, D_CONV), lambda i: (i, 0)),
        out_shape=jax.ShapeDtypeStruct((s, D_CONV), _BF16),
        scratch_shapes=[pltpu.VMEM((tm + _HALO, D_MODEL), _BF16)],
        compiler_params=_params(("arbitrary",)),
        name="conv_in",
    )(xb, xb, xb, w_c3, conv_w)


def _qkvf_kernel(x_ref, w_ref, qg_ref, kg_ref, cos_ref, sin_ref, dft_ref,
                 qt_ref, k_ref, vt_ref, fr_ref, fi_ref):
    tm = x_ref.shape[0]
    u = _dot(x_ref[...], w_ref[...])
    cos = cos_ref[...]
    sin = sin_ref[...]
    lane = lax.broadcasted_iota(jnp.int32, (tm, HEAD_DIM), 1)
    first_of_pair = (lane & (HEAD_DIM // 4)) == 0

    def norm_rope(xh, gain):
        ms = jnp.mean(xh * xh, axis=-1, keepdims=True)
        y = xh * lax.rsqrt(ms + QK_EPS) * gain
        partner = jnp.where(first_of_pair,
                            pltpu.roll(y, HEAD_DIM - HEAD_DIM // 4, 1),
                            pltpu.roll(y, HEAD_DIM // 4, 1))
        return y * cos + partner * sin

    qg = qg_ref[...]
    kg = kg_ref[...]
    for h in range(N_HEADS):
        sl = slice(h * HEAD_DIM, (h + 1) * HEAD_DIM)
        qt_ref[sl, :] = (norm_rope(u[:, sl], qg) * _Q_SCALE).T.astype(_BF16)
    for h in range(N_KV_HEADS):
        sl = slice(h * HEAD_DIM, (h + 1) * HEAD_DIM)
        k_ref[:, sl] = norm_rope(u[:, D_ATTN + h * HEAD_DIM:D_ATTN + (h + 1) * HEAD_DIM], kg).astype(_BF16)
        v0 = h * _VT_ROWS
        vt_ref[v0:v0 + HEAD_DIM, :] = (
            u[:, D_ATTN + D_KV + h * HEAD_DIM:D_ATTN + D_KV + (h + 1) * HEAD_DIM].T.astype(_BF16))
        vt_ref[v0 + HEAD_DIM:v0 + _VT_ROWS, :] = jnp.ones((_HALO, tm), _BF16)
    f0 = D_ATTN + 2 * D_KV
    dft = dft_ref[...]
    for g in range(N_FOURIER_GROUPS):
        sl = slice(g * FOURIER_GROUP, (g + 1) * FOURIER_GROUP)
        fg = u[:, f0 + g * FOURIER_GROUP:f0 + (g + 1) * FOURIER_GROUP].astype(_BF16)
        z = _dot(fg, dft)
        fr_ref[:, sl] = z[:, 0:FOURIER_GROUP].astype(_BF16)
        fi_ref[:, sl] = z[:, FOURIER_GROUP:2 * FOURIER_GROUP].astype(_BF16)


def _qkvf_in(xb, w_qkvf, q_gain, k_gain, cos, sin, dft, tm):
    s = xb.shape[0]
    row = lambda w: pl.BlockSpec((tm, w), lambda i: (i, 0))
    colt = lambda w: pl.BlockSpec((w, tm), lambda i: (0, i))
    full = lambda a: pl.BlockSpec(a.shape, lambda i: (0,) * a.ndim)
    return pl.pallas_call(
        _qkvf_kernel,
        grid=(s // tm,),
        in_specs=[row(D_MODEL), full(w_qkvf), full(q_gain), full(k_gain),
                  row(HEAD_DIM), row(HEAD_DIM), full(dft)],
        out_specs=[colt(D_ATTN), row(D_KV), colt(N_KV_HEADS * _VT_ROWS), row(D_FOURIER), row(D_FOURIER)],
        out_shape=[jax.ShapeDtypeStruct((D_ATTN, s), _BF16),
                   jax.ShapeDtypeStruct((s, D_KV), _BF16),
                   jax.ShapeDtypeStruct((N_KV_HEADS * _VT_ROWS, s), _BF16),
                   jax.ShapeDtypeStruct((s, D_FOURIER), _BF16),
                   jax.ShapeDtypeStruct((s, D_FOURIER), _BF16)],
        compiler_params=_params(("arbitrary",)),
        name="qkvf_in",
    )(xb, w_qkvf, q_gain, k_gain, cos, sin, dft)


def _four_a_kernel(fr_ref, fi_ref, w_ref, yr_ref, yi_ref):
    n2 = fr_ref.shape[0]
    f = jnp.concatenate([fr_ref[...], fi_ref[...]], axis=0)
    y = _dot(w_ref[...], f)
    yr_ref[...] = y[0:n2].astype(_BF16)
    yi_ref[...] = y[n2:2 * n2].astype(_BF16)


def _four_a(fr2, fi2, wa, tn):
    n2, width = fr2.shape
    col = pl.BlockSpec((n2, tn), lambda j: (0, j))
    return pl.pallas_call(
        _four_a_kernel,
        grid=(width // tn,),
        in_specs=[col, col, pl.BlockSpec((2 * n2, 2 * n2), lambda j: (0, 0))],
        out_specs=[col, col],
        out_shape=[jax.ShapeDtypeStruct((n2, width), _BF16)] * 2,
        compiler_params=_params(("arbitrary",)),
        name="four_a",
    )(fr2, fi2, wa)


def _four_c_kernel(yr_ref, yi_ref, m_ref, o_ref):
    for b in range(yr_ref.shape[0]):
        y = jnp.concatenate([yr_ref[b], yi_ref[b]], axis=0)
        o_ref[:, b * D_FOURIER:(b + 1) * D_FOURIER] = _dot(m_ref[b], y).astype(_BF16)


def _four_c(yr3, yi3, mcs, sb):
    n2, n1, _ = yr3.shape
    dat = pl.BlockSpec((sb, n1, D_FOURIER), lambda j: (j, 0, 0))
    return pl.pallas_call(
        _four_c_kernel,
        grid=(n2 // sb,),
        in_specs=[dat, dat, pl.BlockSpec((sb, n1, 2 * n1), lambda j: (j, 0, 0))],
        out_specs=pl.BlockSpec((n1, sb * D_FOURIER), lambda j: (0, j)),
        out_shape=jax.ShapeDtypeStruct((n1, n2 * D_FOURIER), _BF16),
        compiler_params=_params(("arbitrary",)),
        name="four_c",
    )(yr3, yi3, mcs)


def _dft_tables(s):
    n1 = _DFT_N1
    n2 = s // n1
    two_pi = 2.0 * math.pi
    i2 = jnp.arange(n2, dtype=jnp.int32)
    ang2 = ((i2[:, None] * i2[None, :]) % n2).astype(_F32) * (two_pi / n2)
    c2 = jnp.cos(ang2) * (n2 ** -0.5)
    s2 = jnp.sin(ang2) * (n2 ** -0.5)
    wa = jnp.concatenate([jnp.concatenate([c2, s2], axis=1),
                          jnp.concatenate([-s2, c2], axis=1)], axis=0)
    i1 = jnp.arange(n1, dtype=jnp.int32)
    sidx = n2 * i1[None, :, None] + i2[:, None, None]
    ang1 = ((sidx * i1[None, None, :]) % s).astype(_F32) * (two_pi / s)
    mcs = jnp.concatenate([jnp.cos(ang1), jnp.sin(ang1)], axis=2) * (n1 ** -0.5)
    return wa.astype(_BF16), mcs.astype(_BF16)


def _channel_dft_matrix():
    i = jnp.arange(FOURIER_GROUP, dtype=jnp.int32)
    ang = ((i[:, None] * i[None, :]) % FOURIER_GROUP).astype(_F32) * (2.0 * math.pi / FOURIER_GROUP)
    scale = FOURIER_GROUP ** -0.5
    return jnp.concatenate([jnp.cos(ang) * scale, -jnp.sin(ang) * scale], axis=1).astype(_BF16)


def _rope_tables(s):
    rows = s // GRID_W
    row = jnp.repeat(jnp.arange(rows, dtype=_F32), GRID_W)
    col = jnp.tile(jnp.arange(GRID_W, dtype=_F32), rows)
    half = HEAD_DIM // 2
    inv_freq = ROPE_THETA ** (-jnp.arange(0, half, 2, dtype=_F32) / half)
    ar = row[:, None] * inv_freq
    ac = col[:, None] * inv_freq
    cos = jnp.concatenate([jnp.cos(ar), jnp.cos(ar), jnp.cos(ac), jnp.cos(ac)], axis=1)
    sin = jnp.concatenate([-jnp.sin(ar), jnp.sin(ar), -jnp.sin(ac), jnp.sin(ac)], axis=1)
    return cos, sin


def _attn_kernel(qt_ref, k_ref, vt_ref, o_ref, s0_ref, s1_ref, p0_ref, p1_ref, acc_ref, *, tk):
    tq = qt_ref.shape[1]
    n_chunks = k_ref.shape[0] // tk

    def scores(j, s_ref):
        kc = k_ref[pl.ds(pl.multiple_of(j * tk, tk), tk), :]
        cmax = []
        for g in range(GROUP):
            st = _dot(kc, qt_ref[g * HEAD_DIM:(g + 1) * HEAD_DIM, :])
            s_ref[:, g * tq:(g + 1) * tq] = st
            cmax.append(jnp.max(st, axis=0, keepdims=True))
        return jnp.concatenate(cmax, axis=1)

    def softmax(s_ref, p_ref, m, cmax):
        m_new = jnp.maximum(m, cmax)
        alpha = jnp.exp2(m - m_new)
        p_ref[...] = jnp.exp2((s_ref[...] - m_new).astype(_BF16))
        return m_new, alpha

    def accumulate(j, p_ref, alpha):
        vc = vt_ref[:, pl.ds(pl.multiple_of(j * tk, tk), tk)]
        acc_ref[...] = alpha * acc_ref[...] + _dot(vc, p_ref[...])

    cmax0 = scores(0, s0_ref)
    p1_ref[...] = jnp.zeros_like(p1_ref)
    acc_ref[...] = jnp.zeros_like(acc_ref)

    def body(i, carry):
        m, alpha_prev, cmax_even = carry
        j = 2 * i
        cmax_odd = scores(j + 1, s1_ref)
        m, alpha_even = softmax(s0_ref, p0_ref, m, cmax_even)
        accumulate(jnp.maximum(j - 1, 0), p1_ref, alpha_prev)
        cmax_even = scores(jnp.minimum(j + 2, n_chunks - 1), s0_ref)
        m, alpha_odd = softmax(s1_ref, p1_ref, m, cmax_odd)
        accumulate(j, p0_ref, alpha_even)
        return m, alpha_odd, cmax_even

    m0 = jnp.full((1, GROUP * tq), -jnp.inf, _F32)
    a0 = jnp.ones((1, GROUP * tq), _F32)
    _, alpha_last, _ = lax.fori_loop(0, n_chunks // 2, body, (m0, a0, cmax0))
    accumulate(n_chunks - 1, p1_ref, alpha_last)
    out = acc_ref[0:HEAD_DIM, :] / acc_ref[HEAD_DIM:HEAD_DIM + 1, :]
    for g in range(GROUP):
        o_ref[:, g * HEAD_DIM:(g + 1) * HEAD_DIM] = out[:, g * tq:(g + 1) * tq].T.astype(_BF16)


def _attn_bounded_kernel(qt_ref, k_ref, vt_ref, o_ref, p0_ref, p1_ref, acc_ref, *, tk):
    tq = qt_ref.shape[1]
    mq = GROUP * tq
    n_chunks = k_ref.shape[0] // tk
    sub = 8

    def probs(j, p_ref, l8):
        kc = k_ref[pl.ds(pl.multiple_of(j * tk, tk), tk), :]
        parts = []
        for g in range(GROUP):
            p = jnp.exp2(_dot(kc, qt_ref[g * HEAD_DIM:(g + 1) * HEAD_DIM, :]))
            p_ref[:, g * tq:(g + 1) * tq] = p.astype(_BF16)
            parts.append(jnp.sum(p.reshape(tk // sub, sub, tq), axis=0))
        return l8 + jnp.concatenate(parts, axis=1)

    def accumulate(j, p_ref):
        vc = vt_ref[0:HEAD_DIM, pl.ds(pl.multiple_of(j * tk, tk), tk)]
        acc_ref[...] += _dot(vc, p_ref[...])

    acc_ref[...] = jnp.zeros_like(acc_ref)
    l8 = probs(0, p0_ref, jnp.zeros((sub, mq), _F32))

    def body(i, l8):
        j = 2 * i
        l8 = probs(j + 1, p1_ref, l8)
        accumulate(j, p0_ref)
        l8 = probs(j + 2, p0_ref, l8)
        accumulate(j + 1, p1_ref)
        return l8

    l8 = lax.fori_loop(0, n_chunks // 2 - 1, body, l8)
    l8 = probs(n_chunks - 1, p1_ref, l8)
    accumulate(n_chunks - 2, p0_ref)
    accumulate(n_chunks - 1, p1_ref)
    out = acc_ref[...] / jnp.sum(l8, axis=0, keepdims=True)
    for g in range(GROUP):
        o_ref[:, g * HEAD_DIM:(g + 1) * HEAD_DIM] = out[:, g * tq:(g + 1) * tq].T.astype(_BF16)


def _attn(qt, k, vt, score_bound, tq, tk, tk_bounded):
    s = k.shape[0]
    assert s % (2 * tk) == 0 and s % (2 * tk_bounded) == 0 and s % tq == 0
    mq = GROUP * tq
    bufs = lambda rows, dt: [pltpu.VMEM((rows, mq), dt), pltpu.VMEM((rows, mq), dt)]
    common = dict(
        grid=(N_KV_HEADS, s // tq),
        in_specs=[pl.BlockSpec((GROUP * HEAD_DIM, tq), lambda h, i: (h, i)),
                  pl.BlockSpec((s, HEAD_DIM), lambda h, i: (0, h)),
                  pl.BlockSpec((_VT_ROWS, s), lambda h, i: (h, 0))],
        out_specs=pl.BlockSpec((tq, GROUP * HEAD_DIM), lambda h, i: (i, h)),
        out_shape=jax.ShapeDtypeStruct((s, D_ATTN), _BF16),
        compiler_params=_params(("arbitrary", "arbitrary")),
    )
    bounded = pl.pallas_call(
        functools.partial(_attn_bounded_kernel, tk=tk_bounded),
        scratch_shapes=bufs(tk_bounded, _BF16) + [pltpu.VMEM((HEAD_DIM, mq), _F32)],
        name="attn_bounded", **common)
    online = pl.pallas_call(
        functools.partial(_attn_kernel, tk=tk),
        scratch_shapes=bufs(tk, _F32) + bufs(tk, _BF16) + [pltpu.VMEM((_VT_ROWS, mq), _F32)],
        name="attn", **common)
    return lax.cond(score_bound <= _SCORE_BOUND, bounded, online, qt, k, vt)


def _merge_kernel(xb_ref, xf_ref, a_ref, o_ref, fr_ref, wg0_ref, wg1_ref, wg2_ref,
                  wco_ref, wao_ref, wfo_ref, wo_ref, g_ref, b_ref, outf_ref, outb_ref):
    c = pl.program_id(1)
    xb = xb_ref[...]
    m = jax.nn.sigmoid(_dot(xb, wg0_ref[...])) * _dot(a_ref[...], wco_ref[...])
    m = m + jax.nn.sigmoid(_dot(xb, wg1_ref[...])) * _dot(o_ref[...], wao_ref[...])
    m = m + jax.nn.sigmoid(_dot(xb, wg2_ref[...])) * _dot(fr_ref[...], wfo_ref[...])

    @pl.when(c == 0)
    def _():
        outf_ref[...] = jnp.zeros_like(outf_ref)

    outf_ref[...] += _dot(m.astype(_BF16), wo_ref[...])

    @pl.when(c == pl.num_programs(1) - 1)
    def _():
        y = _layer_norm(DEEPNORM_ALPHA * xf_ref[...] + outf_ref[...], g_ref[...], b_ref[...])
        outf_ref[...] = y
        outb_ref[...] = y.astype(_BF16)


def _merge(xb, xf, a, o, fr, w_g, w_co, w_ao, w_fo, w_o, ln_g, ln_b, tm, tn):
    s = xb.shape[0]
    nc = D_MODEL // tn
    row = lambda w: pl.BlockSpec((tm, w), lambda i, c: (i, 0))
    colw = lambda k: pl.BlockSpec((k, tn), lambda i, c: (0, c))
    gate = lambda b: pl.BlockSpec((D_MODEL, tn), lambda i, c: (0, b * nc + c))
    vec = pl.BlockSpec((1, D_MODEL), lambda i, c: (0, 0))
    return pl.pallas_call(
        _merge_kernel,
        grid=(s // tm, nc),
        in_specs=[row(D_MODEL), row(D_MODEL), row(D_CONV), row(D_ATTN), row(D_FOURIER),
                  gate(0), gate(1), gate(2),
                  colw(D_CONV), colw(D_ATTN), colw(D_FOURIER),
                  pl.BlockSpec((tn, D_MODEL), lambda i, c: (c, 0)), vec, vec],
        out_specs=[row(D_MODEL), row(D_MODEL)],
        out_shape=[jax.ShapeDtypeStruct((s, D_MODEL), _F32), jax.ShapeDtypeStruct((s, D_MODEL), _BF16)],
        compiler_params=_params(("arbitrary", "arbitrary")),
        name="merge",
    )(xb, xf, a, o, fr, w_g, w_g, w_g, w_co, w_ao, w_fo, w_o, ln_g, ln_b)


def _ffn_kernel(xm_ref, xp_ref, xn_ref, xf_ref, wg_ref, wv_ref, cw_ref, wd_ref, g_ref, b_ref,
                outf_ref, outb_ref, xs_ref):
    i = pl.program_id(0)
    j = pl.program_id(1)
    tm = xm_ref.shape[0]

    @pl.when(j == 0)
    def _():
        _fill_halo_lhs(xs_ref, xm_ref, xp_ref, xn_ref, i == 0, i == pl.num_programs(0) - 1)
        outf_ref[...] = jnp.zeros_like(outf_ref)

    hg = _dot(xs_ref[...], wg_ref[...])
    hv = _dot(xs_ref[0:tm, :], wv_ref[...])
    h = jax.nn.silu(_dwconv3_rows(hg, cw_ref[...], tm)) * hv
    outf_ref[...] += _dot(h.astype(_BF16), wd_ref[...])

    @pl.when(j == pl.num_programs(1) - 1)
    def _():
        y = _layer_norm(DEEPNORM_ALPHA * xf_ref[...] + outf_ref[...], g_ref[...], b_ref[...])
        outf_ref[...] = y
        outb_ref[...] = y.astype(_BF16)


def _ffn(xb, xf, w_up, ffn_conv_w, w_down, ln_g, ln_b, tm, tn):
    s = xb.shape[0]
    nj = D_FF // tn
    row = pl.BlockSpec((tm, D_MODEL), lambda i, j: (i, 0))
    vec = pl.BlockSpec((1, D_MODEL), lambda i, j: (0, 0))
    return pl.pallas_call(
        _ffn_kernel,
        grid=(s // tm, nj),
        in_specs=_halo_specs(tm, s, D_MODEL, 2) + [
            row,
            pl.BlockSpec((D_MODEL, tn), lambda i, j: (0, j)),
            pl.BlockSpec((D_MODEL, tn), lambda i, j: (0, nj + j)),
            pl.BlockSpec((3, tn), lambda i, j: (0, j)),
            pl.BlockSpec((tn, D_MODEL), lambda i, j: (j, 0)),
            vec, vec],
        out_specs=[row, row],
        out_shape=[jax.ShapeDtypeStruct((s, D_MODEL), _F32), jax.ShapeDtypeStruct((s, D_MODEL), _BF16)],
        scratch_shapes=[pltpu.VMEM((tm + _HALO, D_MODEL), _BF16)],
        compiler_params=_params(("arbitrary", "arbitrary")),
        name="ffn",
    )(xb, xb, xb, xf, w_up, w_up, ffn_conv_w, w_down, ln_g, ln_b)


def _prep_weights(w_in, conv_w, q_gain, k_gain, w_conv_out, w_attn_out, w_fourier_out, w_o,
                  ln1_g, ln1_b, w_up, ffn_conv_w, w_down, ln2_g, ln2_b, depth):
    layers = []
    for l in range(depth):
        layers.append(dict(
            w_c3=w_in[l, :, _C_CONV:_C_QKVF].astype(_BF16),
            w_qkvf=w_in[l, :, _C_QKVF:_C_GATE].astype(_BF16),
            w_g=w_in[l, :, _C_GATE:].astype(_BF16),
            conv_w=conv_w[l],
            q_gain=q_gain[l].reshape(1, HEAD_DIM),
            k_gain=k_gain[l].reshape(1, HEAD_DIM),
            w_co=w_conv_out[l].astype(_BF16),
            w_ao=w_attn_out[l].astype(_BF16),
            w_fo=w_fourier_out[l].astype(_BF16),
            w_o=w_o[l].astype(_BF16),
            ln1_g=ln1_g[l].reshape(1, D_MODEL), ln1_b=ln1_b[l].reshape(1, D_MODEL),
            w_up=w_up[l].astype(_BF16),
            ffn_conv_w=ffn_conv_w[l],
            w_down=w_down[l].astype(_BF16),
            ln2_g=ln2_g[l].reshape(1, D_MODEL), ln2_b=ln2_b[l].reshape(1, D_MODEL),
        ))
    return layers


def _trunk(x, layers):
    s = x.shape[0]
    tm = min(512, s)
    n1 = _DFT_N1
    n2 = s // n1
    cos, sin = _rope_tables(s)
    wa, mcs = _dft_tables(s)
    dft = _channel_dft_matrix()
    xf = x
    xb = x.astype(_BF16)
    for w in layers:
        a = _conv_in(xb, w["w_c3"], w["conv_w"], tm)
        qt, k, vt, fr, fi = _qkvf_in(xb, w["w_qkvf"], w["q_gain"], w["k_gain"], cos, sin, dft, tm)
        yr, yi = _four_a(fr.reshape(n2, n1 * D_FOURIER), fi.reshape(n2, n1 * D_FOURIER), wa,
                         min(4096, n1 * D_FOURIER))
        fmix = _four_c(yr.reshape(n2, n1, D_FOURIER), yi.reshape(n2, n1, D_FOURIER), mcs,
                       min(8, n2)).reshape(s, D_FOURIER)
        score_bound = (1.02 * HEAD_DIM * _Q_SCALE) * jnp.max(jnp.abs(w["q_gain"])) * jnp.max(jnp.abs(w["k_gain"]))
        o = _attn(qt, k, vt, score_bound, min(256, s), min(1024, s // 2), min(1024, s // 2))
        xf, xb = _merge(xb, xf, a, o, fmix, w["w_g"], w["w_co"], w["w_ao"], w["w_fo"], w["w_o"],
                        w["ln1_g"], w["ln1_b"], tm, 512)
        xf, xb = _ffn(xb, xf, w["w_up"], w["ffn_conv_w"], w["w_down"], w["ln2_g"], w["ln2_b"], tm, 512)
    return xf


def kernel(x_prompt, x_sample, w_in, conv_w, q_gain, k_gain, w_conv_out, w_attn_out, w_fourier_out,
           w_o, ln1_g, ln1_b, w_up, ffn_conv_w, w_down, ln2_g, ln2_b):
    layers = _prep_weights(w_in, conv_w, q_gain, k_gain, w_conv_out, w_attn_out, w_fourier_out, w_o,
                           ln1_g, ln1_b, w_up, ffn_conv_w, w_down, ln2_g, ln2_b, DEPTH)
    outs = []
    for x in (x_prompt, x_sample):
        b, s, d = x.shape
        y = jnp.stack([_trunk(x[bi], layers) for bi in range(b)], axis=0)
        outs.append(y)
    return tuple(outs)
```

```python
import functools
import math

import jax
import jax.numpy as jnp
from jax import lax
from jax.experimental import pallas as pl
from jax.experimental.pallas import tpu as pltpu

D_MODEL = 2048
DEPTH = 4
GRID_W = 64
D_CONV = 512
N_HEADS = 8
N_KV_HEADS = 2
HEAD_DIM = 128
GROUP = N_HEADS // N_KV_HEADS
D_ATTN = N_HEADS * HEAD_DIM
D_KV = N_KV_HEADS * HEAD_DIM
ROPE_THETA = 10000.0
N_FOURIER_GROUPS = 4
FOURIER_GROUP = 128
D_FOURIER = N_FOURIER_GROUPS * FOURIER_GROUP
N_BRANCHES = 3
D_FF = 5632
LN_EPS = 1e-5
QK_EPS = 1e-6
DEEPNORM_ALPHA = (2 * DEPTH) ** 0.25

_C_CONV = 0
_C_QKVF = 3 * D_CONV
_C_GATE = _C_QKVF + D_ATTN + 2 * D_KV + D_FOURIER
_W_QKVF = _C_GATE - _C_QKVF

_HALO = 16
_DFT_N1 = 128
_VT_ROWS = HEAD_DIM + _HALO
_Q_SCALE = HEAD_DIM ** -0.5 * math.log2(math.e)
_SCORE_BOUND = 60.0
_VMEM_LIMIT = 58 * 1024 * 1024

_F32 = jnp.float32
_BF16 = jnp.bfloat16


def _dot(a, b):
    return jnp.dot(a, b, preferred_element_type=_F32)


def _layer_norm(y, g, b):
    mu = jnp.mean(y, axis=-1, keepdims=True)
    yc = y - mu
    var = jnp.mean(yc * yc, axis=-1, keepdims=True)
    return yc * lax.rsqrt(var + LN_EPS) * g + b


def _params(sem):
    return pltpu.CompilerParams(dimension_semantics=sem, vmem_limit_bytes=_VMEM_LIMIT)


def _fill_halo_lhs(xs_ref, xm_ref, xp_ref, xn_ref, first, last):
    tm = xm_ref.shape[0]
    half = _HALO // 2
    pv = jnp.where(first, 0.0, xp_ref[...].astype(_F32))
    nx = jnp.where(last, 0.0, xn_ref[...].astype(_F32))
    xs_ref[0:tm, :] = xm_ref[...]
    xs_ref[tm:tm + _HALO, :] = jnp.concatenate([nx[0:half], pv[half:_HALO]], axis=0).astype(_BF16)


def _dwconv3_rows(p_ext, cw, tm):
    n = p_ext.shape[0]
    prev = pltpu.roll(p_ext, 1, 0)[0:tm]
    nxt = pltpu.roll(p_ext, n - 1, 0)[0:tm]
    return prev * cw[0:1, :] + p_ext[0:tm] * cw[1:2, :] + nxt * cw[2:3, :]


def _halo_specs(tm, n_rows, width, ngrid):
    hb = tm // _HALO
    last_blk = n_rows // _HALO - 1
    if ngrid == 1:
        return [
            pl.BlockSpec((tm, width), lambda i: (i, 0)),
            pl.BlockSpec((_HALO, width), lambda i: (jnp.maximum(i * hb - 1, 0), 0)),
            pl.BlockSpec((_HALO, width), lambda i: (jnp.minimum((i + 1) * hb, last_blk), 0)),
        ]
    return [
        pl.BlockSpec((tm, width), lambda i, j: (i, 0)),
        pl.BlockSpec((_HALO, width), lambda i, j: (jnp.maximum(i * hb - 1, 0), 0)),
        pl.BlockSpec((_HALO, width), lambda i, j: (jnp.minimum((i + 1) * hb, last_blk), 0)),
    ]


def _conv_in_kernel(xm_ref, xp_ref, xn_ref, w_ref, cw_ref, a_ref, xs_ref):
    i = pl.program_id(0)
    tm = xm_ref.shape[0]
    _fill_halo_lhs(xs_ref, xm_ref, xp_ref, xn_ref, i == 0, i == pl.num_programs(0) - 1)
    u = _dot(xs_ref[...], w_ref[...])
    p = u[:, D_CONV:2 * D_CONV] * u[:, 2 * D_CONV:3 * D_CONV]
    conv = _dwconv3_rows(p, cw_ref[...], tm)
    a_ref[...] = (u[0:tm, 0:D_CONV] * conv).astype(_BF16)


def _conv_in(xb, w_c3, conv_w, tm):
    s = xb.shape[0]
    return pl.pallas_call(
        _conv_in_kernel,
        grid=(s // tm,),
        in_specs=_halo_specs(tm, s, D_MODEL, 1) + [
            pl.BlockSpec((D_MODEL, 3 * D_CONV), lambda i: (0, 0)),
            pl.BlockSpec((3, D_CONV), lambda i: (0, 0)),
        ],
        out_specs=pl.BlockSpec((tm, D_CONV), lambda i: (i, 0)),
        out_shape=jax.ShapeDtypeStruct((s, D_CONV), _BF16),
        scratch_shapes=[pltpu.VMEM((tm + _HALO, D_MODEL), _BF16)],
        compiler_params=_params(("arbitrary",)),
        name="conv_in",
    )(xb, xb, xb, w_c3, conv_w)


def _qkvf_kernel(x_ref, w_ref, qg_ref, kg_ref, cos_ref, sin_ref, dft_ref,
                 qt_ref, k_ref, vt_ref, fr_ref, fi_ref):
    tm = x_ref.shape[0]
    u = _dot(x_ref[...], w_ref[...])
    cos = cos_ref[...]
    sin = sin_ref[...]

    def norm_rope(xh, gain):
        ms = jnp.mean(xh * xh, axis=-1, keepdims=True)
        y = xh * lax.rsqrt(ms + QK_EPS) * gain
        return y * cos + pltpu.roll(y, HEAD_DIM // 2, 1) * sin

    qg = qg_ref[...]
    kg = kg_ref[...]
    for h in range(N_HEADS):
        sl = slice(h * HEAD_DIM, (h + 1) * HEAD_DIM)
        qt_ref[sl, :] = (norm_rope(u[:, sl], qg) * _Q_SCALE).T.astype(_BF16)
    for h in range(N_KV_HEADS):
        sl = slice(h * HEAD_DIM, (h + 1) * HEAD_DIM)
        k_ref[:, sl] = norm_rope(u[:, D_ATTN + h * HEAD_DIM:D_ATTN + (h + 1) * HEAD_DIM], kg).astype(_BF16)
        v0 = h * _VT_ROWS
        vt_ref[v0:v0 + HEAD_DIM, :] = (
            u[:, D_ATTN + D_KV + h * HEAD_DIM:D_ATTN + D_KV + (h + 1) * HEAD_DIM].T.astype(_BF16))
        vt_ref[v0 + HEAD_DIM:v0 + _VT_ROWS, :] = jnp.ones((_HALO, tm), _BF16)
    f0 = D_ATTN + 2 * D_KV
    dft = dft_ref[...]
    for g in range(N_FOURIER_GROUPS):
        sl = slice(g * FOURIER_GROUP, (g + 1) * FOURIER_GROUP)
        fg = u[:, f0 + g * FOURIER_GROUP:f0 + (g + 1) * FOURIER_GROUP].astype(_BF16)
        z = _dot(fg, dft)
        fr_ref[:, sl] = z[:, 0:FOURIER_GROUP].astype(_BF16)
        fi_ref[:, sl] = z[:, FOURIER_GROUP:2 * FOURIER_GROUP].astype(_BF16)


def _qkvf_in(xb, w_qkvf, q_gain, k_gain, cos, sin, dft, tm):
    s = xb.shape[0]
    row = lambda w: pl.BlockSpec((tm, w), lambda i: (i, 0))
    colt = lambda w: pl.BlockSpec((w, tm), lambda i: (0, i))
    full = lambda a: pl.BlockSpec(a.shape, lambda i: (0,) * a.ndim)
    return pl.pallas_call(
        _qkvf_kernel,
        grid=(s // tm,),
        in_specs=[row(D_MODEL), full(w_qkvf), full(q_gain), full(k_gain),
                  row(HEAD_DIM), row(HEAD_DIM), full(dft)],
        out_specs=[colt(D_ATTN), row(D_KV), colt(N_KV_HEADS * _VT_ROWS), row(D_FOURIER), row(D_FOURIER)],
        out_shape=[jax.ShapeDtypeStruct((D_ATTN, s), _BF16),
                   jax.ShapeDtypeStruct((s, D_KV), _BF16),
                   jax.ShapeDtypeStruct((N_KV_HEADS * _VT_ROWS, s), _BF16),
                   jax.ShapeDtypeStruct((s, D_FOURIER), _BF16),
                   jax.ShapeDtypeStruct((s, D_FOURIER), _BF16)],
        compiler_params=_params(("arbitrary",)),
        name="qkvf_in",
    )(xb, w_qkvf, q_gain, k_gain, cos, sin, dft)


def _four_a_kernel(fr_ref, fi_ref, w_ref, yr_ref, yi_ref):
    n2 = fr_ref.shape[0]
    f = jnp.concatenate([fr_ref[...], fi_ref[...]], axis=0)
    y = _dot(w_ref[...], f)
    yr_ref[...] = y[0:n2].astype(_BF16)
    yi_ref[...] = y[n2:2 * n2].astype(_BF16)


def _four_a(fr2, fi2, wa, tn):
    n2, width = fr2.shape
    col = pl.BlockSpec((n2, tn), lambda j: (0, j))
    return pl.pallas_call(
        _four_a_kernel,
        grid=(width // tn,),
        in_specs=[col, col, pl.BlockSpec((2 * n2, 2 * n2), lambda j: (0, 0))],
        out_specs=[col, col],
        out_shape=[jax.ShapeDtypeStruct((n2, width), _BF16)] * 2,
        compiler_params=_params(("arbitrary",)),
        name="four_a",
    )(fr2, fi2, wa)


def _four_c_kernel(yr_ref, yi_ref, m_ref, o_ref):
    for b in range(yr_ref.shape[0]):
        y = jnp.concatenate([yr_ref[b], yi_ref[b]], axis=0)
        o_ref[:, b * D_FOURIER:(b + 1) * D_FOURIER] = _dot(m_ref[b], y).astype(_BF16)


def _four_c(yr3, yi3, mcs, sb):
    n2, n1, _ = yr3.shape
    dat = pl.BlockSpec((sb, n1, D_FOURIER), lambda j: (j, 0, 0))
    return pl.pallas_call(
        _four_c_kernel,
        grid=(n2 // sb,),
        in_specs=[dat, dat, pl.BlockSpec((sb, n1, 2 * n1), lambda j: (j, 0, 0))],
        out_specs=pl.BlockSpec((n1, sb * D_FOURIER), lambda j: (0, j)),
        out_shape=jax.ShapeDtypeStruct((n1, n2 * D_FOURIER), _BF16),
        compiler_params=_params(("arbitrary",)),
        name="four_c",
    )(yr3, yi3, mcs)


def _dft_tables(s):
    n1 = _DFT_N1
    n2 = s // n1
    two_pi = 2.0 * math.pi
    i2 = jnp.arange(n2, dtype=jnp.int32)
    ang2 = ((i2[:, None] * i2[None, :]) % n2).astype(_F32) * (two_pi / n2)
    c2 = jnp.cos(ang2) * (n2 ** -0.5)
    s2 = jnp.sin(ang2) * (n2 ** -0.5)
    wa = jnp.concatenate([jnp.concatenate([c2, s2], axis=1),
                          jnp.concatenate([-s2, c2], axis=1)], axis=0)
    i1 = jnp.arange(n1, dtype=jnp.int32)
    sidx = n2 * i1[None, :, None] + i2[:, None, None]
    ang1 = ((sidx * i1[None, None, :]) % s).astype(_F32) * (two_pi / s)
    mcs = jnp.concatenate([jnp.cos(ang1), jnp.sin(ang1)], axis=2) * (n1 ** -0.5)
    return wa.astype(_BF16), mcs.astype(_BF16)


def _channel_dft_matrix():
    i = jnp.arange(FOURIER_GROUP, dtype=jnp.int32)
    ang = ((i[:, None] * i[None, :]) % FOURIER_GROUP).astype(_F32) * (2.0 * math.pi / FOURIER_GROUP)
    scale = FOURIER_GROUP ** -0.5
    return jnp.concatenate([jnp.cos(ang) * scale, -jnp.sin(ang) * scale], axis=1).astype(_BF16)


def _rope_tables(s):
    rows = s // GRID_W
    row = jnp.repeat(jnp.arange(rows, dtype=_F32), GRID_W)
    col = jnp.tile(jnp.arange(GRID_W, dtype=_F32), rows)
    half = HEAD_DIM // 2
    inv_freq = ROPE_THETA ** (-jnp.arange(0, half, 2, dtype=_F32) / half)
    ar = row[:, None] * inv_freq
    ac = col[:, None] * inv_freq
    cos = jnp.concatenate([jnp.cos(ar), jnp.cos(ac), jnp.cos(ar), jnp.cos(ac)], axis=1)
    sin = jnp.concatenate([-jnp.sin(ar), -jnp.sin(ac), jnp.sin(ar), jnp.sin(ac)], axis=1)
    return cos, sin


def _pair_major(w):
    lead = w.shape[:-1]
    heads = w.shape[-1] // HEAD_DIM
    w = w.reshape(lead + (heads, 2, 2, HEAD_DIM // 4))
    return jnp.swapaxes(w, -3, -2).reshape(lead + (heads * HEAD_DIM,))


def _attn_kernel(qt_ref, k_ref, vt_ref, o_ref, s0_ref, s1_ref, p0_ref, p1_ref, acc_ref, *, tk):
    tq = qt_ref.shape[1]
    n_chunks = k_ref.shape[0] // tk

    def scores(j, s_ref):
        kc = k_ref[pl.ds(pl.multiple_of(j * tk, tk), tk), :]
        cmax = []
        for g in range(GROUP):
            st = _dot(kc, qt_ref[g * HEAD_DIM:(g + 1) * HEAD_DIM, :])
            s_ref[:, g * tq:(g + 1) * tq] = st
            cmax.append(jnp.max(st, axis=0, keepdims=True))
        return jnp.concatenate(cmax, axis=1)

    def softmax(s_ref, p_ref, m, cmax):
        m_new = jnp.maximum(m, cmax)
        alpha = jnp.exp2(m - m_new)
        p_ref[...] = jnp.exp2((s_ref[...] - m_new).astype(_BF16))
        return m_new, alpha

    def accumulate(j, p_ref, alpha):
        vc = vt_ref[:, pl.ds(pl.multiple_of(j * tk, tk), tk)]
        acc_ref[...] = alpha * acc_ref[...] + _dot(vc, p_ref[...])

    cmax0 = scores(0, s0_ref)
    p1_ref[...] = jnp.zeros_like(p1_ref)
    acc_ref[...] = jnp.zeros_like(acc_ref)

    def body(i, carry):
        m, alpha_prev, cmax_even = carry
        j = 2 * i
        cmax_odd = scores(j + 1, s1_ref)
        m, alpha_even = softmax(s0_ref, p0_ref, m, cmax_even)
        accumulate(jnp.maximum(j - 1, 0), p1_ref, alpha_prev)
        cmax_even = scores(jnp.minimum(j + 2, n_chunks - 1), s0_ref)
        m, alpha_odd = softmax(s1_ref, p1_ref, m, cmax_odd)
        accumulate(j, p0_ref, alpha_even)
        return m, alpha_odd, cmax_even

    m0 = jnp.full((1, GROUP * tq), -jnp.inf, _F32)
    a0 = jnp.ones((1, GROUP * tq), _F32)
    _, alpha_last, _ = lax.fori_loop(0, n_chunks // 2, body, (m0, a0, cmax0))
    accumulate(n_chunks - 1, p1_ref, alpha_last)
    out = acc_ref[0:HEAD_DIM, :] / acc_ref[HEAD_DIM:HEAD_DIM + 1, :]
    for g in range(GROUP):
        o_ref[:, g * HEAD_DIM:(g + 1) * HEAD_DIM] = out[:, g * tq:(g + 1) * tq].T.astype(_BF16)


def _attn_bounded_kernel(qt_ref, k_ref, vt_ref, o_ref, p0_ref, p1_ref, acc_ref, *, tk):
    tq = qt_ref.shape[1]
    mq = GROUP * tq
    n_chunks = k_ref.shape[0] // tk
    sub = 8

    def probs(j, p_ref, l8):
        kc = k_ref[pl.ds(pl.multiple_of(j * tk, tk), tk), :]
        parts = []
        for g in range(GROUP):
            p = jnp.exp2(_dot(kc, qt_ref[g * HEAD_DIM:(g + 1) * HEAD_DIM, :]))
            p_ref[:, g * tq:(g + 1) * tq] = p.astype(_BF16)
            parts.append(jnp.sum(p.reshape(tk // sub, sub, tq), axis=0))
        return l8 + jnp.concatenate(parts, axis=1)

    def accumulate(j, p_ref):
        vc = vt_ref[0:HEAD_DIM, pl.ds(pl.multiple_of(j * tk, tk), tk)]
        acc_ref[...] += _dot(vc, p_ref[...])

    acc_ref[...] = jnp.zeros_like(acc_ref)
    l8 = probs(0, p0_ref, jnp.zeros((sub, mq), _F32))

    def body(i, l8):
        j = 2 * i
        l8 = probs(j + 1, p1_ref, l8)
        accumulate(j, p0_ref)
        l8 = probs(j + 2, p0_ref, l8)
        accumulate(j + 1, p1_ref)
        return l8

    l8 = lax.fori_loop(0, n_chunks // 2 - 1, body, l8)
    l8 = probs(n_chunks - 1, p1_ref, l8)
    accumulate(n_chunks - 2, p0_ref)
    accumulate(n_chunks - 1, p1_ref)
    out = acc_ref[...] / jnp.sum(l8, axis=0, keepdims=True)
    for g in range(GROUP):
        o_ref[:, g * HEAD_DIM:(g + 1) * HEAD_DIM] = out[:, g * tq:(g + 1) * tq].T.astype(_BF16)


def _attn(qt, k, vt, score_bound, tq, tk, tk_bounded):
    s = k.shape[0]
    assert s % (2 * tk) == 0 and s % (2 * tk_bounded) == 0 and s % tq == 0
    mq = GROUP * tq
    bufs = lambda rows, dt: [pltpu.VMEM((rows, mq), dt), pltpu.VMEM((rows, mq), dt)]
    common = dict(
        grid=(N_KV_HEADS, s // tq),
        in_specs=[pl.BlockSpec((GROUP * HEAD_DIM, tq), lambda h, i: (h, i)),
                  pl.BlockSpec((s, HEAD_DIM), lambda h, i: (0, h)),
                  pl.BlockSpec((_VT_ROWS, s), lambda h, i: (h, 0))],
        out_specs=pl.BlockSpec((tq, GROUP * HEAD_DIM), lambda h, i: (i, h)),
        out_shape=jax.ShapeDtypeStruct((s, D_ATTN), _BF16),
        compiler_params=_params(("arbitrary", "arbitrary")),
    )
    bounded = pl.pallas_call(
        functools.partial(_attn_bounded_kernel, tk=tk_bounded),
        scratch_shapes=bufs(tk_bounded, _BF16) + [pltpu.VMEM((HEAD_DIM, mq), _F32)],
        name="attn_bounded", **common)
    online = pl.pallas_call(
        functools.partial(_attn_kernel, tk=tk),
        scratch_shapes=bufs(tk, _F32) + bufs(tk, _BF16) + [pltpu.VMEM((_VT_ROWS, mq), _F32)],
        name="attn", **common)
    return lax.cond(score_bound <= _SCORE_BOUND, bounded, online, qt, k, vt)


def _merge_kernel(xb_ref, xf_ref, a_ref, o_ref, fr_ref, wg0_ref, wg1_ref, wg2_ref,
                  wco_ref, wao_ref, wfo_ref, wo_ref, g_ref, b_ref, outf_ref, outb_ref):
    c = pl.program_id(1)
    xb = xb_ref[...]
    m = jax.nn.sigmoid(_dot(xb, wg0_ref[...])) * _dot(a_ref[...], wco_ref[...])
    m = m + jax.nn.sigmoid(_dot(xb, wg1_ref[...])) * _dot(o_ref[...], wao_ref[...])
    m = m + jax.nn.sigmoid(_dot(xb, wg2_ref[...])) * _dot(fr_ref[...], wfo_ref[...])

    @pl.when(c == 0)
    def _():
        outf_ref[...] = jnp.zeros_like(outf_ref)

    outf_ref[...] += _dot(m.astype(_BF16), wo_ref[...])

    @pl.when(c == pl.num_programs(1) - 1)
    def _():
        y = _layer_norm(DEEPNORM_ALPHA * xf_ref[...] + outf_ref[...], g_ref[...], b_ref[...])
        outf_ref[...] = y
        outb_ref[...] = y.astype(_BF16)


def _merge(xb, xf, a, o, fr, w_g, w_co, w_ao, w_fo, w_o, ln_g, ln_b, tm, tn):
    s = xb.shape[0]
    nc = D_MODEL // tn
    row = lambda w: pl.BlockSpec((tm, w), lambda i, c: (i, 0))
    colw = lambda k: pl.BlockSpec((k, tn), lambda i, c: (0, c))
    gate = lambda b: pl.BlockSpec((D_MODEL, tn), lambda i, c: (0, b * nc + c))
    vec = pl.BlockSpec((1, D_MODEL), lambda i, c: (0, 0))
    return pl.pallas_call(
        _merge_kernel,
        grid=(s // tm, nc),
        in_specs=[row(D_MODEL), row(D_MODEL), row(D_CONV), row(D_ATTN), row(D_FOURIER),
                  gate(0), gate(1), gate(2),
                  colw(D_CONV), colw(D_ATTN), colw(D_FOURIER),
                  pl.BlockSpec((tn, D_MODEL), lambda i, c: (c, 0)), vec, vec],
        out_specs=[row(D_MODEL), row(D_MODEL)],
        out_shape=[jax.ShapeDtypeStruct((s, D_MODEL), _F32), jax.ShapeDtypeStruct((s, D_MODEL), _BF16)],
        compiler_params=_params(("arbitrary", "arbitrary")),
        name="merge",
    )(xb, xf, a, o, fr, w_g, w_g, w_g, w_co, w_ao, w_fo, w_o, ln_g, ln_b)


def _ffn_kernel(xm_ref, xp_ref, xn_ref, xf_ref, wg_ref, wv_ref, cw_ref, wd_ref, g_ref, b_ref,
                outf_ref, outb_ref, xs_ref):
    i = pl.program_id(0)
    j = pl.program_id(1)
    tm = xm_ref.shape[0]

    @pl.when(j == 0)
    def _():
        _fill_halo_lhs(xs_ref, xm_ref, xp_ref, xn_ref, i == 0, i == pl.num_programs(0) - 1)
        outf_ref[...] = jnp.zeros_like(outf_ref)

    hg = _dot(xs_ref[...], wg_ref[...])
    hv = _dot(xs_ref[0:tm, :], wv_ref[...])
    h = jax.nn.silu(_dwconv3_rows(hg, cw_ref[...], tm)) * hv
    outf_ref[...] += _dot(h.astype(_BF16), wd_ref[...])

    @pl.when(j == pl.num_programs(1) - 1)
    def _():
        y = _layer_norm(DEEPNORM_ALPHA * xf_ref[...] + outf_ref[...], g_ref[...], b_ref[...])
        outf_ref[...] = y
        outb_ref[...] = y.astype(_BF16)


def _ffn(xb, xf, w_up, ffn_conv_w, w_down, ln_g, ln_b, tm, tn):
    s = xb.shape[0]
    nj = D_FF // tn
    row = pl.BlockSpec((tm, D_MODEL), lambda i, j: (i, 0))
    vec = pl.BlockSpec((1, D_MODEL), lambda i, j: (0, 0))
    return pl.pallas_call(
        _ffn_kernel,
        grid=(s // tm, nj),
        in_specs=_halo_specs(tm, s, D_MODEL, 2) + [
            row,
            pl.BlockSpec((D_MODEL, tn), lambda i, j: (0, j)),
            pl.BlockSpec((D_MODEL, tn), lambda i, j: (0, nj + j)),
            pl.BlockSpec((3, tn), lambda i, j: (0, j)),
            pl.BlockSpec((tn, D_MODEL), lambda i, j: (j, 0)),
            vec, vec],
        out_specs=[row, row],
        out_shape=[jax.ShapeDtypeStruct((s, D_MODEL), _F32), jax.ShapeDtypeStruct((s, D_MODEL), _BF16)],
        scratch_shapes=[pltpu.VMEM((tm + _HALO, D_MODEL), _BF16)],
        compiler_params=_params(("arbitrary", "arbitrary")),
        name="ffn",
    )(xb, xb, xb, xf, w_up, w_up, ffn_conv_w, w_down, ln_g, ln_b)


def _prep_weights(w_in, conv_w, q_gain, k_gain, w_conv_out, w_attn_out, w_fourier_out, w_o,
                  ln1_g, ln1_b, w_up, ffn_conv_w, w_down, ln2_g, ln2_b, depth):
    layers = []
    for l in range(depth):
        layers.append(dict(
            w_c3=w_in[l, :, _C_CONV:_C_QKVF].astype(_BF16),
            w_qkvf=jnp.concatenate(
                [_pair_major(w_in[l, :, _C_QKVF:_C_QKVF + D_ATTN + D_KV]),
                 w_in[l, :, _C_QKVF + D_ATTN + D_KV:_C_GATE]], axis=1).astype(_BF16),
            w_g=w_in[l, :, _C_GATE:].astype(_BF16),
            conv_w=conv_w[l],
            q_gain=_pair_major(q_gain[l]).reshape(1, HEAD_DIM),
            k_gain=_pair_major(k_gain[l]).reshape(1, HEAD_DIM),
            w_co=w_conv_out[l].astype(_BF16),
            w_ao=w_attn_out[l].astype(_BF16),
            w_fo=w_fourier_out[l].astype(_BF16),
            w_o=w_o[l].astype(_BF16),
            ln1_g=ln1_g[l].reshape(1, D_MODEL), ln1_b=ln1_b[l].reshape(1, D_MODEL),
            w_up=w_up[l].astype(_BF16),
            ffn_conv_w=ffn_conv_w[l],
            w_down=w_down[l].astype(_BF16),
            ln2_g=ln2_g[l].reshape(1, D_MODEL), ln2_b=ln2_b[l].reshape(1, D_MODEL),
        ))
    return layers


def _trunk(x, layers):
    s = x.shape[0]
    tm = min(512, s)
    n1 = _DFT_N1
    n2 = s // n1
    cos, sin = _rope_tables(s)
    wa, mcs = _dft_tables(s)
    dft = _channel_dft_matrix()
    xf = x
    xb = x.astype(_BF16)
    for w in layers:
        a = _conv_in(xb, w["w_c3"], w["conv_w"], tm)
        qt, k, vt, fr, fi = _qkvf_in(xb, w["w_qkvf"], w["q_gain"], w["k_gain"], cos, sin, dft, tm)
        yr, yi = _four_a(fr.reshape(n2, n1 * D_FOURIER), fi.reshape(n2, n1 * D_FOURIER), wa,
                         min(4096, n1 * D_FOURIER))
        fmix = _four_c(yr.reshape(n2, n1, D_FOURIER), yi.reshape(n2, n1, D_FOURIER), mcs,
                       min(8, n2)).reshape(s, D_FOURIER)
        score_bound = (1.02 * HEAD_DIM * _Q_SCALE) * jnp.max(jnp.abs(w["q_gain"])) * jnp.max(jnp.abs(w["k_gain"]))
        o = _attn(qt, k, vt, score_bound, min(256, s), min(1024, s // 2), min(2048, s // 2))
        xf, xb = _merge(xb, xf, a, o, fmix, w["w_g"], w["w_co"], w["w_ao"], w["w_fo"], w["w_o"],
                        w["ln1_g"], w["ln1_b"], tm, 512)
        xf, xb = _ffn(xb, xf, w["w_up"], w["ffn_conv_w"], w["w_down"], w["ln2_g"], w["ln2_b"], tm, 512)
    return xf


def kernel(x_prompt, x_sample, w_in, conv_w, q_gain, k_gain, w_conv_out, w_attn_out, w_fourier_out,
           w_o, ln1_g, ln1_b, w_up, ffn_conv_w, w_down, ln2_g, ln2_b):
    layers = _prep_weights(w_in, conv_w, q_gain, k_gain, w_conv_out, w_attn_out, w_fourier_out, w_o,
                           ln1_g, ln1_b, w_up, ffn_conv_w, w_down, ln2_g, ln2_b, DEPTH)
    outs = []
    for x in (x_prompt, x_sample):
        b, s, d = x.shape
        rows = x.reshape(b * s, d)
        ys = [_trunk(rows[bi * s:(bi + 1) * s], layers) for bi in range(b)]
        outs.append((ys[0] if b == 1 else jnp.concatenate(ys, axis=0)).reshape(b, s, d))
    return tuple(outs)
```

```python
import functools
import math

import jax
import jax.numpy as jnp
from jax import lax
from jax.experimental import pallas as pl
from jax.experimental.pallas import tpu as pltpu

D_MODEL = 2048
DEPTH = 4
GRID_W = 64
D_CONV = 512
N_HEADS = 8
N_KV_HEADS = 2
HEAD_DIM = 128
GROUP = N_HEADS // N_KV_HEADS
D_ATTN = N_HEADS * HEAD_DIM
D_KV = N_KV_HEADS * HEAD_DIM
ROPE_THETA = 10000.0
N_FOURIER_GROUPS = 4
FOURIER_GROUP = 128
D_FOURIER = N_FOURIER_GROUPS * FOURIER_GROUP
N_BRANCHES = 3
D_FF = 5632
LN_EPS = 1e-5
QK_EPS = 1e-6
DEEPNORM_ALPHA = (2 * DEPTH) ** 0.25

_C_CONV = 0
_C_QKVF = 3 * D_CONV
_C_GATE = _C_QKVF + D_ATTN + 2 * D_KV + D_FOURIER
_W_QKVF = _C_GATE - _C_QKVF

_HALO = 16
_DFT_N1 = 128
_VT_ROWS = HEAD_DIM + _HALO
_Q_SCALE = HEAD_DIM ** -0.5 * math.log2(math.e)
_SCORE_BOUND = 60.0
_VMEM_LIMIT = 58 * 1024 * 1024

_F32 = jnp.float32
_BF16 = jnp.bfloat16


def _dot(a, b):
    return jnp.dot(a, b, preferred_element_type=_F32)


def _layer_norm(y, g, b):
    mu = jnp.mean(y, axis=-1, keepdims=True)
    yc = y - mu
    var = jnp.mean(yc * yc, axis=-1, keepdims=True)
    return yc * lax.rsqrt(var + LN_EPS) * g + b


def _params(sem):
    return pltpu.CompilerParams(dimension_semantics=sem, vmem_limit_bytes=_VMEM_LIMIT)


def _fill_halo_lhs(xs_ref, xm_ref, xp_ref, xn_ref, first, last):
    tm = xm_ref.shape[0]
    half = _HALO // 2
    pv = jnp.where(first, 0.0, xp_ref[...].astype(_F32))
    nx = jnp.where(last, 0.0, xn_ref[...].astype(_F32))
    xs_ref[0:tm, :] = xm_ref[...]
    xs_ref[tm:tm + _HALO, :] = jnp.concatenate([nx[0:half], pv[half:_HALO]], axis=0).astype(_BF16)


def _dwconv3_rows(p_ext, cw, tm):
    n = p_ext.shape[0]
    prev = pltpu.roll(p_ext, 1, 0)[0:tm]
    nxt = pltpu.roll(p_ext, n - 1, 0)[0:tm]
    return prev * cw[0:1, :] + p_ext[0:tm] * cw[1:2, :] + nxt * cw[2:3, :]


def _halo_specs(tm, n_rows, width, ngrid):
    hb = tm // _HALO
    last_blk = n_rows // _HALO - 1
    if ngrid == 1:
        return [
            pl.BlockSpec((tm, width), lambda i: (i, 0)),
            pl.BlockSpec((_HALO, width), lambda i: (jnp.maximum(i * hb - 1, 0), 0)),
            pl.BlockSpec((_HALO, width), lambda i: (jnp.minimum((i + 1) * hb, last_blk), 0)),
        ]
    return [
        pl.BlockSpec((tm, width), lambda i, j: (i, 0)),
        pl.BlockSpec((_HALO, width), lambda i, j: (jnp.maximum(i * hb - 1, 0), 0)),
        pl.BlockSpec((_HALO, width), lambda i, j: (jnp.minimum((i + 1) * hb, last_blk), 0)),
    ]


def _conv_in_kernel(xm_ref, xp_ref, xn_ref, w_ref, cw_ref, a_ref, xs_ref):
    i = pl.program_id(0)
    tm = xm_ref.shape[0]
    _fill_halo_lhs(xs_ref, xm_ref, xp_ref, xn_ref, i == 0, i == pl.num_programs(0) - 1)
    u = _dot(xs_ref[...], w_ref[...])
    p = u[:, D_CONV:2 * D_CONV] * u[:, 2 * D_CONV:3 * D_CONV]
    conv = _dwconv3_rows(p, cw_ref[...], tm)
    a_ref[...] = (u[0:tm, 0:D_CONV] * conv).astype(_BF16)


def _conv_in(xb, w_in, conv_w, l, tm):
    s = xb.shape[0]
    return pl.pallas_call(
        _conv_in_kernel,
        grid=(s // tm,),
        in_specs=_halo_specs(tm, s, D_MODEL, 1) + [
            pl.BlockSpec((None, D_MODEL, 3 * D_CONV), lambda i: (l, 0, _C_CONV // (3 * D_CONV))),
            pl.BlockSpec((None, 3, D_CONV), lambda i: (l, 0, 0)),
        ],
        out_specs=pl.BlockSpec((tm, D_CONV), lambda i: (i, 0)),
        out_shape=jax.ShapeDtypeStruct((s, D_CONV), _BF16),
        scratch_shapes=[pltpu.VMEM((tm + _HALO, D_MODEL), _BF16)],
        compiler_params=_params(("arbitrary",)),
        name="conv_in",
    )(xb, xb, xb, w_in, conv_w)


def _qkvf_kernel(x_ref, w_ref, qg_ref, kg_ref, cos_ref, sin_ref, dft_ref,
                 qt_ref, k_ref, vt_ref, fr_ref, fi_ref):
    tm = x_ref.shape[0]
    u = _dot(x_ref[...], w_ref[...])
    cos = cos_ref[...]
    sin = sin_ref[...]

    def norm_rope(xh, gain):
        ms = jnp.mean(xh * xh, axis=-1, keepdims=True)
        y = xh * lax.rsqrt(ms + QK_EPS) * gain
        return y * cos + pltpu.roll(y, HEAD_DIM // 2, 1) * sin

    qg = qg_ref[...]
    kg = kg_ref[...]
    for h in range(N_HEADS):
        sl = slice(h * HEAD_DIM, (h + 1) * HEAD_DIM)
        qt_ref[sl, :] = (norm_rope(u[:, sl], qg) * _Q_SCALE).T.astype(_BF16)
    for h in range(N_KV_HEADS):
        sl = slice(h * HEAD_DIM, (h + 1) * HEAD_DIM)
        k_ref[:, sl] = norm_rope(u[:, D_ATTN + h * HEAD_DIM:D_ATTN + (h + 1) * HEAD_DIM], kg).astype(_BF16)
        v0 = h * _VT_ROWS
        vt_ref[v0:v0 + HEAD_DIM, :] = (
            u[:, D_ATTN + D_KV + h * HEAD_DIM:D_ATTN + D_KV + (h + 1) * HEAD_DIM].T.astype(_BF16))
        vt_ref[v0 + HEAD_DIM:v0 + _VT_ROWS, :] = jnp.ones((_HALO, tm), _BF16)
    f0 = D_ATTN + 2 * D_KV
    dft = dft_ref[...]
    for g in range(N_FOURIER_GROUPS):
        sl = slice(g * FOURIER_GROUP, (g + 1) * FOURIER_GROUP)
        fg = u[:, f0 + g * FOURIER_GROUP:f0 + (g + 1) * FOURIER_GROUP].astype(_BF16)
        z = _dot(fg, dft)
        fr_ref[:, sl] = z[:, 0:FOURIER_GROUP].astype(_BF16)
        fi_ref[:, sl] = z[:, FOURIER_GROUP:2 * FOURIER_GROUP].astype(_BF16)


def _qkvf_in(xb, w_qkvf, q_gain, k_gain, cos, sin, dft, l, tm):
    s = xb.shape[0]
    row = lambda w: pl.BlockSpec((tm, w), lambda i: (i, 0))
    colt = lambda w: pl.BlockSpec((w, tm), lambda i: (0, i))
    layer = lambda a: pl.BlockSpec((None,) + a.shape[1:], lambda i: (l,) + (0,) * (a.ndim - 1))
    full = lambda a: pl.BlockSpec(a.shape, lambda i: (0,) * a.ndim)
    return pl.pallas_call(
        _qkvf_kernel,
        grid=(s // tm,),
        in_specs=[row(D_MODEL), layer(w_qkvf), layer(q_gain), layer(k_gain),
                  row(HEAD_DIM), row(HEAD_DIM), full(dft)],
        out_specs=[colt(D_ATTN), row(D_KV), colt(N_KV_HEADS * _VT_ROWS), row(D_FOURIER), row(D_FOURIER)],
        out_shape=[jax.ShapeDtypeStruct((D_ATTN, s), _BF16),
                   jax.ShapeDtypeStruct((s, D_KV), _BF16),
                   jax.ShapeDtypeStruct((N_KV_HEADS * _VT_ROWS, s), _BF16),
                   jax.ShapeDtypeStruct((s, D_FOURIER), _BF16),
                   jax.ShapeDtypeStruct((s, D_FOURIER), _BF16)],
        compiler_params=_params(("arbitrary",)),
        name="qkvf_in",
    )(xb, w_qkvf, q_gain, k_gain, cos, sin, dft)


def _four_a_kernel(fr_ref, fi_ref, w_ref, yr_ref, yi_ref):
    n2 = fr_ref.shape[0]
    f = jnp.concatenate([fr_ref[...], fi_ref[...]], axis=0)
    y = _dot(w_ref[...], f)
    yr_ref[...] = y[0:n2].astype(_BF16)
    yi_ref[...] = y[n2:2 * n2].astype(_BF16)


def _four_a(fr2, fi2, wa, tn):
    n2, width = fr2.shape
    col = pl.BlockSpec((n2, tn), lambda j: (0, j))
    return pl.pallas_call(
        _four_a_kernel,
        grid=(width // tn,),
        in_specs=[col, col, pl.BlockSpec((2 * n2, 2 * n2), lambda j: (0, 0))],
        out_specs=[col, col],
        out_shape=[jax.ShapeDtypeStruct((n2, width), _BF16)] * 2,
        compiler_params=_params(("arbitrary",)),
        name="four_a",
    )(fr2, fi2, wa)


def _four_c_kernel(yr_ref, yi_ref, m_ref, o_ref):
    for b in range(yr_ref.shape[0]):
        y = jnp.concatenate([yr_ref[b], yi_ref[b]], axis=0)
        o_ref[:, b * D_FOURIER:(b + 1) * D_FOURIER] = _dot(m_ref[b], y).astype(_BF16)


def _four_c(yr3, yi3, mcs, sb):
    n2, n1, _ = yr3.shape
    dat = pl.BlockSpec((sb, n1, D_FOURIER), lambda j: (j, 0, 0))
    return pl.pallas_call(
        _four_c_kernel,
        grid=(n2 // sb,),
        in_specs=[dat, dat, pl.BlockSpec((sb, n1, 2 * n1), lambda j: (j, 0, 0))],
        out_specs=pl.BlockSpec((n1, sb * D_FOURIER), lambda j: (0, j)),
        out_shape=jax.ShapeDtypeStruct((n1, n2 * D_FOURIER), _BF16),
        compiler_params=_params(("arbitrary",)),
        name="four_c",
    )(yr3, yi3, mcs)


def _dft_tables(s):
    n1 = _DFT_N1
    n2 = s // n1
    two_pi = 2.0 * math.pi
    i2 = jnp.arange(n2, dtype=jnp.int32)
    ang2 = ((i2[:, None] * i2[None, :]) % n2).astype(_F32) * (two_pi / n2)
    c2 = jnp.cos(ang2) * (n2 ** -0.5)
    s2 = jnp.sin(ang2) * (n2 ** -0.5)
    wa = jnp.concatenate([jnp.concatenate([c2, s2], axis=1),
                          jnp.concatenate([-s2, c2], axis=1)], axis=0)
    i1 = jnp.arange(n1, dtype=jnp.int32)
    ang_a = ((i1[:, None] * i1[None, :]) % n1).astype(_F32) * (two_pi / n1)
    ang_b = (i2[:, None] * i1[None, :]).astype(_F32) * (two_pi / s)
    ca, sa = jnp.cos(ang_a)[None], jnp.sin(ang_a)[None]
    cb, sb = jnp.cos(ang_b)[:, None, :], jnp.sin(ang_b)[:, None, :]
    mcs = jnp.concatenate([ca * cb - sa * sb, sa * cb + ca * sb], axis=2) * (n1 ** -0.5)
    return wa.astype(_BF16), mcs.astype(_BF16)


def _channel_dft_matrix():
    i = jnp.arange(FOURIER_GROUP, dtype=jnp.int32)
    ang = ((i[:, None] * i[None, :]) % FOURIER_GROUP).astype(_F32) * (2.0 * math.pi / FOURIER_GROUP)
    scale = FOURIER_GROUP ** -0.5
    return jnp.concatenate([jnp.cos(ang) * scale, -jnp.sin(ang) * scale], axis=1).astype(_BF16)


def _rope_tables(s):
    rows = s // GRID_W
    row = jnp.repeat(jnp.arange(rows, dtype=_F32), GRID_W)
    col = jnp.tile(jnp.arange(GRID_W, dtype=_F32), rows)
    half = HEAD_DIM // 2
    inv_freq = ROPE_THETA ** (-jnp.arange(0, half, 2, dtype=_F32) / half)
    ar = row[:, None] * inv_freq
    ac = col[:, None] * inv_freq
    cos = jnp.concatenate([jnp.cos(ar), jnp.cos(ac), jnp.cos(ar), jnp.cos(ac)], axis=1)
    sin = jnp.concatenate([-jnp.sin(ar), -jnp.sin(ac), jnp.sin(ar), jnp.sin(ac)], axis=1)
    return cos, sin


def _pair_major(w):
    lead = w.shape[:-1]
    heads = w.shape[-1] // HEAD_DIM
    w = w.reshape(lead + (heads, 2, 2, HEAD_DIM // 4))
    return jnp.swapaxes(w, -3, -2).reshape(lead + (heads * HEAD_DIM,))


def _attn_kernel(qt_ref, k_ref, vt_ref, o_ref, s0_ref, s1_ref, p0_ref, p1_ref, acc_ref, *, tk):
    tq = qt_ref.shape[1]
    n_chunks = k_ref.shape[0] // tk

    def scores(j, s_ref):
        kc = k_ref[pl.ds(pl.multiple_of(j * tk, tk), tk), :]
        cmax = []
        for g in range(GROUP):
            st = _dot(kc, qt_ref[g * HEAD_DIM:(g + 1) * HEAD_DIM, :])
            s_ref[:, g * tq:(g + 1) * tq] = st
            cmax.append(jnp.max(st, axis=0, keepdims=True))
        return jnp.concatenate(cmax, axis=1)

    def softmax(s_ref, p_ref, m, cmax):
        m_new = jnp.maximum(m, cmax)
        alpha = jnp.exp2(m - m_new)
        p_ref[...] = jnp.exp2((s_ref[...] - m_new).astype(_BF16))
        return m_new, alpha

    def accumulate(j, p_ref, alpha):
        vc = vt_ref[:, pl.ds(pl.multiple_of(j * tk, tk), tk)]
        acc_ref[...] = alpha * acc_ref[...] + _dot(vc, p_ref[...])

    cmax0 = scores(0, s0_ref)
    p1_ref[...] = jnp.zeros_like(p1_ref)
    acc_ref[...] = jnp.zeros_like(acc_ref)

    def body(i, carry):
        m, alpha_prev, cmax_even = carry
        j = 2 * i
        cmax_odd = scores(j + 1, s1_ref)
        m, alpha_even = softmax(s0_ref, p0_ref, m, cmax_even)
        accumulate(jnp.maximum(j - 1, 0), p1_ref, alpha_prev)
        cmax_even = scores(jnp.minimum(j + 2, n_chunks - 1), s0_ref)
        m, alpha_odd = softmax(s1_ref, p1_ref, m, cmax_odd)
        accumulate(j, p0_ref, alpha_even)
        return m, alpha_odd, cmax_even

    m0 = jnp.full((1, GROUP * tq), -jnp.inf, _F32)
    a0 = jnp.ones((1, GROUP * tq), _F32)
    _, alpha_last, _ = lax.fori_loop(0, n_chunks // 2, body, (m0, a0, cmax0))
    accumulate(n_chunks - 1, p1_ref, alpha_last)
    out = acc_ref[0:HEAD_DIM, :] / acc_ref[HEAD_DIM:HEAD_DIM + 1, :]
    for g in range(GROUP):
        o_ref[:, g * HEAD_DIM:(g + 1) * HEAD_DIM] = out[:, g * tq:(g + 1) * tq].T.astype(_BF16)


def _attn_bounded_kernel(qt_ref, k_ref, vt_ref, o_ref, p0_ref, p1_ref, acc_ref, *, tk):
    tq = qt_ref.shape[1]
    mq = GROUP * tq
    n_chunks = k_ref.shape[0] // tk
    sub = 8

    def probs(j, p_ref, l8):
        kc = k_ref[pl.ds(pl.multiple_of(j * tk, tk), tk), :]
        parts = []
        for g in range(GROUP):
            p = jnp.exp2(_dot(kc, qt_ref[g * HEAD_DIM:(g + 1) * HEAD_DIM, :]))
            p_ref[:, g * tq:(g + 1) * tq] = p.astype(_BF16)
            parts.append(jnp.sum(p.reshape(tk // sub, sub, tq), axis=0))
        return l8 + jnp.concatenate(parts, axis=1)

    def accumulate(j, p_ref):
        vc = vt_ref[0:HEAD_DIM, pl.ds(pl.multiple_of(j * tk, tk), tk)]
        acc_ref[...] += _dot(vc, p_ref[...])

    acc_ref[...] = jnp.zeros_like(acc_ref)
    l8 = probs(0, p0_ref, jnp.zeros((sub, mq), _F32))

    def body(i, l8):
        j = 2 * i
        l8 = probs(j + 1, p1_ref, l8)
        accumulate(j, p0_ref)
        l8 = probs(j + 2, p0_ref, l8)
        accumulate(j + 1, p1_ref)
        return l8

    l8 = lax.fori_loop(0, n_chunks // 2 - 1, body, l8)
    l8 = probs(n_chunks - 1, p1_ref, l8)
    accumulate(n_chunks - 2, p0_ref)
    accumulate(n_chunks - 1, p1_ref)
    out = acc_ref[...] / jnp.sum(l8, axis=0, keepdims=True)
    for g in range(GROUP):
        o_ref[:, g * HEAD_DIM:(g + 1) * HEAD_DIM] = out[:, g * tq:(g + 1) * tq].T.astype(_BF16)


def _attn(qt, k, vt, score_bound, tq, tk, tk_bounded):
    s = k.shape[0]
    assert s % (2 * tk) == 0 and s % (2 * tk_bounded) == 0 and s % tq == 0
    mq = GROUP * tq
    bufs = lambda rows, dt: [pltpu.VMEM((rows, mq), dt), pltpu.VMEM((rows, mq), dt)]
    common = dict(
        grid=(N_KV_HEADS, s // tq),
        in_specs=[pl.BlockSpec((GROUP * HEAD_DIM, tq), lambda h, i: (h, i)),
                  pl.BlockSpec((s, HEAD_DIM), lambda h, i: (0, h)),
                  pl.BlockSpec((_VT_ROWS, s), lambda h, i: (h, 0))],
        out_specs=pl.BlockSpec((tq, GROUP * HEAD_DIM), lambda h, i: (i, h)),
        out_shape=jax.ShapeDtypeStruct((s, D_ATTN), _BF16),
        compiler_params=_params(("arbitrary", "arbitrary")),
    )
    bounded = pl.pallas_call(
        functools.partial(_attn_bounded_kernel, tk=tk_bounded),
        scratch_shapes=bufs(tk_bounded, _BF16) + [pltpu.VMEM((HEAD_DIM, mq), _F32)],
        name="attn_bounded", **common)
    online = pl.pallas_call(
        functools.partial(_attn_kernel, tk=tk),
        scratch_shapes=bufs(tk, _F32) + bufs(tk, _BF16) + [pltpu.VMEM((_VT_ROWS, mq), _F32)],
        name="attn", **common)
    return lax.cond(score_bound <= _SCORE_BOUND, bounded, online, qt, k, vt)


def _merge_kernel(xb_ref, xf_ref, a_ref, o_ref, fr_ref, wg0_ref, wg1_ref, wg2_ref,
                  wco_ref, wao_ref, wfo_ref, wo_ref, g_ref, b_ref, outf_ref, outb_ref):
    c = pl.program_id(1)
    xb = xb_ref[...]
    m = jax.nn.sigmoid(_dot(xb, wg0_ref[...])) * _dot(a_ref[...], wco_ref[...])
    m = m + jax.nn.sigmoid(_dot(xb, wg1_ref[...])) * _dot(o_ref[...], wao_ref[...])
    m = m + jax.nn.sigmoid(_dot(xb, wg2_ref[...])) * _dot(fr_ref[...], wfo_ref[...])

    @pl.when(c == 0)
    def _():
        outf_ref[...] = jnp.zeros_like(outf_ref)

    outf_ref[...] += _dot(m.astype(_BF16), wo_ref[...])

    @pl.when(c == pl.num_programs(1) - 1)
    def _():
        y = _layer_norm(DEEPNORM_ALPHA * xf_ref[...] + outf_ref[...], g_ref[...], b_ref[...])
        outf_ref[...] = y
        outb_ref[...] = y.astype(_BF16)


def _merge(xb, xf, a, o, fr, w_in, w_co, w_ao, w_fo, w_o, ln_g, ln_b, l, tm, tn):
    s = xb.shape[0]
    nc = D_MODEL // tn
    gate0 = _C_GATE // tn
    row = lambda w: pl.BlockSpec((tm, w), lambda i, c: (i, 0))
    colw = lambda k: pl.BlockSpec((None, k, tn), lambda i, c: (l, 0, c))
    gate = lambda b: pl.BlockSpec((None, D_MODEL, tn), lambda i, c: (l, 0, gate0 + b * nc + c))
    vec = pl.BlockSpec((None, 1, D_MODEL), lambda i, c: (l, 0, 0))
    return pl.pallas_call(
        _merge_kernel,
        grid=(s // tm, nc),
        in_specs=[row(D_MODEL), row(D_MODEL), row(D_CONV), row(D_ATTN), row(D_FOURIER),
                  gate(0), gate(1), gate(2),
                  colw(D_CONV), colw(D_ATTN), colw(D_FOURIER),
                  pl.BlockSpec((None, tn, D_MODEL), lambda i, c: (l, c, 0)), vec, vec],
        out_specs=[row(D_MODEL), row(D_MODEL)],
        out_shape=[jax.ShapeDtypeStruct((s, D_MODEL), _F32), jax.ShapeDtypeStruct((s, D_MODEL), _BF16)],
        compiler_params=_params(("arbitrary", "arbitrary")),
        name="merge",
    )(xb, xf, a, o, fr, w_in, w_in, w_in, w_co, w_ao, w_fo, w_o, ln_g, ln_b)


def _ffn_kernel(xm_ref, xp_ref, xn_ref, xf_ref, wg_ref, wv_ref, cw_ref, wd_ref, g_ref, b_ref,
                outf_ref, outb_ref, xs_ref):
    i = pl.program_id(0)
    j = pl.program_id(1)
    tm = xm_ref.shape[0]

    @pl.when(j == 0)
    def _():
        _fill_halo_lhs(xs_ref, xm_ref, xp_ref, xn_ref, i == 0, i == pl.num_programs(0) - 1)
        outf_ref[...] = jnp.zeros_like(outf_ref)

    hg = _dot(xs_ref[...], wg_ref[...])
    hv = _dot(xs_ref[0:tm, :], wv_ref[...])
    h = jax.nn.silu(_dwconv3_rows(hg, cw_ref[...], tm)) * hv
    outf_ref[...] += _dot(h.astype(_BF16), wd_ref[...])

    @pl.when(j == pl.num_programs(1) - 1)
    def _():
        y = _layer_norm(DEEPNORM_ALPHA * xf_ref[...] + outf_ref[...], g_ref[...], b_ref[...])
        outf_ref[...] = y
        outb_ref[...] = y.astype(_BF16)


def _ffn(xb, xf, w_up, ffn_conv_w, w_down, ln_g, ln_b, l, tm, tn):
    s = xb.shape[0]
    nj = D_FF // tn
    row = pl.BlockSpec((tm, D_MODEL), lambda i, j: (i, 0))
    vec = pl.BlockSpec((None, 1, D_MODEL), lambda i, j: (l, 0, 0))
    return pl.pallas_call(
        _ffn_kernel,
        grid=(s // tm, nj),
        in_specs=_halo_specs(tm, s, D_MODEL, 2) + [
            row,
            pl.BlockSpec((None, D_MODEL, tn), lambda i, j: (l, 0, j)),
            pl.BlockSpec((None, D_MODEL, tn), lambda i, j: (l, 0, nj + j)),
            pl.BlockSpec((None, 3, tn), lambda i, j: (l, 0, j)),
            pl.BlockSpec((None, tn, D_MODEL), lambda i, j: (l, j, 0)),
            vec, vec],
        out_specs=[row, row],
        out_shape=[jax.ShapeDtypeStruct((s, D_MODEL), _F32), jax.ShapeDtypeStruct((s, D_MODEL), _BF16)],
        scratch_shapes=[pltpu.VMEM((tm + _HALO, D_MODEL), _BF16)],
        compiler_params=_params(("arbitrary", "arbitrary")),
        name="ffn",
    )(xb, xb, xb, xf, w_up, w_up, ffn_conv_w, w_down, ln_g, ln_b)


def _prep_weights(w_in, conv_w, q_gain, k_gain, w_conv_out, w_attn_out, w_fourier_out, w_o,
                  ln1_g, ln1_b, w_up, ffn_conv_w, w_down, ln2_g, ln2_b):
    depth = w_in.shape[0]
    vec = lambda v: v.reshape(depth, 1, v.shape[-1])
    qk0, qk1 = _C_QKVF, _C_QKVF + D_ATTN + D_KV
    return dict(
        depth=depth,
        w_in=w_in.astype(_BF16),
        w_qkvf=jnp.concatenate([_pair_major(w_in[:, :, qk0:qk1]), w_in[:, :, qk1:_C_GATE]],
                               axis=2).astype(_BF16),
        conv_w=conv_w,
        q_gain=vec(_pair_major(q_gain)), k_gain=vec(_pair_major(k_gain)),
        w_co=w_conv_out.astype(_BF16), w_ao=w_attn_out.astype(_BF16), w_fo=w_fourier_out.astype(_BF16),
        w_o=w_o.astype(_BF16),
        ln1_g=vec(ln1_g), ln1_b=vec(ln1_b),
        w_up=w_up.astype(_BF16), ffn_conv_w=ffn_conv_w, w_down=w_down.astype(_BF16),
        ln2_g=vec(ln2_g), ln2_b=vec(ln2_b),
    )


def _trunk(x, w):
    s = x.shape[0]
    tm = min(512, s)
    n1 = _DFT_N1
    n2 = s // n1
    cos, sin = _rope_tables(s)
    wa, mcs = _dft_tables(s)
    dft = _channel_dft_matrix()
    xf = x
    xb = x.astype(_BF16)
    for l in range(w["depth"]):
        a = _conv_in(xb, w["w_in"], w["conv_w"], l, tm)
        qt, k, vt, fr, fi = _qkvf_in(xb, w["w_qkvf"], w["q_gain"], w["k_gain"], cos, sin, dft, l, tm)
        yr, yi = _four_a(fr.reshape(n2, n1 * D_FOURIER), fi.reshape(n2, n1 * D_FOURIER), wa,
                         min(4096, n1 * D_FOURIER))
        fmix = _four_c(yr.reshape(n2, n1, D_FOURIER), yi.reshape(n2, n1, D_FOURIER), mcs,
                       min(8, n2)).reshape(s, D_FOURIER)
        score_bound = ((1.02 * HEAD_DIM * _Q_SCALE) * jnp.max(jnp.abs(w["q_gain"][l]))
                       * jnp.max(jnp.abs(w["k_gain"][l])))
        o = _attn(qt, k, vt, score_bound, min(256, s), min(1024, s // 2), min(2048, s // 2))
        xf, xb = _merge(xb, xf, a, o, fmix, w["w_in"], w["w_co"], w["w_ao"], w["w_fo"], w["w_o"],
                        w["ln1_g"], w["ln1_b"], l, tm, 512)
        xf, xb = _ffn(xb, xf, w["w_up"], w["ffn_conv_w"], w["w_down"], w["ln2_g"], w["ln2_b"], l, tm, 512)
    return xf


def kernel(x_prompt, x_sample, w_in, conv_w, q_gain, k_gain, w_conv_out, w_attn_out, w_fourier_out,
           w_o, ln1_g, ln1_b, w_up, ffn_conv_w, w_down, ln2_g, ln2_b):
    layers = _prep_weights(w_in, conv_w, q_gain, k_gain, w_conv_out, w_attn_out, w_fourier_out, w_o,
                           ln1_g, ln1_b, w_up, ffn_conv_w, w_down, ln2_g, ln2_b)
    outs = []
    for x in (x_prompt, x_sample):
        b, s, d = x.shape
        rows = x.reshape(b * s, d)
        ys = [_trunk(rows[bi * s:(bi + 1) * s], layers) for bi in range(b)]
        outs.append((ys[0] if b == 1 else jnp.concatenate(ys, axis=0)).reshape(b, s, d))
    return tuple(outs)
```

```python
import functools
import math

import jax
import jax.numpy as jnp
from jax import lax
from jax.experimental import pallas as pl
from jax.experimental.pallas import tpu as pltpu

D_MODEL = 2048
DEPTH = 4
GRID_W = 64
D_CONV = 512
N_HEADS = 8
N_KV_HEADS = 2
HEAD_DIM = 128
GROUP = N_HEADS // N_KV_HEADS
D_ATTN = N_HEADS * HEAD_DIM
D_KV = N_KV_HEADS * HEAD_DIM
ROPE_THETA = 10000.0
N_FOURIER_GROUPS = 4
FOURIER_GROUP = 128
D_FOURIER = N_FOURIER_GROUPS * FOURIER_GROUP
N_BRANCHES = 3
D_FF = 5632
LN_EPS = 1e-5
QK_EPS = 1e-6
DEEPNORM_ALPHA = (2 * DEPTH) ** 0.25

_C_CONV = 0
_C_QKVF = 3 * D_CONV
_C_GATE = _C_QKVF + D_ATTN + 2 * D_KV + D_FOURIER
_W_QKVF = _C_GATE - _C_QKVF

_HALO = 16
_DFT_N1 = 128
_VT_ROWS = HEAD_DIM + _HALO
_Q_SCALE = HEAD_DIM ** -0.5 * math.log2(math.e)
_SCORE_BOUND = 60.0
_VMEM_LIMIT = 58 * 1024 * 1024

_F32 = jnp.float32
_BF16 = jnp.bfloat16


def _dot(a, b):
    return jnp.dot(a, b, preferred_element_type=_F32)


def _layer_norm(y, g, b):
    mu = jnp.mean(y, axis=-1, keepdims=True)
    yc = y - mu
    var = jnp.mean(yc * yc, axis=-1, keepdims=True)
    return yc * lax.rsqrt(var + LN_EPS) * g + b


def _params(sem):
    return pltpu.CompilerParams(dimension_semantics=sem, vmem_limit_bytes=_VMEM_LIMIT)


def _fill_halo_lhs(xs_ref, xm_ref, xp_ref, xn_ref, first, last):
    tm = xm_ref.shape[0]
    half = _HALO // 2
    pv = jnp.where(first, 0.0, xp_ref[...].astype(_F32))
    nx = jnp.where(last, 0.0, xn_ref[...].astype(_F32))
    xs_ref[0:tm, :] = xm_ref[...]
    xs_ref[tm:tm + _HALO, :] = jnp.concatenate([nx[0:half], pv[half:_HALO]], axis=0).astype(_BF16)


def _dwconv3_rows(p_ext, cw, tm):
    n = p_ext.shape[0]
    prev = pltpu.roll(p_ext, 1, 0)[0:tm]
    nxt = pltpu.roll(p_ext, n - 1, 0)[0:tm]
    return prev * cw[0:1, :] + p_ext[0:tm] * cw[1:2, :] + nxt * cw[2:3, :]


def _halo_specs(tm, n_rows, width, ngrid):
    hb = tm // _HALO
    last_blk = n_rows // _HALO - 1
    if ngrid == 1:
        return [
            pl.BlockSpec((tm, width), lambda i: (i, 0)),
            pl.BlockSpec((_HALO, width), lambda i: (jnp.maximum(i * hb - 1, 0), 0)),
            pl.BlockSpec((_HALO, width), lambda i: (jnp.minimum((i + 1) * hb, last_blk), 0)),
        ]
    return [
        pl.BlockSpec((tm, width), lambda i, j: (i, 0)),
        pl.BlockSpec((_HALO, width), lambda i, j: (jnp.maximum(i * hb - 1, 0), 0)),
        pl.BlockSpec((_HALO, width), lambda i, j: (jnp.minimum((i + 1) * hb, last_blk), 0)),
    ]


def _conv_in_kernel(xm_ref, xp_ref, xn_ref, w_ref, cw_ref, a_ref, xs_ref):
    i = pl.program_id(0)
    tm = xm_ref.shape[0]
    _fill_halo_lhs(xs_ref, xm_ref, xp_ref, xn_ref, i == 0, i == pl.num_programs(0) - 1)
    u = _dot(xs_ref[...], w_ref[...])
    p = u[:, D_CONV:2 * D_CONV] * u[:, 2 * D_CONV:3 * D_CONV]
    conv = _dwconv3_rows(p, cw_ref[...], tm)
    a_ref[...] = (u[0:tm, 0:D_CONV] * conv).astype(_BF16)


def _conv_in(xb, w_in, conv_w, l, tm):
    s = xb.shape[0]
    return pl.pallas_call(
        _conv_in_kernel,
        grid=(s // tm,),
        in_specs=_halo_specs(tm, s, D_MODEL, 1) + [
            pl.BlockSpec((None, D_MODEL, 3 * D_CONV), lambda i: (l, 0, _C_CONV // (3 * D_CONV))),
            pl.BlockSpec((None, 3, D_CONV), lambda i: (l, 0, 0)),
        ],
        out_specs=pl.BlockSpec((tm, D_CONV), lambda i: (i, 0)),
        out_shape=jax.ShapeDtypeStruct((s, D_CONV), _BF16),
        scratch_shapes=[pltpu.VMEM((tm + _HALO, D_MODEL), _BF16)],
        compiler_params=_params(("arbitrary",)),
        name="conv_in",
    )(xb, xb, xb, w_in, conv_w)


def _qkvf_kernel(x_ref, wqk_ref, wv_ref, wf_ref, qg_ref, kg_ref, cos_ref, sin_ref, dft_ref,
                 qt_ref, k_ref, vt_ref, fr_ref, fi_ref):
    tm = x_ref.shape[0]
    x = x_ref[...]
    cos = cos_ref[...]
    sin = sin_ref[...]

    def norm_rope(xh, gain):
        ms = jnp.mean(xh * xh, axis=-1, keepdims=True)
        y = xh * lax.rsqrt(ms + QK_EPS) * gain
        return y * cos + pltpu.roll(y, HEAD_DIM // 2, 1) * sin

    u = _dot(x, wqk_ref[...])
    qg = qg_ref[...]
    kg = kg_ref[...]
    for h in range(N_HEADS):
        sl = slice(h * HEAD_DIM, (h + 1) * HEAD_DIM)
        qt_ref[sl, :] = (norm_rope(u[:, sl], qg) * _Q_SCALE).T.astype(_BF16)
    for h in range(N_KV_HEADS):
        sl = slice(h * HEAD_DIM, (h + 1) * HEAD_DIM)
        k_ref[:, sl] = norm_rope(u[:, D_ATTN + h * HEAD_DIM:D_ATTN + (h + 1) * HEAD_DIM], kg).astype(_BF16)
    v = _dot(x, wv_ref[...])
    for h in range(N_KV_HEADS):
        v0 = h * _VT_ROWS
        vt_ref[v0:v0 + HEAD_DIM, :] = v[:, h * HEAD_DIM:(h + 1) * HEAD_DIM].T.astype(_BF16)
        vt_ref[v0 + HEAD_DIM:v0 + _VT_ROWS, :] = jnp.ones((_HALO, tm), _BF16)
    f = _dot(x, wf_ref[...]).astype(_BF16)
    dft = dft_ref[...]
    for g in range(N_FOURIER_GROUPS):
        sl = slice(g * FOURIER_GROUP, (g + 1) * FOURIER_GROUP)
        z = _dot(f[:, sl], dft)
        fr_ref[:, sl] = z[:, 0:FOURIER_GROUP].astype(_BF16)
        fi_ref[:, sl] = z[:, FOURIER_GROUP:2 * FOURIER_GROUP].astype(_BF16)


def _qkvf_in(xb, w_qk, w_in, q_gain, k_gain, cos, sin, dft, l, tm):
    s = xb.shape[0]
    c_v = _C_QKVF + D_ATTN + D_KV
    c_f = c_v + D_KV
    row = lambda w: pl.BlockSpec((tm, w), lambda i: (i, 0))
    colt = lambda w: pl.BlockSpec((w, tm), lambda i: (0, i))
    layer = lambda a: pl.BlockSpec((None,) + a.shape[1:], lambda i: (l,) + (0,) * (a.ndim - 1))
    cols = lambda c0, w: pl.BlockSpec((None, D_MODEL, w), lambda i: (l, 0, c0 // w))
    full = lambda a: pl.BlockSpec(a.shape, lambda i: (0,) * a.ndim)
    assert c_v % D_KV == 0 and c_f % D_FOURIER == 0
    return pl.pallas_call(
        _qkvf_kernel,
        grid=(s // tm,),
        in_specs=[row(D_MODEL), layer(w_qk), cols(c_v, D_KV), cols(c_f, D_FOURIER),
                  layer(q_gain), layer(k_gain), row(HEAD_DIM), row(HEAD_DIM), full(dft)],
        out_specs=[colt(D_ATTN), row(D_KV), colt(N_KV_HEADS * _VT_ROWS), row(D_FOURIER), row(D_FOURIER)],
        out_shape=[jax.ShapeDtypeStruct((D_ATTN, s), _BF16),
                   jax.ShapeDtypeStruct((s, D_KV), _BF16),
                   jax.ShapeDtypeStruct((N_KV_HEADS * _VT_ROWS, s), _BF16),
                   jax.ShapeDtypeStruct((s, D_FOURIER), _BF16),
                   jax.ShapeDtypeStruct((s, D_FOURIER), _BF16)],
        compiler_params=_params(("arbitrary",)),
        name="qkvf_in",
    )(xb, w_qk, w_in, w_in, q_gain, k_gain, cos, sin, dft)


def _four_a_kernel(fr_ref, fi_ref, w_ref, yr_ref, yi_ref):
    n2 = fr_ref.shape[0]
    f = jnp.concatenate([fr_ref[...], fi_ref[...]], axis=0)
    y = _dot(w_ref[...], f)
    yr_ref[...] = y[0:n2].astype(_BF16)
    yi_ref[...] = y[n2:2 * n2].astype(_BF16)


def _four_a(fr2, fi2, wa, tn):
    n2, width = fr2.shape
    col = pl.BlockSpec((n2, tn), lambda j: (0, j))
    return pl.pallas_call(
        _four_a_kernel,
        grid=(width // tn,),
        in_specs=[col, col, pl.BlockSpec((2 * n2, 2 * n2), lambda j: (0, 0))],
        out_specs=[col, col],
        out_shape=[jax.ShapeDtypeStruct((n2, width), _BF16)] * 2,
        compiler_params=_params(("arbitrary",)),
        name="four_a",
    )(fr2, fi2, wa)


def _four_c_kernel(yr_ref, yi_ref, m_ref, o_ref):
    for b in range(yr_ref.shape[0]):
        y = jnp.concatenate([yr_ref[b], yi_ref[b]], axis=0)
        o_ref[:, b * D_FOURIER:(b + 1) * D_FOURIER] = _dot(m_ref[b], y).astype(_BF16)


def _four_c(yr3, yi3, mcs, sb):
    n2, n1, _ = yr3.shape
    dat = pl.BlockSpec((sb, n1, D_FOURIER), lambda j: (j, 0, 0))
    return pl.pallas_call(
        _four_c_kernel,
        grid=(n2 // sb,),
        in_specs=[dat, dat, pl.BlockSpec((sb, n1, 2 * n1), lambda j: (j, 0, 0))],
        out_specs=pl.BlockSpec((n1, sb * D_FOURIER), lambda j: (0, j)),
        out_shape=jax.ShapeDtypeStruct((n1, n2 * D_FOURIER), _BF16),
        compiler_params=_params(("arbitrary",)),
        name="four_c",
    )(yr3, yi3, mcs)


def _dft_tables(s):
    n1 = _DFT_N1
    n2 = s // n1
    two_pi = 2.0 * math.pi
    i2 = jnp.arange(n2, dtype=jnp.int32)
    ang2 = ((i2[:, None] * i2[None, :]) % n2).astype(_F32) * (two_pi / n2)
    c2 = jnp.cos(ang2) * (n2 ** -0.5)
    s2 = jnp.sin(ang2) * (n2 ** -0.5)
    wa = jnp.concatenate([jnp.concatenate([c2, s2], axis=1),
                          jnp.concatenate([-s2, c2], axis=1)], axis=0)
    i1 = jnp.arange(n1, dtype=jnp.int32)
    ang_a = ((i1[:, None] * i1[None, :]) % n1).astype(_F32) * (two_pi / n1)
    ang_b = (i2[:, None] * i1[None, :]).astype(_F32) * (two_pi / s)
    ca, sa = jnp.cos(ang_a)[None], jnp.sin(ang_a)[None]
    cb, sb = jnp.cos(ang_b)[:, None, :], jnp.sin(ang_b)[:, None, :]
    mcs = jnp.concatenate([ca * cb - sa * sb, sa * cb + ca * sb], axis=2) * (n1 ** -0.5)
    return wa.astype(_BF16), mcs.astype(_BF16)


def _channel_dft_matrix():
    i = jnp.arange(FOURIER_GROUP, dtype=jnp.int32)
    ang = ((i[:, None] * i[None, :]) % FOURIER_GROUP).astype(_F32) * (2.0 * math.pi / FOURIER_GROUP)
    scale = FOURIER_GROUP ** -0.5
    return jnp.concatenate([jnp.cos(ang) * scale, -jnp.sin(ang) * scale], axis=1).astype(_BF16)


def _rope_tables(s):
    rows = s // GRID_W
    half = HEAD_DIM // 2
    inv_freq = ROPE_THETA ** (-jnp.arange(0, half, 2, dtype=_F32) / half)
    ar = jnp.arange(rows, dtype=_F32)[:, None] * inv_freq
    ac = jnp.arange(GRID_W, dtype=_F32)[:, None] * inv_freq
    grid = (rows, GRID_W, HEAD_DIM // 4)
    by_row = lambda v: jnp.broadcast_to(v[:, None, :], grid)
    by_col = lambda v: jnp.broadcast_to(v[None, :, :], grid)
    cr, sr, cc, sc = by_row(jnp.cos(ar)), by_row(jnp.sin(ar)), by_col(jnp.cos(ac)), by_col(jnp.sin(ac))
    cos = jnp.concatenate([cr, cc, cr, cc], axis=2).reshape(s, HEAD_DIM)
    sin = jnp.concatenate([-sr, -sc, sr, sc], axis=2).reshape(s, HEAD_DIM)
    return cos, sin


def _pair_major(w):
    lead = w.shape[:-1]
    heads = w.shape[-1] // HEAD_DIM
    w = w.reshape(lead + (heads, 2, 2, HEAD_DIM // 4))
    return jnp.swapaxes(w, -3, -2).reshape(lead + (heads * HEAD_DIM,))


def _attn_kernel(qt_ref, k_ref, vt_ref, o_ref, s0_ref, s1_ref, p0_ref, p1_ref, acc_ref, *, tk):
    tq = qt_ref.shape[1]
    n_chunks = k_ref.shape[0] // tk

    def scores(j, s_ref):
        kc = k_ref[pl.ds(pl.multiple_of(j * tk, tk), tk), :]
        cmax = []
        for g in range(GROUP):
            st = _dot(kc, qt_ref[g * HEAD_DIM:(g + 1) * HEAD_DIM, :])
            s_ref[:, g * tq:(g + 1) * tq] = st
            cmax.append(jnp.max(st, axis=0, keepdims=True))
        return jnp.concatenate(cmax, axis=1)

    def softmax(s_ref, p_ref, m, cmax):
        m_new = jnp.maximum(m, cmax)
        alpha = jnp.exp2(m - m_new)
        p_ref[...] = jnp.exp2((s_ref[...] - m_new).astype(_BF16))
        return m_new, alpha

    def accumulate(j, p_ref, alpha):
        vc = vt_ref[:, pl.ds(pl.multiple_of(j * tk, tk), tk)]
        acc_ref[...] = alpha * acc_ref[...] + _dot(vc, p_ref[...])

    cmax0 = scores(0, s0_ref)
    p1_ref[...] = jnp.zeros_like(p1_ref)
    acc_ref[...] = jnp.zeros_like(acc_ref)

    def body(i, carry):
        m, alpha_prev, cmax_even = carry
        j = 2 * i
        cmax_odd = scores(j + 1, s1_ref)
        m, alpha_even = softmax(s0_ref, p0_ref, m, cmax_even)
        accumulate(jnp.maximum(j - 1, 0), p1_ref, alpha_prev)
        cmax_even = scores(jnp.minimum(j + 2, n_chunks - 1), s0_ref)
        m, alpha_odd = softmax(s1_ref, p1_ref, m, cmax_odd)
        accumulate(j, p0_ref, alpha_even)
        return m, alpha_odd, cmax_even

    m0 = jnp.full((1, GROUP * tq), -jnp.inf, _F32)
    a0 = jnp.ones((1, GROUP * tq), _F32)
    _, alpha_last, _ = lax.fori_loop(0, n_chunks // 2, body, (m0, a0, cmax0))
    accumulate(n_chunks - 1, p1_ref, alpha_last)
    out = acc_ref[0:HEAD_DIM, :] / acc_ref[HEAD_DIM:HEAD_DIM + 1, :]
    for g in range(GROUP):
        o_ref[:, g * HEAD_DIM:(g + 1) * HEAD_DIM] = out[:, g * tq:(g + 1) * tq].T.astype(_BF16)


def _attn_bounded_kernel(qt_ref, k_ref, vt_ref, o_ref, p0_ref, p1_ref, acc_ref, *, tk):
    tq = qt_ref.shape[1]
    mq = GROUP * tq
    n_chunks = k_ref.shape[0] // tk
    sub = 8

    def probs(j, p_ref, l8):
        kc = k_ref[pl.ds(pl.multiple_of(j * tk, tk), tk), :]
        parts = []
        for g in range(GROUP):
            p = jnp.exp2(_dot(kc, qt_ref[g * HEAD_DIM:(g + 1) * HEAD_DIM, :]))
            p_ref[:, g * tq:(g + 1) * tq] = p.astype(_BF16)
            parts.append(jnp.sum(p.reshape(tk // sub, sub, tq), axis=0))
        return l8 + jnp.concatenate(parts, axis=1)

    def accumulate(j, p_ref):
        vc = vt_ref[0:HEAD_DIM, pl.ds(pl.multiple_of(j * tk, tk), tk)]
        acc_ref[...] += _dot(vc, p_ref[...])

    acc_ref[...] = jnp.zeros_like(acc_ref)
    l8 = probs(0, p0_ref, jnp.zeros((sub, mq), _F32))

    def body(i, l8):
        j = 2 * i
        l8 = probs(j + 1, p1_ref, l8)
        accumulate(j, p0_ref)
        l8 = probs(j + 2, p0_ref, l8)
        accumulate(j + 1, p1_ref)
        return l8

    l8 = lax.fori_loop(0, n_chunks // 2 - 1, body, l8)
    l8 = probs(n_chunks - 1, p1_ref, l8)
    accumulate(n_chunks - 2, p0_ref)
    accumulate(n_chunks - 1, p1_ref)
    out = acc_ref[...] / jnp.sum(l8, axis=0, keepdims=True)
    for g in range(GROUP):
        o_ref[:, g * HEAD_DIM:(g + 1) * HEAD_DIM] = out[:, g * tq:(g + 1) * tq].T.astype(_BF16)


def _attn(qt, k, vt, score_bound, tq, tk, tk_bounded):
    s = k.shape[0]
    assert s % (2 * tk) == 0 and s % (2 * tk_bounded) == 0 and s % tq == 0
    mq = GROUP * tq
    bufs = lambda rows, dt: [pltpu.VMEM((rows, mq), dt), pltpu.VMEM((rows, mq), dt)]
    common = dict(
        grid=(N_KV_HEADS, s // tq),
        in_specs=[pl.BlockSpec((GROUP * HEAD_DIM, tq), lambda h, i: (h, i)),
                  pl.BlockSpec((s, HEAD_DIM), lambda h, i: (0, h)),
                  pl.BlockSpec((_VT_ROWS, s), lambda h, i: (h, 0))],
        out_specs=pl.BlockSpec((tq, GROUP * HEAD_DIM), lambda h, i: (i, h)),
        out_shape=jax.ShapeDtypeStruct((s, D_ATTN), _BF16),
        compiler_params=_params(("arbitrary", "arbitrary")),
    )
    bounded = pl.pallas_call(
        functools.partial(_attn_bounded_kernel, tk=tk_bounded),
        scratch_shapes=bufs(tk_bounded, _BF16) + [pltpu.VMEM((HEAD_DIM, mq), _F32)],
        name="attn_bounded", **common)
    online = pl.pallas_call(
        functools.partial(_attn_kernel, tk=tk),
        scratch_shapes=bufs(tk, _F32) + bufs(tk, _BF16) + [pltpu.VMEM((_VT_ROWS, mq), _F32)],
        name="attn", **common)
    return lax.cond(score_bound <= _SCORE_BOUND, bounded, online, qt, k, vt)


def _merge_kernel(xb_ref, xf_ref, a_ref, o_ref, fr_ref, wg0_ref, wg1_ref, wg2_ref,
                  wco_ref, wao_ref, wfo_ref, wo_ref, g_ref, b_ref, outf_ref, outb_ref):
    c = pl.program_id(1)
    xb = xb_ref[...]
    m = jax.nn.sigmoid(_dot(xb, wg0_ref[...])) * _dot(a_ref[...], wco_ref[...])
    m = m + jax.nn.sigmoid(_dot(xb, wg1_ref[...])) * _dot(o_ref[...], wao_ref[...])
    m = m + jax.nn.sigmoid(_dot(xb, wg2_ref[...])) * _dot(fr_ref[...], wfo_ref[...])

    @pl.when(c == 0)
    def _():
        outf_ref[...] = DEEPNORM_ALPHA * xf_ref[...]

    outf_ref[...] += _dot(m.astype(_BF16), wo_ref[...])

    @pl.when(c == pl.num_programs(1) - 1)
    def _():
        y = _layer_norm(outf_ref[...], g_ref[...], b_ref[...])
        outf_ref[...] = y
        outb_ref[...] = y.astype(_BF16)


def _merge(xb, xf, a, o, fr, w_in, w_co, w_ao, w_fo, w_o, ln_g, ln_b, l, tm, tn):
    s = xb.shape[0]
    nc = D_MODEL // tn
    gate0 = _C_GATE // tn
    row = lambda w: pl.BlockSpec((tm, w), lambda i, c: (i, 0))
    colw = lambda k: pl.BlockSpec((None, k, tn), lambda i, c: (l, 0, c))
    gate = lambda b: pl.BlockSpec((None, D_MODEL, tn), lambda i, c: (l, 0, gate0 + b * nc + c))
    vec = pl.BlockSpec((None, 1, D_MODEL), lambda i, c: (l, 0, 0))
    return pl.pallas_call(
        _merge_kernel,
        grid=(s // tm, nc),
        in_specs=[row(D_MODEL), row(D_MODEL), row(D_CONV), row(D_ATTN), row(D_FOURIER),
                  gate(0), gate(1), gate(2),
                  colw(D_CONV), colw(D_ATTN), colw(D_FOURIER),
                  pl.BlockSpec((None, tn, D_MODEL), lambda i, c: (l, c, 0)), vec, vec],
        out_specs=[row(D_MODEL), row(D_MODEL)],
        out_shape=[jax.ShapeDtypeStruct((s, D_MODEL), _F32), jax.ShapeDtypeStruct((s, D_MODEL), _BF16)],
        compiler_params=_params(("arbitrary", "arbitrary")),
        name="merge",
    )(xb, xf, a, o, fr, w_in, w_in, w_in, w_co, w_ao, w_fo, w_o, ln_g, ln_b)


def _ffn_kernel(xm_ref, xp_ref, xn_ref, xf_ref, wg_ref, wv_ref, cw_ref, wd_ref, g_ref, b_ref,
                outf_ref, outb_ref, xs_ref):
    i = pl.program_id(0)
    j = pl.program_id(1)
    tm = xm_ref.shape[0]

    @pl.when(j == 0)
    def _():
        _fill_halo_lhs(xs_ref, xm_ref, xp_ref, xn_ref, i == 0, i == pl.num_programs(0) - 1)
        outf_ref[...] = DEEPNORM_ALPHA * xf_ref[...]

    hg = _dot(xs_ref[...], wg_ref[...])
    hv = _dot(xs_ref[0:tm, :], wv_ref[...])
    h = jax.nn.silu(_dwconv3_rows(hg, cw_ref[...], tm)) * hv
    outf_ref[...] += _dot(h.astype(_BF16), wd_ref[...])

    @pl.when(j == pl.num_programs(1) - 1)
    def _():
        y = _layer_norm(outf_ref[...], g_ref[...], b_ref[...])
        outf_ref[...] = y
        outb_ref[...] = y.astype(_BF16)


def _ffn(xb, xf, w_up, ffn_conv_w, w_down, ln_g, ln_b, l, tm, tn):
    s = xb.shape[0]
    nj = D_FF // tn
    row = pl.BlockSpec((tm, D_MODEL), lambda i, j: (i, 0))
    vec = pl.BlockSpec((None, 1, D_MODEL), lambda i, j: (l, 0, 0))
    return pl.pallas_call(
        _ffn_kernel,
        grid=(s // tm, nj),
        in_specs=_halo_specs(tm, s, D_MODEL, 2) + [
            row,
            pl.BlockSpec((None, D_MODEL, tn), lambda i, j: (l, 0, j)),
            pl.BlockSpec((None, D_MODEL, tn), lambda i, j: (l, 0, nj + j)),
            pl.BlockSpec((None, 3, tn), lambda i, j: (l, 0, j)),
            pl.BlockSpec((None, tn, D_MODEL), lambda i, j: (l, j, 0)),
            vec, vec],
        out_specs=[row, row],
        out_shape=[jax.ShapeDtypeStruct((s, D_MODEL), _F32), jax.ShapeDtypeStruct((s, D_MODEL), _BF16)],
        scratch_shapes=[pltpu.VMEM((tm + _HALO, D_MODEL), _BF16)],
        compiler_params=_params(("arbitrary", "arbitrary")),
        name="ffn",
    )(xb, xb, xb, xf, w_up, w_up, ffn_conv_w, w_down, ln_g, ln_b)


def _prep_weights(w_in, conv_w, q_gain, k_gain, w_conv_out, w_attn_out, w_fourier_out, w_o,
                  ln1_g, ln1_b, w_up, ffn_conv_w, w_down, ln2_g, ln2_b):
    depth = w_in.shape[0]
    vec = lambda v: v.reshape(depth, 1, v.shape[-1])
    qk0, qk1 = _C_QKVF, _C_QKVF + D_ATTN + D_KV
    return dict(
        depth=depth,
        w_in=w_in.astype(_BF16),
        w_qk=_pair_major(w_in[:, :, qk0:qk1]).astype(_BF16),
        conv_w=conv_w,
        q_gain=vec(_pair_major(q_gain)), k_gain=vec(_pair_major(k_gain)),
        w_co=w_conv_out.astype(_BF16), w_ao=w_attn_out.astype(_BF16), w_fo=w_fourier_out.astype(_BF16),
        w_o=w_o.astype(_BF16),
        ln1_g=vec(ln1_g), ln1_b=vec(ln1_b),
        w_up=w_up.astype(_BF16), ffn_conv_w=ffn_conv_w, w_down=w_down.astype(_BF16),
        ln2_g=vec(ln2_g), ln2_b=vec(ln2_b),
    )


def _trunk(x, w):
    s = x.shape[0]
    tm = min(512, s)
    n1 = _DFT_N1
    n2 = s // n1
    cos, sin = _rope_tables(s)
    wa, mcs = _dft_tables(s)
    dft = _channel_dft_matrix()
    xf = x
    xb = x.astype(_BF16)
    for l in range(w["depth"]):
        a = _conv_in(xb, w["w_in"], w["conv_w"], l, tm)
        qt, k, vt, fr, fi = _qkvf_in(xb, w["w_qk"], w["w_in"], w["q_gain"], w["k_gain"], cos, sin, dft, l, tm)
        yr, yi = _four_a(fr.reshape(n2, n1 * D_FOURIER), fi.reshape(n2, n1 * D_FOURIER), wa,
                         min(4096, n1 * D_FOURIER))
        fmix = _four_c(yr.reshape(n2, n1, D_FOURIER), yi.reshape(n2, n1, D_FOURIER), mcs,
                       min(8, n2)).reshape(s, D_FOURIER)
        score_bound = ((1.02 * HEAD_DIM * _Q_SCALE) * jnp.max(jnp.abs(w["q_gain"][l]))
                       * jnp.max(jnp.abs(w["k_gain"][l])))
        o = _attn(qt, k, vt, score_bound, min(256, s), min(1024, s // 2), min(2048, s // 2))
        xf, xb = _merge(xb, xf, a, o, fmix, w["w_in"], w["w_co"], w["w_ao"], w["w_fo"], w["w_o"],
                        w["ln1_g"], w["ln1_b"], l, tm, 512)
        xf, xb = _ffn(xb, xf, w["w_up"], w["ffn_conv_w"], w["w_down"], w["ln2_g"], w["ln2_b"], l, tm, 512)
    return xf


def kernel(x_prompt, x_sample, w_in, conv_w, q_gain, k_gain, w_conv_out, w_attn_out, w_fourier_out,
           w_o, ln1_g, ln1_b, w_up, ffn_conv_w, w_down, ln2_g, ln2_b):
    layers = _prep_weights(w_in, conv_w, q_gain, k_gain, w_conv_out, w_attn_out, w_fourier_out, w_o,
                           ln1_g, ln1_b, w_up, ffn_conv_w, w_down, ln2_g, ln2_b)
    outs = []
    for x in (x_prompt, x_sample):
        b, s, d = x.shape
        rows = x.reshape(b * s, d)
        ys = [_trunk(rows[bi * s:(bi + 1) * s], layers) for bi in range(b)]
        outs.append((ys[0] if b == 1 else jnp.concatenate(ys, axis=0)).reshape(b, s, d))
    return tuple(outs)
```

```python
import functools
import math

import jax
import jax.numpy as jnp
from jax import lax
from jax.experimental import pallas as pl
from jax.experimental.pallas import tpu as pltpu

D_MODEL = 2048
DEPTH = 4
GRID_W = 64
D_CONV = 512
N_HEADS = 8
N_KV_HEADS = 2
HEAD_DIM = 128
GROUP = N_HEADS // N_KV_HEADS
D_ATTN = N_HEADS * HEAD_DIM
D_KV = N_KV_HEADS * HEAD_DIM
ROPE_THETA = 10000.0
N_FOURIER_GROUPS = 4
FOURIER_GROUP = 128
D_FOURIER = N_FOURIER_GROUPS * FOURIER_GROUP
N_BRANCHES = 3
D_FF = 5632
LN_EPS = 1e-5
QK_EPS = 1e-6
DEEPNORM_ALPHA = (2 * DEPTH) ** 0.25

_C_CONV = 0
_C_QKVF = 3 * D_CONV
_C_GATE = _C_QKVF + D_ATTN + 2 * D_KV + D_FOURIER
_W_QKVF = _C_GATE - _C_QKVF

_HALO = 16
_DFT_N1 = 128
_VT_ROWS = HEAD_DIM + _HALO
_Q_SCALE = HEAD_DIM ** -0.5 * math.log2(math.e)
_SCORE_BOUND = 60.0
_VMEM_LIMIT = 58 * 1024 * 1024

_F32 = jnp.float32
_BF16 = jnp.bfloat16


def _dot(a, b):
    return jnp.dot(a, b, preferred_element_type=_F32)


def _layer_norm(y, g, b):
    mu = jnp.mean(y, axis=-1, keepdims=True)
    yc = y - mu
    var = jnp.mean(yc * yc, axis=-1, keepdims=True)
    return yc * lax.rsqrt(var + LN_EPS) * g + b


def _params(sem):
    return pltpu.CompilerParams(dimension_semantics=sem, vmem_limit_bytes=_VMEM_LIMIT)


def _fill_halo_lhs(xs_ref, xm_ref, xp_ref, xn_ref, first, last):
    tm = xm_ref.shape[0]
    half = _HALO // 2
    pv = jnp.where(first, 0.0, xp_ref[...].astype(_F32))
    nx = jnp.where(last, 0.0, xn_ref[...].astype(_F32))
    xs_ref[0:tm, :] = xm_ref[...]
    xs_ref[tm:tm + _HALO, :] = jnp.concatenate([nx[0:half], pv[half:_HALO]], axis=0).astype(_BF16)


def _dwconv3_rows(p_ext, cw, tm):
    n = p_ext.shape[0]
    prev = pltpu.roll(p_ext, 1, 0)[0:tm]
    nxt = pltpu.roll(p_ext, n - 1, 0)[0:tm]
    return prev * cw[0:1, :] + p_ext[0:tm] * cw[1:2, :] + nxt * cw[2:3, :]


def _halo_specs(tm, n_rows, width, ngrid):
    hb = tm // _HALO
    last_blk = n_rows // _HALO - 1
    if ngrid == 1:
        return [
            pl.BlockSpec((tm, width), lambda i: (i, 0)),
            pl.BlockSpec((_HALO, width), lambda i: (jnp.maximum(i * hb - 1, 0), 0)),
            pl.BlockSpec((_HALO, width), lambda i: (jnp.minimum((i + 1) * hb, last_blk), 0)),
        ]
    return [
        pl.BlockSpec((tm, width), lambda i, j: (i, 0)),
        pl.BlockSpec((_HALO, width), lambda i, j: (jnp.maximum(i * hb - 1, 0), 0)),
        pl.BlockSpec((_HALO, width), lambda i, j: (jnp.minimum((i + 1) * hb, last_blk), 0)),
    ]


def _conv_in_kernel(xm_ref, xp_ref, xn_ref, w_ref, cw_ref, a_ref, xs_ref):
    i = pl.program_id(0)
    tm = xm_ref.shape[0]
    _fill_halo_lhs(xs_ref, xm_ref, xp_ref, xn_ref, i == 0, i == pl.num_programs(0) - 1)
    u = _dot(xs_ref[...], w_ref[...])
    p = u[:, D_CONV:2 * D_CONV] * u[:, 2 * D_CONV:3 * D_CONV]
    conv = _dwconv3_rows(p, cw_ref[...], tm)
    a_ref[...] = (u[0:tm, 0:D_CONV] * conv).astype(_BF16)


def _conv_in(xb, w_in, conv_w, l, tm):
    s = xb.shape[0]
    return pl.pallas_call(
        _conv_in_kernel,
        grid=(s // tm,),
        in_specs=_halo_specs(tm, s, D_MODEL, 1) + [
            pl.BlockSpec((None, D_MODEL, 3 * D_CONV), lambda i: (l, 0, _C_CONV // (3 * D_CONV))),
            pl.BlockSpec((None, 3, D_CONV), lambda i: (l, 0, 0)),
        ],
        out_specs=pl.BlockSpec((tm, D_CONV), lambda i: (i, 0)),
        out_shape=jax.ShapeDtypeStruct((s, D_CONV), _BF16),
        scratch_shapes=[pltpu.VMEM((tm + _HALO, D_MODEL), _BF16)],
        compiler_params=_params(("arbitrary",)),
        name="conv_in",
    )(xb, xb, xb, w_in, conv_w)


def _qkvf_kernel(x_ref, wqk_ref, wv_ref, wf_ref, qg_ref, kg_ref, cos_ref, sin_ref, dft_ref,
                 qt_ref, k_ref, vt_ref, fr_ref, fi_ref):
    tm = x_ref.shape[0]
    x = x_ref[...]
    cos = cos_ref[...]
    sin = sin_ref[...]

    def norm_rope(xh, gain):
        ms = jnp.mean(xh * xh, axis=-1, keepdims=True)
        y = xh * lax.rsqrt(ms + QK_EPS) * gain
        return y * cos + pltpu.roll(y, HEAD_DIM // 2, 1) * sin

    u = _dot(x, wqk_ref[...])
    qg = qg_ref[...]
    kg = kg_ref[...]
    for h in range(N_HEADS):
        sl = slice(h * HEAD_DIM, (h + 1) * HEAD_DIM)
        qt_ref[sl, :] = (norm_rope(u[:, sl], qg) * _Q_SCALE).T.astype(_BF16)
    for h in range(N_KV_HEADS):
        sl = slice(h * HEAD_DIM, (h + 1) * HEAD_DIM)
        k_ref[:, sl] = norm_rope(u[:, D_ATTN + h * HEAD_DIM:D_ATTN + (h + 1) * HEAD_DIM], kg).astype(_BF16)
    v = _dot(x, wv_ref[...])
    for h in range(N_KV_HEADS):
        v0 = h * _VT_ROWS
        vt_ref[v0:v0 + HEAD_DIM, :] = v[:, h * HEAD_DIM:(h + 1) * HEAD_DIM].T.astype(_BF16)
        vt_ref[v0 + HEAD_DIM:v0 + _VT_ROWS, :] = jnp.ones((_HALO, tm), _BF16)
    f = _dot(x, wf_ref[...]).astype(_BF16)
    dft = dft_ref[...]
    for g in range(N_FOURIER_GROUPS):
        sl = slice(g * FOURIER_GROUP, (g + 1) * FOURIER_GROUP)
        z = _dot(f[:, sl], dft)
        fr_ref[:, sl] = z[:, 0:FOURIER_GROUP].astype(_BF16)
        fi_ref[:, sl] = z[:, FOURIER_GROUP:2 * FOURIER_GROUP].astype(_BF16)


def _qkvf_in(xb, w_qk, w_in, q_gain, k_gain, cos, sin, dft, l, tm):
    s = xb.shape[0]
    c_v = _C_QKVF + D_ATTN + D_KV
    c_f = c_v + D_KV
    row = lambda w: pl.BlockSpec((tm, w), lambda i: (i, 0))
    colt = lambda w: pl.BlockSpec((w, tm), lambda i: (0, i))
    layer = lambda a: pl.BlockSpec((None,) + a.shape[1:], lambda i: (l,) + (0,) * (a.ndim - 1))
    cols = lambda c0, w: pl.BlockSpec((None, D_MODEL, w), lambda i: (l, 0, c0 // w))
    full = lambda a: pl.BlockSpec(a.shape, lambda i: (0,) * a.ndim)
    assert c_v % D_KV == 0 and c_f % D_FOURIER == 0
    return pl.pallas_call(
        _qkvf_kernel,
        grid=(s // tm,),
        in_specs=[row(D_MODEL), layer(w_qk), cols(c_v, D_KV), cols(c_f, D_FOURIER),
                  layer(q_gain), layer(k_gain), row(HEAD_DIM), row(HEAD_DIM), full(dft)],
        out_specs=[colt(D_ATTN), row(D_KV), colt(N_KV_HEADS * _VT_ROWS), row(D_FOURIER), row(D_FOURIER)],
        out_shape=[jax.ShapeDtypeStruct((D_ATTN, s), _BF16),
                   jax.ShapeDtypeStruct((s, D_KV), _BF16),
                   jax.ShapeDtypeStruct((N_KV_HEADS * _VT_ROWS, s), _BF16),
                   jax.ShapeDtypeStruct((s, D_FOURIER), _BF16),
                   jax.ShapeDtypeStruct((s, D_FOURIER), _BF16)],
        compiler_params=_params(("arbitrary",)),
        name="qkvf_in",
    )(xb, w_qk, w_in, w_in, q_gain, k_gain, cos, sin, dft)


def _four_a_kernel(fr_ref, fi_ref, w_ref, yr_ref, yi_ref):
    n2 = fr_ref.shape[0]
    f = jnp.concatenate([fr_ref[...], fi_ref[...]], axis=0)
    y = _dot(w_ref[...], f)
    yr_ref[...] = y[0:n2].astype(_BF16)
    yi_ref[...] = y[n2:2 * n2].astype(_BF16)


def _four_a(fr2, fi2, wa, tn):
    n2, width = fr2.shape
    col = pl.BlockSpec((n2, tn), lambda j: (0, j))
    return pl.pallas_call(
        _four_a_kernel,
        grid=(width // tn,),
        in_specs=[col, col, pl.BlockSpec((2 * n2, 2 * n2), lambda j: (0, 0))],
        out_specs=[col, col],
        out_shape=[jax.ShapeDtypeStruct((n2, width), _BF16)] * 2,
        compiler_params=_params(("arbitrary",)),
        name="four_a",
    )(fr2, fi2, wa)


def _four_c_kernel(yr_ref, yi_ref, m_ref, o_ref):
    for b in range(yr_ref.shape[0]):
        y = jnp.concatenate([yr_ref[b], yi_ref[b]], axis=0)
        o_ref[:, b * D_FOURIER:(b + 1) * D_FOURIER] = _dot(m_ref[b], y).astype(_BF16)


def _four_c(yr3, yi3, mcs, sb):
    n2, n1, _ = yr3.shape
    dat = pl.BlockSpec((sb, n1, D_FOURIER), lambda j: (j, 0, 0))
    return pl.pallas_call(
        _four_c_kernel,
        grid=(n2 // sb,),
        in_specs=[dat, dat, pl.BlockSpec((sb, n1, 2 * n1), lambda j: (j, 0, 0))],
        out_specs=pl.BlockSpec((n1, sb * D_FOURIER), lambda j: (0, j)),
        out_shape=jax.ShapeDtypeStruct((n1, n2 * D_FOURIER), _BF16),
        compiler_params=_params(("arbitrary",)),
        name="four_c",
    )(yr3, yi3, mcs)


def _dft_tables(s):
    n1 = _DFT_N1
    n2 = s // n1
    two_pi = 2.0 * math.pi
    i2 = jnp.arange(n2, dtype=jnp.int32)
    ang2 = ((i2[:, None] * i2[None, :]) % n2).astype(_F32) * (two_pi / n2)
    c2 = jnp.cos(ang2) * (n2 ** -0.5)
    s2 = jnp.sin(ang2) * (n2 ** -0.5)
    wa = jnp.concatenate([jnp.concatenate([c2, s2], axis=1),
                          jnp.concatenate([-s2, c2], axis=1)], axis=0)
    i1 = jnp.arange(n1, dtype=jnp.int32)
    ang_a = ((i1[:, None] * i1[None, :]) % n1).astype(_F32) * (two_pi / n1)
    ang_b = (i2[:, None] * i1[None, :]).astype(_F32) * (two_pi / s)
    ca, sa = jnp.cos(ang_a)[None], jnp.sin(ang_a)[None]
    cb, sb = jnp.cos(ang_b)[:, None, :], jnp.sin(ang_b)[:, None, :]
    mcs = jnp.concatenate([ca * cb - sa * sb, sa * cb + ca * sb], axis=2) * (n1 ** -0.5)
    return wa.astype(_BF16), mcs.astype(_BF16)


def _channel_dft_matrix():
    i = jnp.arange(FOURIER_GROUP, dtype=jnp.int32)
    ang = ((i[:, None] * i[None, :]) % FOURIER_GROUP).astype(_F32) * (2.0 * math.pi / FOURIER_GROUP)
    scale = FOURIER_GROUP ** -0.5
    return jnp.concatenate([jnp.cos(ang) * scale, -jnp.sin(ang) * scale], axis=1).astype(_BF16)


def _rope_tables(s):
    rows = s // GRID_W
    half = HEAD_DIM // 2
    inv_freq = ROPE_THETA ** (-jnp.arange(0, half, 2, dtype=_F32) / half)
    ar = jnp.arange(rows, dtype=_F32)[:, None] * inv_freq
    ac = jnp.arange(GRID_W, dtype=_F32)[:, None] * inv_freq
    grid = (rows, GRID_W, HEAD_DIM // 4)
    by_row = lambda v: jnp.broadcast_to(v[:, None, :], grid)
    by_col = lambda v: jnp.broadcast_to(v[None, :, :], grid)
    cr, sr, cc, sc = by_row(jnp.cos(ar)), by_row(jnp.sin(ar)), by_col(jnp.cos(ac)), by_col(jnp.sin(ac))
    cos = jnp.concatenate([cr, cc, cr, cc], axis=2).reshape(s, HEAD_DIM)
    sin = jnp.concatenate([-sr, -sc, sr, sc], axis=2).reshape(s, HEAD_DIM)
    return cos, sin


def _pair_major(w):
    lead = w.shape[:-1]
    heads = w.shape[-1] // HEAD_DIM
    w = w.reshape(lead + (heads, 2, 2, HEAD_DIM // 4))
    return jnp.swapaxes(w, -3, -2).reshape(lead + (heads * HEAD_DIM,))


def _attn_kernel(qt_ref, k_ref, vt_ref, o_ref, s0_ref, s1_ref, p0_ref, p1_ref, acc_ref, *, tk):
    tq = qt_ref.shape[1]
    n_chunks = k_ref.shape[0] // tk

    def scores(j, s_ref):
        kc = k_ref[pl.ds(pl.multiple_of(j * tk, tk), tk), :]
        cmax = []
        for g in range(GROUP):
            st = _dot(kc, qt_ref[g * HEAD_DIM:(g + 1) * HEAD_DIM, :])
            s_ref[:, g * tq:(g + 1) * tq] = st
            cmax.append(jnp.max(st, axis=0, keepdims=True))
        return jnp.concatenate(cmax, axis=1)

    def softmax(s_ref, p_ref, m, cmax):
        m_new = jnp.maximum(m, cmax)
        alpha = jnp.exp2(m - m_new)
        p_ref[...] = jnp.exp2((s_ref[...] - m_new).astype(_BF16))
        return m_new, alpha

    def accumulate(j, p_ref, alpha):
        vc = vt_ref[:, pl.ds(pl.multiple_of(j * tk, tk), tk)]
        acc_ref[...] = alpha * acc_ref[...] + _dot(vc, p_ref[...])

    cmax0 = scores(0, s0_ref)
    p1_ref[...] = jnp.zeros_like(p1_ref)
    acc_ref[...] = jnp.zeros_like(acc_ref)

    def body(i, carry):
        m, alpha_prev, cmax_even = carry
        j = 2 * i
        cmax_odd = scores(j + 1, s1_ref)
        m, alpha_even = softmax(s0_ref, p0_ref, m, cmax_even)
        accumulate(jnp.maximum(j - 1, 0), p1_ref, alpha_prev)
        cmax_even = scores(jnp.minimum(j + 2, n_chunks - 1), s0_ref)
        m, alpha_odd = softmax(s1_ref, p1_ref, m, cmax_odd)
        accumulate(j, p0_ref, alpha_even)
        return m, alpha_odd, cmax_even

    m0 = jnp.full((1, GROUP * tq), -jnp.inf, _F32)
    a0 = jnp.ones((1, GROUP * tq), _F32)
    _, alpha_last, _ = lax.fori_loop(0, n_chunks // 2, body, (m0, a0, cmax0))
    accumulate(n_chunks - 1, p1_ref, alpha_last)
    out = acc_ref[0:HEAD_DIM, :] / acc_ref[HEAD_DIM:HEAD_DIM + 1, :]
    for g in range(GROUP):
        o_ref[:, g * HEAD_DIM:(g + 1) * HEAD_DIM] = out[:, g * tq:(g + 1) * tq].T.astype(_BF16)


def _attn_bounded_kernel(qt_ref, k_ref, vt_ref, o_ref, p0_ref, p1_ref, acc_ref, *, tk):
    tq = qt_ref.shape[1]
    mq = GROUP * tq
    n_chunks = k_ref.shape[0] // tk
    sub = 8

    def probs(j, p_ref, l8):
        kc = k_ref[pl.ds(pl.multiple_of(j * tk, tk), tk), :]
        parts = []
        for g in range(GROUP):
            p = jnp.exp2(_dot(kc, qt_ref[g * HEAD_DIM:(g + 1) * HEAD_DIM, :]))
            p_ref[:, g * tq:(g + 1) * tq] = p.astype(_BF16)
            parts.append(jnp.sum(p.reshape(tk // sub, sub, tq), axis=0))
        return l8 + jnp.concatenate(parts, axis=1)

    def accumulate(j, p_ref):
        vc = vt_ref[0:HEAD_DIM, pl.ds(pl.multiple_of(j * tk, tk), tk)]
        acc_ref[...] += _dot(vc, p_ref[...])

    acc_ref[...] = jnp.zeros_like(acc_ref)
    l8 = probs(0, p0_ref, jnp.zeros((sub, mq), _F32))

    def body(i, l8):
        j = 2 * i
        l8 = probs(j + 1, p1_ref, l8)
        accumulate(j, p0_ref)
        l8 = probs(j + 2, p0_ref, l8)
        accumulate(j + 1, p1_ref)
        return l8

    l8 = lax.fori_loop(0, n_chunks // 2 - 1, body, l8)
    l8 = probs(n_chunks - 1, p1_ref, l8)
    accumulate(n_chunks - 2, p0_ref)
    accumulate(n_chunks - 1, p1_ref)
    out = acc_ref[...] / jnp.sum(l8, axis=0, keepdims=True)
    for g in range(GROUP):
        o_ref[:, g * HEAD_DIM:(g + 1) * HEAD_DIM] = out[:, g * tq:(g + 1) * tq].T.astype(_BF16)


def _attn(qt, k, vt, score_bound, tq, tk, tk_bounded):
    s = k.shape[0]
    assert s % (2 * tk) == 0 and s % (2 * tk_bounded) == 0 and s % tq == 0
    mq = GROUP * tq
    bufs = lambda rows, dt: [pltpu.VMEM((rows, mq), dt), pltpu.VMEM((rows, mq), dt)]
    common = dict(
        grid=(N_KV_HEADS, s // tq),
        in_specs=[pl.BlockSpec((GROUP * HEAD_DIM, tq), lambda h, i: (h, i)),
                  pl.BlockSpec((s, HEAD_DIM), lambda h, i: (0, h)),
                  pl.BlockSpec((_VT_ROWS, s), lambda h, i: (h, 0))],
        out_specs=pl.BlockSpec((tq, GROUP * HEAD_DIM), lambda h, i: (i, h)),
        out_shape=jax.ShapeDtypeStruct((s, D_ATTN), _BF16),
        compiler_params=_params(("arbitrary", "arbitrary")),
    )
    bounded = pl.pallas_call(
        functools.partial(_attn_bounded_kernel, tk=tk_bounded),
        scratch_shapes=bufs(tk_bounded, _BF16) + [pltpu.VMEM((HEAD_DIM, mq), _F32)],
        name="attn_bounded", **common)
    online = pl.pallas_call(
        functools.partial(_attn_kernel, tk=tk),
        scratch_shapes=bufs(tk, _F32) + bufs(tk, _BF16) + [pltpu.VMEM((_VT_ROWS, mq), _F32)],
        name="attn", **common)
    return lax.cond(score_bound <= _SCORE_BOUND, bounded, online, qt, k, vt)


def _merge_kernel(xb_ref, xf_ref, a_ref, o_ref, fr_ref, wg0_ref, wg1_ref, wg2_ref,
                  wco_ref, wao_ref, wfo_ref, wo_ref, g_ref, b_ref, outf_ref, outb_ref):
    c = pl.program_id(1)
    xb = xb_ref[...]
    m = jax.nn.sigmoid(_dot(xb, wg0_ref[...])) * _dot(a_ref[...], wco_ref[...])
    m = m + jax.nn.sigmoid(_dot(xb, wg1_ref[...])) * _dot(o_ref[...], wao_ref[...])
    m = m + jax.nn.sigmoid(_dot(xb, wg2_ref[...])) * _dot(fr_ref[...], wfo_ref[...])

    @pl.when(c == 0)
    def _():
        outf_ref[...] = DEEPNORM_ALPHA * xf_ref[...]

    outf_ref[...] += _dot(m.astype(_BF16), wo_ref[...])

    @pl.when(c == pl.num_programs(1) - 1)
    def _():
        y = _layer_norm(outf_ref[...], g_ref[...], b_ref[...])
        outf_ref[...] = y
        outb_ref[...] = y.astype(_BF16)


def _merge(xb, xf, a, o, fr, w_in, w_co, w_ao, w_fo, w_o, ln_g, ln_b, l, tm, tn):
    s = xb.shape[0]
    nc = D_MODEL // tn
    gate0 = _C_GATE // tn
    row = lambda w: pl.BlockSpec((tm, w), lambda i, c: (i, 0))
    colw = lambda k: pl.BlockSpec((None, k, tn), lambda i, c: (l, 0, c))
    gate = lambda b: pl.BlockSpec((None, D_MODEL, tn), lambda i, c: (l, 0, gate0 + b * nc + c))
    vec = pl.BlockSpec((None, 1, D_MODEL), lambda i, c: (l, 0, 0))
    return pl.pallas_call(
        _merge_kernel,
        grid=(s // tm, nc),
        in_specs=[row(D_MODEL), row(D_MODEL), row(D_CONV), row(D_ATTN), row(D_FOURIER),
                  gate(0), gate(1), gate(2),
                  colw(D_CONV), colw(D_ATTN), colw(D_FOURIER),
                  pl.BlockSpec((None, tn, D_MODEL), lambda i, c: (l, c, 0)), vec, vec],
        out_specs=[row(D_MODEL), row(D_MODEL)],
        out_shape=[jax.ShapeDtypeStruct((s, D_MODEL), _F32), jax.ShapeDtypeStruct((s, D_MODEL), _BF16)],
        compiler_params=_params(("arbitrary", "arbitrary")),
        name="merge",
    )(xb, xf, a, o, fr, w_in, w_in, w_in, w_co, w_ao, w_fo, w_o, ln_g, ln_b)


def _ffn_kernel(xm_ref, xp_ref, xn_ref, xf_ref, *rest, n_chunks, per_step):
    chunks = [rest[4 * c:4 * c + 4] for c in range(per_step)]
    g_ref, b_ref, outf_ref, outb_ref, xs_ref = rest[4 * per_step:]
    i = pl.program_id(0)
    j = pl.program_id(1)
    tm = xm_ref.shape[0]

    @pl.when(j == 0)
    def _():
        _fill_halo_lhs(xs_ref, xm_ref, xp_ref, xn_ref, i == 0, i == pl.num_programs(0) - 1)
        outf_ref[...] = DEEPNORM_ALPHA * xf_ref[...]

    def run_chunk(wg_ref, wv_ref, cw_ref, wd_ref):
        hg = _dot(xs_ref[...], wg_ref[...])
        hv = _dot(xs_ref[0:tm, :], wv_ref[...])
        h = jax.nn.silu(_dwconv3_rows(hg, cw_ref[...], tm)) * hv
        outf_ref[...] += _dot(h.astype(_BF16), wd_ref[...])

    last_step = -(-n_chunks // per_step) - 1
    for c, refs in enumerate(chunks):
        if last_step * per_step + c < n_chunks:
            run_chunk(*refs)
        else:
            pl.when(j * per_step + c < n_chunks)(functools.partial(run_chunk, *refs))

    @pl.when(j == pl.num_programs(1) - 1)
    def _():
        y = _layer_norm(outf_ref[...], g_ref[...], b_ref[...])
        outf_ref[...] = y
        outb_ref[...] = y.astype(_BF16)


def _ffn(xb, xf, w_up, ffn_conv_w, w_down, ln_g, ln_b, l, tm, tn, per_step):
    s = xb.shape[0]
    n_chunks = D_FF // tn
    nj = pl.cdiv(n_chunks, per_step)
    row = pl.BlockSpec((tm, D_MODEL), lambda i, j: (i, 0))
    vec = pl.BlockSpec((None, 1, D_MODEL), lambda i, j: (l, 0, 0))
    chunk_specs, chunk_args = [], []
    for c in range(per_step):
        idx = lambda j, c=c: jnp.minimum(j * per_step + c, n_chunks - 1)
        chunk_specs += [
            pl.BlockSpec((None, D_MODEL, tn), lambda i, j, idx=idx: (l, 0, idx(j))),
            pl.BlockSpec((None, D_MODEL, tn), lambda i, j, idx=idx: (l, 0, n_chunks + idx(j))),
            pl.BlockSpec((None, 3, tn), lambda i, j, idx=idx: (l, 0, idx(j))),
            pl.BlockSpec((None, tn, D_MODEL), lambda i, j, idx=idx: (l, idx(j), 0))]
        chunk_args += [w_up, w_up, ffn_conv_w, w_down]
    return pl.pallas_call(
        functools.partial(_ffn_kernel, n_chunks=n_chunks, per_step=per_step),
        grid=(s // tm, nj),
        in_specs=_halo_specs(tm, s, D_MODEL, 2) + [row] + chunk_specs + [vec, vec],
        out_specs=[row, row],
        out_shape=[jax.ShapeDtypeStruct((s, D_MODEL), _F32), jax.ShapeDtypeStruct((s, D_MODEL), _BF16)],
        scratch_shapes=[pltpu.VMEM((tm + _HALO, D_MODEL), _BF16)],
        compiler_params=_params(("arbitrary", "arbitrary")),
        name="ffn",
    )(xb, xb, xb, xf, *chunk_args, ln_g, ln_b)


def _prep_weights(w_in, conv_w, q_gain, k_gain, w_conv_out, w_attn_out, w_fourier_out, w_o,
                  ln1_g, ln1_b, w_up, ffn_conv_w, w_down, ln2_g, ln2_b):
    depth = w_in.shape[0]
    vec = lambda v: v.reshape(depth, 1, v.shape[-1])
    qk0, qk1 = _C_QKVF, _C_QKVF + D_ATTN + D_KV
    return dict(
        depth=depth,
        w_in=w_in.astype(_BF16),
        w_qk=_pair_major(w_in[:, :, qk0:qk1]).astype(_BF16),
        conv_w=conv_w,
        q_gain=vec(_pair_major(q_gain)), k_gain=vec(_pair_major(k_gain)),
        w_co=w_conv_out.astype(_BF16), w_ao=w_attn_out.astype(_BF16), w_fo=w_fourier_out.astype(_BF16),
        w_o=w_o.astype(_BF16),
        ln1_g=vec(ln1_g), ln1_b=vec(ln1_b),
        w_up=w_up.astype(_BF16), ffn_conv_w=ffn_conv_w, w_down=w_down.astype(_BF16),
        ln2_g=vec(ln2_g), ln2_b=vec(ln2_b),
    )


def _trunk(x, w):
    s = x.shape[0]
    tm = min(512, s)
    n1 = _DFT_N1
    n2 = s // n1
    cos, sin = _rope_tables(s)
    wa, mcs = _dft_tables(s)
    dft = _channel_dft_matrix()
    xf = x
    xb = x.astype(_BF16)
    for l in range(w["depth"]):
        a = _conv_in(xb, w["w_in"], w["conv_w"], l, tm)
        qt, k, vt, fr, fi = _qkvf_in(xb, w["w_qk"], w["w_in"], w["q_gain"], w["k_gain"], cos, sin, dft, l, tm)
        yr, yi = _four_a(fr.reshape(n2, n1 * D_FOURIER), fi.reshape(n2, n1 * D_FOURIER), wa,
                         min(4096, n1 * D_FOURIER))
        fmix = _four_c(yr.reshape(n2, n1, D_FOURIER), yi.reshape(n2, n1, D_FOURIER), mcs,
                       min(8, n2)).reshape(s, D_FOURIER)
        score_bound = ((1.02 * HEAD_DIM * _Q_SCALE) * jnp.max(jnp.abs(w["q_gain"][l]))
                       * jnp.max(jnp.abs(w["k_gain"][l])))
        o = _attn(qt, k, vt, score_bound, min(512, s), min(1024, s // 2), min(2048, s // 2))
        xf, xb = _merge(xb, xf, a, o, fmix, w["w_in"], w["w_co"], w["w_ao"], w["w_fo"], w["w_o"],
                        w["ln1_g"], w["ln1_b"], l, tm, 512)
        xf, xb = _ffn(xb, xf, w["w_up"], w["ffn_conv_w"], w["w_down"], w["ln2_g"], w["ln2_b"], l, tm, 512, 2)
    return xf


def kernel(x_prompt, x_sample, w_in, conv_w, q_gain, k_gain, w_conv_out, w_attn_out, w_fourier_out,
           w_o, ln1_g, ln1_b, w_up, ffn_conv_w, w_down, ln2_g, ln2_b):
    layers = _prep_weights(w_in, conv_w, q_gain, k_gain, w_conv_out, w_attn_out, w_fourier_out, w_o,
                           ln1_g, ln1_b, w_up, ffn_conv_w, w_down, ln2_g, ln2_b)
    outs = []
    for x in (x_prompt, x_sample):
        b, s, d = x.shape
        rows = x.reshape(b * s, d)
        ys = [_trunk(rows[bi * s:(bi + 1) * s], layers) for bi in range(b)]
        outs.append((ys[0] if b == 1 else jnp.concatenate(ys, axis=0)).reshape(b, s, d))
    return tuple(outs)
```

```python
import functools
import math

import jax
import jax.numpy as jnp
from jax import lax
from jax.experimental import pallas as pl
from jax.experimental.pallas import tpu as pltpu

D_MODEL = 2048
DEPTH = 4
GRID_W = 64
D_CONV = 512
N_HEADS = 8
N_KV_HEADS = 2
HEAD_DIM = 128
GROUP = N_HEADS // N_KV_HEADS
D_ATTN = N_HEADS * HEAD_DIM
D_KV = N_KV_HEADS * HEAD_DIM
ROPE_THETA = 10000.0
N_FOURIER_GROUPS = 4
FOURIER_GROUP = 128
D_FOURIER = N_FOURIER_GROUPS * FOURIER_GROUP
N_BRANCHES = 3
D_FF = 5632
LN_EPS = 1e-5
QK_EPS = 1e-6
DEEPNORM_ALPHA = (2 * DEPTH) ** 0.25

_C_CONV = 0
_C_QKVF = 3 * D_CONV
_C_GATE = _C_QKVF + D_ATTN + 2 * D_KV + D_FOURIER
_W_QKVF = _C_GATE - _C_QKVF

_HALO = 16
_DFT_N1 = 128
_VT_ROWS = HEAD_DIM + _HALO
_Q_SCALE = HEAD_DIM ** -0.5 * math.log2(math.e)
_SCORE_BOUND = 60.0
_VMEM_LIMIT = 58 * 1024 * 1024

_F32 = jnp.float32
_BF16 = jnp.bfloat16


def _dot(a, b):
    return jnp.dot(a, b, preferred_element_type=_F32)


def _layer_norm(y, g, b):
    mu = jnp.mean(y, axis=-1, keepdims=True)
    yc = y - mu
    var = jnp.mean(yc * yc, axis=-1, keepdims=True)
    return yc * lax.rsqrt(var + LN_EPS) * g + b


def _params(sem):
    return pltpu.CompilerParams(dimension_semantics=sem, vmem_limit_bytes=_VMEM_LIMIT)


def _fill_halo_lhs(xs_ref, xm_ref, xp_ref, xn_ref, first, last):
    tm = xm_ref.shape[0]
    half = _HALO // 2
    pv = jnp.where(first, 0.0, xp_ref[...].astype(_F32))
    nx = jnp.where(last, 0.0, xn_ref[...].astype(_F32))
    xs_ref[0:tm, :] = xm_ref[...]
    xs_ref[tm:tm + _HALO, :] = jnp.concatenate([nx[0:half], pv[half:_HALO]], axis=0).astype(_BF16)


def _dwconv3_rows(p_ext, cw, tm):
    n = p_ext.shape[0]
    prev = pltpu.roll(p_ext, 1, 0)[0:tm]
    nxt = pltpu.roll(p_ext, n - 1, 0)[0:tm]
    return prev * cw[0:1, :] + p_ext[0:tm] * cw[1:2, :] + nxt * cw[2:3, :]


def _halo_specs(tm, n_rows, width, ngrid):
    hb = tm // _HALO
    last_blk = n_rows // _HALO - 1
    if ngrid == 1:
        return [
            pl.BlockSpec((tm, width), lambda i: (i, 0)),
            pl.BlockSpec((_HALO, width), lambda i: (jnp.maximum(i * hb - 1, 0), 0)),
            pl.BlockSpec((_HALO, width), lambda i: (jnp.minimum((i + 1) * hb, last_blk), 0)),
        ]
    return [
        pl.BlockSpec((tm, width), lambda i, j: (i, 0)),
        pl.BlockSpec((_HALO, width), lambda i, j: (jnp.maximum(i * hb - 1, 0), 0)),
        pl.BlockSpec((_HALO, width), lambda i, j: (jnp.minimum((i + 1) * hb, last_blk), 0)),
    ]


def _conv_in_kernel(xm_ref, xp_ref, xn_ref, w_ref, cw_ref, a_ref, xs_ref):
    i = pl.program_id(0)
    tm = xm_ref.shape[0]
    _fill_halo_lhs(xs_ref, xm_ref, xp_ref, xn_ref, i == 0, i == pl.num_programs(0) - 1)
    u = _dot(xs_ref[...], w_ref[...])
    p = u[:, D_CONV:2 * D_CONV] * u[:, 2 * D_CONV:3 * D_CONV]
    conv = _dwconv3_rows(p, cw_ref[...], tm)
    a_ref[...] = (u[0:tm, 0:D_CONV] * conv).astype(_BF16)


def _conv_in(xb, w_in, conv_w, l, tm):
    s = xb.shape[0]
    return pl.pallas_call(
        _conv_in_kernel,
        grid=(s // tm,),
        in_specs=_halo_specs(tm, s, D_MODEL, 1) + [
            pl.BlockSpec((None, D_MODEL, 3 * D_CONV), lambda i: (l, 0, _C_CONV // (3 * D_CONV))),
            pl.BlockSpec((None, 3, D_CONV), lambda i: (l, 0, 0)),
        ],
        out_specs=pl.BlockSpec((tm, D_CONV), lambda i: (i, 0)),
        out_shape=jax.ShapeDtypeStruct((s, D_CONV), _BF16),
        scratch_shapes=[pltpu.VMEM((tm + _HALO, D_MODEL), _BF16)],
        compiler_params=_params(("arbitrary",)),
        name="conv_in",
    )(xb, xb, xb, w_in, conv_w)


def _qkvf_kernel(x_ref, wqk_ref, wv_ref, wf_ref, qg_ref, kg_ref, cos_ref, sin_ref, dft_ref,
                 qt_ref, k_ref, vt_ref, fr_ref, fi_ref):
    tm = x_ref.shape[0]
    x = x_ref[...]
    cos = cos_ref[...]
    sin = sin_ref[...]

    def norm_rope(xh, gain):
        ms = jnp.mean(xh * xh, axis=-1, keepdims=True)
        y = xh * lax.rsqrt(ms + QK_EPS) * gain
        return y * cos + pltpu.roll(y, HEAD_DIM // 2, 1) * sin

    u = _dot(x, wqk_ref[...])
    qg = qg_ref[...]
    kg = kg_ref[...]
    for h in range(N_HEADS):
        sl = slice(h * HEAD_DIM, (h + 1) * HEAD_DIM)
        qt_ref[sl, :] = (norm_rope(u[:, sl], qg) * _Q_SCALE).T.astype(_BF16)
    for h in range(N_KV_HEADS):
        sl = slice(h * HEAD_DIM, (h + 1) * HEAD_DIM)
        k_ref[:, sl] = norm_rope(u[:, D_ATTN + h * HEAD_DIM:D_ATTN + (h + 1) * HEAD_DIM], kg).astype(_BF16)
    v = _dot(x, wv_ref[...])
    for h in range(N_KV_HEADS):
        v0 = h * _VT_ROWS
        vt_ref[v0:v0 + HEAD_DIM, :] = v[:, h * HEAD_DIM:(h + 1) * HEAD_DIM].T.astype(_BF16)
        vt_ref[v0 + HEAD_DIM:v0 + _VT_ROWS, :] = jnp.ones((_HALO, tm), _BF16)
    f = _dot(x, wf_ref[...]).astype(_BF16)
    dft = dft_ref[...]
    for g in range(N_FOURIER_GROUPS):
        sl = slice(g * FOURIER_GROUP, (g + 1) * FOURIER_GROUP)
        z = _dot(f[:, sl], dft)
        fr_ref[:, sl] = z[:, 0:FOURIER_GROUP]
        fi_ref[:, sl] = z[:, FOURIER_GROUP:2 * FOURIER_GROUP]


def _qkvf_in(xb, w_qk, w_in, q_gain, k_gain, cos, sin, dft, l, tm):
    s = xb.shape[0]
    c_v = _C_QKVF + D_ATTN + D_KV
    c_f = c_v + D_KV
    row = lambda w: pl.BlockSpec((tm, w), lambda i: (i, 0))
    colt = lambda w: pl.BlockSpec((w, tm), lambda i: (0, i))
    layer = lambda a: pl.BlockSpec((None,) + a.shape[1:], lambda i: (l,) + (0,) * (a.ndim - 1))
    cols = lambda c0, w: pl.BlockSpec((None, D_MODEL, w), lambda i: (l, 0, c0 // w))
    full = lambda a: pl.BlockSpec(a.shape, lambda i: (0,) * a.ndim)
    assert c_v % D_KV == 0 and c_f % D_FOURIER == 0
    return pl.pallas_call(
        _qkvf_kernel,
        grid=(s // tm,),
        in_specs=[row(D_MODEL), layer(w_qk), cols(c_v, D_KV), cols(c_f, D_FOURIER),
                  layer(q_gain), layer(k_gain), row(HEAD_DIM), row(HEAD_DIM), full(dft)],
        out_specs=[colt(D_ATTN), row(D_KV), colt(N_KV_HEADS * _VT_ROWS), row(D_FOURIER), row(D_FOURIER)],
        out_shape=[jax.ShapeDtypeStruct((D_ATTN, s), _BF16),
                   jax.ShapeDtypeStruct((s, D_KV), _BF16),
                   jax.ShapeDtypeStruct((N_KV_HEADS * _VT_ROWS, s), _BF16),
                   jax.ShapeDtypeStruct((s, D_FOURIER), _F32),
                   jax.ShapeDtypeStruct((s, D_FOURIER), _F32)],
        compiler_params=_params(("arbitrary",)),
        name="qkvf_in",
    )(xb, w_qk, w_in, w_in, q_gain, k_gain, cos, sin, dft)


def _four_a_kernel(fr_ref, fi_ref, w_ref, yr_ref, yi_ref):
    n2, tb, _ = fr_ref.shape
    w = w_ref[...]
    for t in range(tb):
        f = jnp.concatenate([fr_ref[:, t, :], fi_ref[:, t, :]], axis=0).astype(_BF16)
        y = _dot(w, f)
        yr_ref[:, t, :] = y[0:n2]
        yi_ref[:, t, :] = y[n2:2 * n2]


def _four_a(fr3, fi3, wa, tb):
    n2, n1, _ = fr3.shape
    dat = pl.BlockSpec((n2, tb, D_FOURIER), lambda j: (0, j, 0))
    return pl.pallas_call(
        _four_a_kernel,
        grid=(n1 // tb,),
        in_specs=[dat, dat, pl.BlockSpec((2 * n2, 2 * n2), lambda j: (0, 0))],
        out_specs=[dat, dat],
        out_shape=[jax.ShapeDtypeStruct((n2, n1, D_FOURIER), _F32)] * 2,
        compiler_params=_params(("arbitrary",)),
        name="four_a",
    )(fr3, fi3, wa)


def _four_c_kernel(yr_ref, yi_ref, m_ref, o_ref):
    for b in range(yr_ref.shape[0]):
        y = jnp.concatenate([yr_ref[b], yi_ref[b]], axis=0).astype(_BF16)
        o_ref[:, b, :] = _dot(m_ref[b], y)


def _four_c(yr3, yi3, mcs, sb):
    n2, n1, _ = yr3.shape
    dat = pl.BlockSpec((sb, n1, D_FOURIER), lambda j: (j, 0, 0))
    return pl.pallas_call(
        _four_c_kernel,
        grid=(n2 // sb,),
        in_specs=[dat, dat, pl.BlockSpec((sb, n1, 2 * n1), lambda j: (j, 0, 0))],
        out_specs=pl.BlockSpec((n1, sb, D_FOURIER), lambda j: (0, j, 0)),
        out_shape=jax.ShapeDtypeStruct((n1, n2, D_FOURIER), _F32),
        compiler_params=_params(("arbitrary",)),
        name="four_c",
    )(yr3, yi3, mcs)


def _dft_tables(s):
    n1 = _DFT_N1
    n2 = s // n1
    two_pi = 2.0 * math.pi
    i2 = jnp.arange(n2, dtype=jnp.int32)
    ang2 = ((i2[:, None] * i2[None, :]) % n2).astype(_F32) * (two_pi / n2)
    c2 = jnp.cos(ang2) * (n2 ** -0.5)
    s2 = jnp.sin(ang2) * (n2 ** -0.5)
    wa = jnp.concatenate([jnp.concatenate([c2, s2], axis=1),
                          jnp.concatenate([-s2, c2], axis=1)], axis=0)
    i1 = jnp.arange(n1, dtype=jnp.int32)
    ang_a = ((i1[:, None] * i1[None, :]) % n1).astype(_F32) * (two_pi / n1)
    ang_b = (i2[:, None] * i1[None, :]).astype(_F32) * (two_pi / s)
    ca, sa = jnp.cos(ang_a)[None], jnp.sin(ang_a)[None]
    cb, sb = jnp.cos(ang_b)[:, None, :], jnp.sin(ang_b)[:, None, :]
    mcs = jnp.concatenate([ca * cb - sa * sb, sa * cb + ca * sb], axis=2) * (n1 ** -0.5)
    return wa.astype(_BF16), mcs.astype(_BF16)


def _channel_dft_matrix():
    i = jnp.arange(FOURIER_GROUP, dtype=jnp.int32)
    ang = ((i[:, None] * i[None, :]) % FOURIER_GROUP).astype(_F32) * (2.0 * math.pi / FOURIER_GROUP)
    scale = FOURIER_GROUP ** -0.5
    return jnp.concatenate([jnp.cos(ang) * scale, -jnp.sin(ang) * scale], axis=1).astype(_BF16)


def _rope_tables(s):
    rows = s // GRID_W
    half = HEAD_DIM // 2
    inv_freq = ROPE_THETA ** (-jnp.arange(0, half, 2, dtype=_F32) / half)
    ar = jnp.arange(rows, dtype=_F32)[:, None] * inv_freq
    ac = jnp.arange(GRID_W, dtype=_F32)[:, None] * inv_freq
    grid = (rows, GRID_W, HEAD_DIM // 4)
    by_row = lambda v: jnp.broadcast_to(v[:, None, :], grid)
    by_col = lambda v: jnp.broadcast_to(v[None, :, :], grid)
    cr, sr, cc, sc = by_row(jnp.cos(ar)), by_row(jnp.sin(ar)), by_col(jnp.cos(ac)), by_col(jnp.sin(ac))
    cos = jnp.concatenate([cr, cc, cr, cc], axis=2).reshape(s, HEAD_DIM)
    sin = jnp.concatenate([-sr, -sc, sr, sc], axis=2).reshape(s, HEAD_DIM)
    return cos, sin


def _pair_major(w):
    lead = w.shape[:-1]
    heads = w.shape[-1] // HEAD_DIM
    w = w.reshape(lead + (heads, 2, 2, HEAD_DIM // 4))
    return jnp.swapaxes(w, -3, -2).reshape(lead + (heads * HEAD_DIM,))


def _attn_kernel(qt_ref, k_ref, vt_ref, o_ref, s0_ref, s1_ref, p0_ref, p1_ref, acc_ref, *, tk):
    tq = qt_ref.shape[1]
    n_chunks = k_ref.shape[0] // tk

    def scores(j, s_ref):
        kc = k_ref[pl.ds(pl.multiple_of(j * tk, tk), tk), :]
        cmax = []
        for g in range(GROUP):
            st = _dot(kc, qt_ref[g * HEAD_DIM:(g + 1) * HEAD_DIM, :])
            s_ref[:, g * tq:(g + 1) * tq] = st
            cmax.append(jnp.max(st, axis=0, keepdims=True))
        return jnp.concatenate(cmax, axis=1)

    def softmax(s_ref, p_ref, m, cmax):
        m_new = jnp.maximum(m, cmax)
        alpha = jnp.exp2(m - m_new)
        p_ref[...] = jnp.exp2((s_ref[...] - m_new).astype(_BF16))
        return m_new, alpha

    def accumulate(j, p_ref, alpha):
        vc = vt_ref[:, pl.ds(pl.multiple_of(j * tk, tk), tk)]
        acc_ref[...] = alpha * acc_ref[...] + _dot(vc, p_ref[...])

    cmax0 = scores(0, s0_ref)
    p1_ref[...] = jnp.zeros_like(p1_ref)
    acc_ref[...] = jnp.zeros_like(acc_ref)

    def body(i, carry):
        m, alpha_prev, cmax_even = carry
        j = 2 * i
        cmax_odd = scores(j + 1, s1_ref)
        m, alpha_even = softmax(s0_ref, p0_ref, m, cmax_even)
        accumulate(jnp.maximum(j - 1, 0), p1_ref, alpha_prev)
        cmax_even = scores(jnp.minimum(j + 2, n_chunks - 1), s0_ref)
        m, alpha_odd = softmax(s1_ref, p1_ref, m, cmax_odd)
        accumulate(j, p0_ref, alpha_even)
        return m, alpha_odd, cmax_even

    m0 = jnp.full((1, GROUP * tq), -jnp.inf, _F32)
    a0 = jnp.ones((1, GROUP * tq), _F32)
    _, alpha_last, _ = lax.fori_loop(0, n_chunks // 2, body, (m0, a0, cmax0))
    accumulate(n_chunks - 1, p1_ref, alpha_last)
    out = acc_ref[0:HEAD_DIM, :] / acc_ref[HEAD_DIM:HEAD_DIM + 1, :]
    for g in range(GROUP):
        o_ref[:, g * HEAD_DIM:(g + 1) * HEAD_DIM] = out[:, g * tq:(g + 1) * tq].T.astype(_BF16)


def _attn_bounded_kernel(qt_ref, k_ref, vt_ref, o_ref, p0_ref, p1_ref, acc_ref, *, tk):
    tq = qt_ref.shape[1]
    mq = GROUP * tq
    n_chunks = k_ref.shape[0] // tk
    sub = 8

    def probs(j, p_ref, l8):
        kc = k_ref[pl.ds(pl.multiple_of(j * tk, tk), tk), :]
        parts = []
        for g in range(GROUP):
            p = jnp.exp2(_dot(kc, qt_ref[g * HEAD_DIM:(g + 1) * HEAD_DIM, :]))
            p_ref[:, g * tq:(g + 1) * tq] = p.astype(_BF16)
            parts.append(jnp.sum(p.reshape(tk // sub, sub, tq), axis=0))
        return l8 + jnp.concatenate(parts, axis=1)

    def accumulate(j, p_ref):
        vc = vt_ref[0:HEAD_DIM, pl.ds(pl.multiple_of(j * tk, tk), tk)]
        acc_ref[...] += _dot(vc, p_ref[...])

    acc_ref[...] = jnp.zeros_like(acc_ref)
    l8 = probs(0, p0_ref, jnp.zeros((sub, mq), _F32))

    def body(i, l8):
        j = 2 * i
        l8 = probs(j + 1, p1_ref, l8)
        accumulate(j, p0_ref)
        l8 = probs(j + 2, p0_ref, l8)
        accumulate(j + 1, p1_ref)
        return l8

    l8 = lax.fori_loop(0, n_chunks // 2 - 1, body, l8)
    l8 = probs(n_chunks - 1, p1_ref, l8)
    accumulate(n_chunks - 2, p0_ref)
    accumulate(n_chunks - 1, p1_ref)
    out = acc_ref[...] / jnp.sum(l8, axis=0, keepdims=True)
    for g in range(GROUP):
        o_ref[:, g * HEAD_DIM:(g + 1) * HEAD_DIM] = out[:, g * tq:(g + 1) * tq].T.astype(_BF16)


def _attn(qt, k, vt, score_bound, tq, tk, tk_bounded):
    s = k.shape[0]
    assert s % (2 * tk) == 0 and s % (2 * tk_bounded) == 0 and s % tq == 0
    mq = GROUP * tq
    bufs = lambda rows, dt: [pltpu.VMEM((rows, mq), dt), pltpu.VMEM((rows, mq), dt)]
    common = dict(
        grid=(N_KV_HEADS, s // tq),
        in_specs=[pl.BlockSpec((GROUP * HEAD_DIM, tq), lambda h, i: (h, i)),
                  pl.BlockSpec((s, HEAD_DIM), lambda h, i: (0, h)),
                  pl.BlockSpec((_VT_ROWS, s), lambda h, i: (h, 0))],
        out_specs=pl.BlockSpec((tq, GROUP * HEAD_DIM), lambda h, i: (i, h)),
        out_shape=jax.ShapeDtypeStruct((s, D_ATTN), _BF16),
        compiler_params=_params(("arbitrary", "arbitrary")),
    )
    bounded = pl.pallas_call(
        functools.partial(_attn_bounded_kernel, tk=tk_bounded),
        scratch_shapes=bufs(tk_bounded, _BF16) + [pltpu.VMEM((HEAD_DIM, mq), _F32)],
        name="attn_bounded", **common)
    online = pl.pallas_call(
        functools.partial(_attn_kernel, tk=tk),
        scratch_shapes=bufs(tk, _F32) + bufs(tk, _BF16) + [pltpu.VMEM((_VT_ROWS, mq), _F32)],
        name="attn", **common)
    return lax.cond(score_bound <= _SCORE_BOUND, bounded, online, qt, k, vt)


def _merge_kernel(xb_ref, xf_ref, a_ref, o_ref, fr_ref, wg0_ref, wg1_ref, wg2_ref,
                  wco_ref, wao_ref, wfo_ref, wo_ref, g_ref, b_ref, outf_ref, outb_ref):
    c = pl.program_id(1)
    xb = xb_ref[...]
    m = jax.nn.sigmoid(_dot(xb, wg0_ref[...])) * _dot(a_ref[...], wco_ref[...])
    m = m + jax.nn.sigmoid(_dot(xb, wg1_ref[...])) * _dot(o_ref[...], wao_ref[...])
    m = m + jax.nn.sigmoid(_dot(xb, wg2_ref[...])) * _dot(fr_ref[...].astype(_BF16), wfo_ref[...])

    @pl.when(c == 0)
    def _():
        outf_ref[...] = DEEPNORM_ALPHA * xf_ref[...]

    outf_ref[...] += _dot(m.astype(_BF16), wo_ref[...])

    @pl.when(c == pl.num_programs(1) - 1)
    def _():
        y = _layer_norm(outf_ref[...], g_ref[...], b_ref[...])
        outf_ref[...] = y
        outb_ref[...] = y.astype(_BF16)


def _merge(xb, xf, a, o, fr, w_in, w_co, w_ao, w_fo, w_o, ln_g, ln_b, l, tm, tn):
    s = xb.shape[0]
    nc = D_MODEL // tn
    gate0 = _C_GATE // tn
    row = lambda w: pl.BlockSpec((tm, w), lambda i, c: (i, 0))
    colw = lambda k: pl.BlockSpec((None, k, tn), lambda i, c: (l, 0, c))
    gate = lambda b: pl.BlockSpec((None, D_MODEL, tn), lambda i, c: (l, 0, gate0 + b * nc + c))
    vec = pl.BlockSpec((None, 1, D_MODEL), lambda i, c: (l, 0, 0))
    return pl.pallas_call(
        _merge_kernel,
        grid=(s // tm, nc),
        in_specs=[row(D_MODEL), row(D_MODEL), row(D_CONV), row(D_ATTN), row(D_FOURIER),
                  gate(0), gate(1), gate(2),
                  colw(D_CONV), colw(D_ATTN), colw(D_FOURIER),
                  pl.BlockSpec((None, tn, D_MODEL), lambda i, c: (l, c, 0)), vec, vec],
        out_specs=[row(D_MODEL), row(D_MODEL)],
        out_shape=[jax.ShapeDtypeStruct((s, D_MODEL), _F32), jax.ShapeDtypeStruct((s, D_MODEL), _BF16)],
        compiler_params=_params(("arbitrary", "arbitrary")),
        name="merge",
    )(xb, xf, a, o, fr, w_in, w_in, w_in, w_co, w_ao, w_fo, w_o, ln_g, ln_b)


def _ffn_kernel(xm_ref, xp_ref, xn_ref, xf_ref, wg_ref, wv_ref, cw_ref, wd_ref, g_ref, b_ref,
                outf_ref, outb_ref, xs_ref):
    i = pl.program_id(0)
    j = pl.program_id(1)
    tm = xm_ref.shape[0]

    @pl.when(j == 0)
    def _():
        _fill_halo_lhs(xs_ref, xm_ref, xp_ref, xn_ref, i == 0, i == pl.num_programs(0) - 1)
        outf_ref[...] = DEEPNORM_ALPHA * xf_ref[...]

    hg = _dot(xs_ref[...], wg_ref[...])
    hv = _dot(xs_ref[0:tm, :], wv_ref[...])
    h = jax.nn.silu(_dwconv3_rows(hg, cw_ref[...], tm)) * hv
    outf_ref[...] += _dot(h.astype(_BF16), wd_ref[...])

    @pl.when(j == pl.num_programs(1) - 1)
    def _():
        y = _layer_norm(outf_ref[...], g_ref[...], b_ref[...])
        outf_ref[...] = y
        outb_ref[...] = y.astype(_BF16)


def _ffn(xb, xf, w_up, ffn_conv_w, w_down, ln_g, ln_b, l, tm, tn):
    s = xb.shape[0]
    nj = D_FF // tn
    row = pl.BlockSpec((tm, D_MODEL), lambda i, j: (i, 0))
    vec = pl.BlockSpec((None, 1, D_MODEL), lambda i, j: (l, 0, 0))
    return pl.pallas_call(
        _ffn_kernel,
        grid=(s // tm, nj),
        in_specs=_halo_specs(tm, s, D_MODEL, 2) + [
            row,
            pl.BlockSpec((None, D_MODEL, tn), lambda i, j: (l, 0, j)),
            pl.BlockSpec((None, D_MODEL, tn), lambda i, j: (l, 0, nj + j)),
            pl.BlockSpec((None, 3, tn), lambda i, j: (l, 0, j)),
            pl.BlockSpec((None, tn, D_MODEL), lambda i, j: (l, j, 0)),
            vec, vec],
        out_specs=[row, row],
        out_shape=[jax.ShapeDtypeStruct((s, D_MODEL), _F32), jax.ShapeDtypeStruct((s, D_MODEL), _BF16)],
        scratch_shapes=[pltpu.VMEM((tm + _HALO, D_MODEL), _BF16)],
        compiler_params=_params(("arbitrary", "arbitrary")),
        name="ffn",
    )(xb, xb, xb, xf, w_up, w_up, ffn_conv_w, w_down, ln_g, ln_b)


def _prep_weights(w_in, conv_w, q_gain, k_gain, w_conv_out, w_attn_out, w_fourier_out, w_o,
                  ln1_g, ln1_b, w_up, ffn_conv_w, w_down, ln2_g, ln2_b):
    depth = w_in.shape[0]
    vec = lambda v: v.reshape(depth, 1, v.shape[-1])
    qk0, qk1 = _C_QKVF, _C_QKVF + D_ATTN + D_KV
    return dict(
        depth=depth,
        w_in=w_in.astype(_BF16),
        w_qk=_pair_major(w_in[:, :, qk0:qk1]).astype(_BF16),
        conv_w=conv_w,
        q_gain=vec(_pair_major(q_gain)), k_gain=vec(_pair_major(k_gain)),
        w_co=w_conv_out.astype(_BF16), w_ao=w_attn_out.astype(_BF16), w_fo=w_fourier_out.astype(_BF16),
        w_o=w_o.astype(_BF16),
        ln1_g=vec(ln1_g), ln1_b=vec(ln1_b),
        w_up=w_up.astype(_BF16), ffn_conv_w=ffn_conv_w, w_down=w_down.astype(_BF16),
        ln2_g=vec(ln2_g), ln2_b=vec(ln2_b),
    )


def _trunk(x, w):
    s = x.shape[0]
    tm = min(512, s)
    n1 = _DFT_N1
    n2 = s // n1
    cos, sin = _rope_tables(s)
    wa, mcs = _dft_tables(s)
    dft = _channel_dft_matrix()
    xf = x
    xb = x.astype(_BF16)
    for l in range(w["depth"]):
        a = _conv_in(xb, w["w_in"], w["conv_w"], l, tm)
        qt, k, vt, fr, fi = _qkvf_in(xb, w["w_qk"], w["w_in"], w["q_gain"], w["k_gain"], cos, sin, dft, l, tm)
        yr, yi = _four_a(fr.reshape(n2, n1, D_FOURIER), fi.reshape(n2, n1, D_FOURIER), wa, 8)
        fmix = _four_c(yr, yi, mcs, min(8, n2)).reshape(s, D_FOURIER)
        score_bound = ((1.02 * HEAD_DIM * _Q_SCALE) * jnp.max(jnp.abs(w["q_gain"][l]))
                       * jnp.max(jnp.abs(w["k_gain"][l])))
        o = _attn(qt, k, vt, score_bound, min(512, s), min(1024, s // 2), min(2048, s // 2))
        xf, xb = _merge(xb, xf, a, o, fmix, w["w_in"], w["w_co"], w["w_ao"], w["w_fo"], w["w_o"],
                        w["ln1_g"], w["ln1_b"], l, tm, 512)
        xf, xb = _ffn(xb, xf, w["w_up"], w["ffn_conv_w"], w["w_down"], w["ln2_g"], w["ln2_b"], l, tm, 512)
    return xf


def kernel(x_prompt, x_sample, w_in, conv_w, q_gain, k_gain, w_conv_out, w_attn_out, w_fourier_out,
           w_o, ln1_g, ln1_b, w_up, ffn_conv_w, w_down, ln2_g, ln2_b):
    layers = _prep_weights(w_in, conv_w, q_gain, k_gain, w_conv_out, w_attn_out, w_fourier_out, w_o,
                           ln1_g, ln1_b, w_up, ffn_conv_w, w_down, ln2_g, ln2_b)
    outs = []
    for x in (x_prompt, x_sample):
        b, s, d = x.shape
        rows = x.reshape(b * s, d)
        ys = [_trunk(rows[bi * s:(bi + 1) * s], layers) for bi in range(b)]
        outs.append((ys[0] if b == 1 else jnp.concatenate(ys, axis=0)).reshape(b, s, d))
    return tuple(outs)
```

```python
import functools
import math

import jax
import jax.numpy as jnp
from jax import lax
from jax.experimental import pallas as pl
from jax.experimental.pallas import tpu as pltpu

D_MODEL = 2048
DEPTH = 4
GRID_W = 64
D_CONV = 512
N_HEADS = 8
N_KV_HEADS = 2
HEAD_DIM = 128
GROUP = N_HEADS // N_KV_HEADS
D_ATTN = N_HEADS * HEAD_DIM
D_KV = N_KV_HEADS * HEAD_DIM
ROPE_THETA = 10000.0
N_FOURIER_GROUPS = 4
FOURIER_GROUP = 128
D_FOURIER = N_FOURIER_GROUPS * FOURIER_GROUP
N_BRANCHES = 3
D_FF = 5632
LN_EPS = 1e-5
QK_EPS = 1e-6
DEEPNORM_ALPHA = (2 * DEPTH) ** 0.25

_C_CONV = 0
_C_QKVF = 3 * D_CONV
_C_GATE = _C_QKVF + D_ATTN + 2 * D_KV + D_FOURIER
_W_QKVF = _C_GATE - _C_QKVF

_HALO = 16
_DFT_N1 = 128
_VT_ROWS = HEAD_DIM + _HALO
_Q_SCALE = HEAD_DIM ** -0.5 * math.log2(math.e)
_SCORE_BOUND = 60.0
_VMEM_LIMIT = 58 * 1024 * 1024

_F32 = jnp.float32
_BF16 = jnp.bfloat16


def _dot(a, b):
    return jnp.dot(a, b, preferred_element_type=_F32)


def _layer_norm(y, g, b):
    mu = jnp.mean(y, axis=-1, keepdims=True)
    yc = y - mu
    var = jnp.mean(yc * yc, axis=-1, keepdims=True)
    return yc * lax.rsqrt(var + LN_EPS) * g + b


def _params(sem):
    return pltpu.CompilerParams(dimension_semantics=sem, vmem_limit_bytes=_VMEM_LIMIT)


def _fill_halo_lhs(xs_ref, xm_ref, xp_ref, xn_ref, first, last):
    tm = xm_ref.shape[0]
    half = _HALO // 2
    pv = jnp.where(first, 0.0, xp_ref[...].astype(_F32))
    nx = jnp.where(last, 0.0, xn_ref[...].astype(_F32))
    xs_ref[0:tm, :] = xm_ref[...]
    xs_ref[tm:tm + _HALO, :] = jnp.concatenate([nx[0:half], pv[half:_HALO]], axis=0).astype(_BF16)


def _dwconv3_rows(p_ext, cw, tm):
    n = p_ext.shape[0]
    prev = pltpu.roll(p_ext, 1, 0)[0:tm]
    nxt = pltpu.roll(p_ext, n - 1, 0)[0:tm]
    return prev * cw[0:1, :] + p_ext[0:tm] * cw[1:2, :] + nxt * cw[2:3, :]


def _halo_specs(tm, n_rows, width, ngrid):
    hb = tm // _HALO
    last_blk = n_rows // _HALO - 1
    if ngrid == 1:
        return [
            pl.BlockSpec((tm, width), lambda i: (i, 0)),
            pl.BlockSpec((_HALO, width), lambda i: (jnp.maximum(i * hb - 1, 0), 0)),
            pl.BlockSpec((_HALO, width), lambda i: (jnp.minimum((i + 1) * hb, last_blk), 0)),
        ]
    return [
        pl.BlockSpec((tm, width), lambda i, j: (i, 0)),
        pl.BlockSpec((_HALO, width), lambda i, j: (jnp.maximum(i * hb - 1, 0), 0)),
        pl.BlockSpec((_HALO, width), lambda i, j: (jnp.minimum((i + 1) * hb, last_blk), 0)),
    ]


def _conv_in_kernel(xm_ref, xp_ref, xn_ref, w_ref, cw_ref, a_ref, xs_ref):
    i = pl.program_id(0)
    tm = xm_ref.shape[0]
    _fill_halo_lhs(xs_ref, xm_ref, xp_ref, xn_ref, i == 0, i == pl.num_programs(0) - 1)
    u = _dot(xs_ref[...], w_ref[...])
    p = u[:, D_CONV:2 * D_CONV] * u[:, 2 * D_CONV:3 * D_CONV]
    conv = _dwconv3_rows(p, cw_ref[...], tm)
    a_ref[...] = (u[0:tm, 0:D_CONV] * conv).astype(_BF16)


def _conv_in(xb, w_in, conv_w, l, tm):
    s = xb.shape[0]
    return pl.pallas_call(
        _conv_in_kernel,
        grid=(s // tm,),
        in_specs=_halo_specs(tm, s, D_MODEL, 1) + [
            pl.BlockSpec((None, D_MODEL, 3 * D_CONV), lambda i: (l, 0, _C_CONV // (3 * D_CONV))),
            pl.BlockSpec((None, 3, D_CONV), lambda i: (l, 0, 0)),
        ],
        out_specs=pl.BlockSpec((tm, D_CONV), lambda i: (i, 0)),
        out_shape=jax.ShapeDtypeStruct((s, D_CONV), _BF16),
        scratch_shapes=[pltpu.VMEM((tm + _HALO, D_MODEL), _BF16)],
        compiler_params=_params(("arbitrary",)),
        name="conv_in",
    )(xb, xb, xb, w_in, conv_w)


def _qkvf_kernel(x_ref, wqk_ref, wv_ref, wf_ref, qg_ref, kg_ref, cos_ref, sin_ref, dft_ref,
                 qt_ref, k_ref, vt_ref, fr_ref, fi_ref):
    tm = x_ref.shape[0]
    x = x_ref[...]
    cos = cos_ref[...]
    sin = sin_ref[...]

    def norm_rope(xh, gain):
        ms = jnp.mean(xh * xh, axis=-1, keepdims=True)
        y = xh * lax.rsqrt(ms + QK_EPS) * gain
        return y * cos + pltpu.roll(y, HEAD_DIM // 2, 1) * sin

    u = _dot(x, wqk_ref[...])
    qg = qg_ref[...]
    kg = kg_ref[...]
    for h in range(N_HEADS):
        sl = slice(h * HEAD_DIM, (h + 1) * HEAD_DIM)
        qt_ref[sl, :] = (norm_rope(u[:, sl], qg) * _Q_SCALE).T.astype(_BF16)
    for h in range(N_KV_HEADS):
        sl = slice(h * HEAD_DIM, (h + 1) * HEAD_DIM)
        k_ref[:, sl] = norm_rope(u[:, D_ATTN + h * HEAD_DIM:D_ATTN + (h + 1) * HEAD_DIM], kg).astype(_BF16)
    v = _dot(x, wv_ref[...])
    for h in range(N_KV_HEADS):
        v0 = h * _VT_ROWS
        vt_ref[v0:v0 + HEAD_DIM, :] = v[:, h * HEAD_DIM:(h + 1) * HEAD_DIM].T.astype(_BF16)
        vt_ref[v0 + HEAD_DIM:v0 + _VT_ROWS, :] = jnp.ones((_HALO, tm), _BF16)
    f = _dot(x, wf_ref[...]).astype(_BF16)
    dft = dft_ref[...]
    for g in range(N_FOURIER_GROUPS):
        sl = slice(g * FOURIER_GROUP, (g + 1) * FOURIER_GROUP)
        z = _dot(f[:, sl], dft)
        fr_ref[:, sl] = z[:, 0:FOURIER_GROUP]
        fi_ref[:, sl] = z[:, FOURIER_GROUP:2 * FOURIER_GROUP]


def _qkvf_in(xb, w_qk, w_in, q_gain, k_gain, cos, sin, dft, l, tm):
    s = xb.shape[0]
    c_v = _C_QKVF + D_ATTN + D_KV
    c_f = c_v + D_KV
    row = lambda w: pl.BlockSpec((tm, w), lambda i: (i, 0))
    colt = lambda w: pl.BlockSpec((w, tm), lambda i: (0, i))
    layer = lambda a: pl.BlockSpec((None,) + a.shape[1:], lambda i: (l,) + (0,) * (a.ndim - 1))
    cols = lambda c0, w: pl.BlockSpec((None, D_MODEL, w), lambda i: (l, 0, c0 // w))
    full = lambda a: pl.BlockSpec(a.shape, lambda i: (0,) * a.ndim)
    assert c_v % D_KV == 0 and c_f % D_FOURIER == 0
    return pl.pallas_call(
        _qkvf_kernel,
        grid=(s // tm,),
        in_specs=[row(D_MODEL), layer(w_qk), cols(c_v, D_KV), cols(c_f, D_FOURIER),
                  layer(q_gain), layer(k_gain), row(HEAD_DIM), row(HEAD_DIM), full(dft)],
        out_specs=[colt(D_ATTN), row(D_KV), colt(N_KV_HEADS * _VT_ROWS), row(D_FOURIER), row(D_FOURIER)],
        out_shape=[jax.ShapeDtypeStruct((D_ATTN, s), _BF16),
                   jax.ShapeDtypeStruct((s, D_KV), _BF16),
                   jax.ShapeDtypeStruct((N_KV_HEADS * _VT_ROWS, s), _BF16),
                   jax.ShapeDtypeStruct((s, D_FOURIER), _F32),
                   jax.ShapeDtypeStruct((s, D_FOURIER), _F32)],
        compiler_params=_params(("arbitrary",)),
        name="qkvf_in",
    )(xb, w_qk, w_in, w_in, q_gain, k_gain, cos, sin, dft)


def _four_a_kernel(fr_ref, fi_ref, w_ref, yr_ref, yi_ref):
    n2, tb, _ = fr_ref.shape
    w = w_ref[...]
    for t in range(tb):
        f = jnp.concatenate([fr_ref[:, t, :], fi_ref[:, t, :]], axis=0).astype(_BF16)
        y = _dot(w, f)
        yr_ref[:, t, :] = y[0:n2]
        yi_ref[:, t, :] = y[n2:2 * n2]


def _four_a(fr3, fi3, wa, tb):
    n2, n1, _ = fr3.shape
    dat = pl.BlockSpec((n2, tb, D_FOURIER), lambda j: (0, j, 0))
    return pl.pallas_call(
        _four_a_kernel,
        grid=(n1 // tb,),
        in_specs=[dat, dat, pl.BlockSpec((2 * n2, 2 * n2), lambda j: (0, 0))],
        out_specs=[dat, dat],
        out_shape=[jax.ShapeDtypeStruct((n2, n1, D_FOURIER), _F32)] * 2,
        compiler_params=_params(("arbitrary",)),
        name="four_a",
    )(fr3, fi3, wa)


def _four_c_kernel(yr_ref, yi_ref, m_ref, o_ref):
    for b in range(yr_ref.shape[0]):
        y = jnp.concatenate([yr_ref[b], yi_ref[b]], axis=0).astype(_BF16)
        o_ref[:, b, :] = _dot(m_ref[b], y)


def _four_c(yr3, yi3, mcs, sb):
    n2, n1, _ = yr3.shape
    dat = pl.BlockSpec((sb, n1, D_FOURIER), lambda j: (j, 0, 0))
    return pl.pallas_call(
        _four_c_kernel,
        grid=(n2 // sb,),
        in_specs=[dat, dat, pl.BlockSpec((sb, n1, 2 * n1), lambda j: (j, 0, 0))],
        out_specs=pl.BlockSpec((n1, sb, D_FOURIER), lambda j: (0, j, 0)),
        out_shape=jax.ShapeDtypeStruct((n1, n2, D_FOURIER), _F32),
        compiler_params=_params(("arbitrary",)),
        name="four_c",
    )(yr3, yi3, mcs)


def _dft_tables(s):
    n1 = _DFT_N1
    n2 = s // n1
    two_pi = 2.0 * math.pi
    i2 = jnp.arange(n2, dtype=jnp.int32)
    ang2 = ((i2[:, None] * i2[None, :]) % n2).astype(_F32) * (two_pi / n2)
    c2 = jnp.cos(ang2) * (n2 ** -0.5)
    s2 = jnp.sin(ang2) * (n2 ** -0.5)
    wa = jnp.concatenate([jnp.concatenate([c2, s2], axis=1),
                          jnp.concatenate([-s2, c2], axis=1)], axis=0)
    i1 = jnp.arange(n1, dtype=jnp.int32)
    ang_a = ((i1[:, None] * i1[None, :]) % n1).astype(_F32) * (two_pi / n1)
    ang_b = (i2[:, None] * i1[None, :]).astype(_F32) * (two_pi / s)
    ca, sa = jnp.cos(ang_a)[None], jnp.sin(ang_a)[None]
    cb, sb = jnp.cos(ang_b)[:, None, :], jnp.sin(ang_b)[:, None, :]
    mcs = jnp.concatenate([ca * cb - sa * sb, sa * cb + ca * sb], axis=2) * (n1 ** -0.5)
    return wa.astype(_BF16), mcs.astype(_BF16)


def _channel_dft_matrix():
    i = jnp.arange(FOURIER_GROUP, dtype=jnp.int32)
    ang = ((i[:, None] * i[None, :]) % FOURIER_GROUP).astype(_F32) * (2.0 * math.pi / FOURIER_GROUP)
    scale = FOURIER_GROUP ** -0.5
    return jnp.concatenate([jnp.cos(ang) * scale, -jnp.sin(ang) * scale], axis=1).astype(_BF16)


def _rope_tables(s):
    rows = s // GRID_W
    half = HEAD_DIM // 2
    inv_freq = ROPE_THETA ** (-jnp.arange(0, half, 2, dtype=_F32) / half)
    ar = jnp.arange(rows, dtype=_F32)[:, None] * inv_freq
    ac = jnp.arange(GRID_W, dtype=_F32)[:, None] * inv_freq
    grid = (rows, GRID_W, HEAD_DIM // 4)
    by_row = lambda v: jnp.broadcast_to(v[:, None, :], grid)
    by_col = lambda v: jnp.broadcast_to(v[None, :, :], grid)
    cr, sr, cc, sc = by_row(jnp.cos(ar)), by_row(jnp.sin(ar)), by_col(jnp.cos(ac)), by_col(jnp.sin(ac))
    cos = jnp.concatenate([cr, cc, cr, cc], axis=2).reshape(s, HEAD_DIM)
    sin = jnp.concatenate([-sr, -sc, sr, sc], axis=2).reshape(s, HEAD_DIM)
    return cos, sin


def _pair_major(w):
    lead = w.shape[:-1]
    heads = w.shape[-1] // HEAD_DIM
    w = w.reshape(lead + (heads, 2, 2, HEAD_DIM // 4))
    return jnp.swapaxes(w, -3, -2).reshape(lead + (heads * HEAD_DIM,))


def _attn_kernel(qt_ref, k_ref, vt_ref, o_ref, s0_ref, s1_ref, p0_ref, p1_ref, acc_ref, *, tk):
    tq = qt_ref.shape[1]
    n_chunks = k_ref.shape[0] // tk

    def scores(j, s_ref):
        kc = k_ref[pl.ds(pl.multiple_of(j * tk, tk), tk), :]
        cmax = []
        for g in range(GROUP):
            st = _dot(kc, qt_ref[g * HEAD_DIM:(g + 1) * HEAD_DIM, :])
            s_ref[:, g * tq:(g + 1) * tq] = st
            cmax.append(jnp.max(st, axis=0, keepdims=True))
        return jnp.concatenate(cmax, axis=1)

    def softmax(s_ref, p_ref, m, cmax):
        m_new = jnp.maximum(m, cmax)
        alpha = jnp.exp2(m - m_new)
        p_ref[...] = jnp.exp2((s_ref[...] - m_new).astype(_BF16))
        return m_new, alpha

    def accumulate(j, p_ref, alpha):
        vc = vt_ref[:, pl.ds(pl.multiple_of(j * tk, tk), tk)]
        acc_ref[...] = alpha * acc_ref[...] + _dot(vc, p_ref[...])

    cmax0 = scores(0, s0_ref)
    p1_ref[...] = jnp.zeros_like(p1_ref)
    acc_ref[...] = jnp.zeros_like(acc_ref)

    def body(i, carry):
        m, alpha_prev, cmax_even = carry
        j = 2 * i
        cmax_odd = scores(j + 1, s1_ref)
        m, alpha_even = softmax(s0_ref, p0_ref, m, cmax_even)
        accumulate(jnp.maximum(j - 1, 0), p1_ref, alpha_prev)
        cmax_even = scores(jnp.minimum(j + 2, n_chunks - 1), s0_ref)
        m, alpha_odd = softmax(s1_ref, p1_ref, m, cmax_odd)
        accumulate(j, p0_ref, alpha_even)
        return m, alpha_odd, cmax_even

    m0 = jnp.full((1, GROUP * tq), -jnp.inf, _F32)
    a0 = jnp.ones((1, GROUP * tq), _F32)
    _, alpha_last, _ = lax.fori_loop(0, n_chunks // 2, body, (m0, a0, cmax0))
    accumulate(n_chunks - 1, p1_ref, alpha_last)
    out = acc_ref[0:HEAD_DIM, :] / acc_ref[HEAD_DIM:HEAD_DIM + 1, :]
    for g in range(GROUP):
        o_ref[:, g * HEAD_DIM:(g + 1) * HEAD_DIM] = out[:, g * tq:(g + 1) * tq].T.astype(_BF16)


def _attn_bounded_kernel(qt_ref, k_ref, vt_ref, o_ref, p0_ref, p1_ref, acc_ref, *, tk):
    tq = qt_ref.shape[1]
    mq = GROUP * tq
    n_chunks = k_ref.shape[0] // tk
    sub = 8

    def probs(j, p_ref, l8):
        kc = k_ref[pl.ds(pl.multiple_of(j * tk, tk), tk), :]
        parts = []
        for g in range(GROUP):
            p = jnp.exp2(_dot(kc, qt_ref[g * HEAD_DIM:(g + 1) * HEAD_DIM, :]))
            p_ref[:, g * tq:(g + 1) * tq] = p.astype(_BF16)
            parts.append(jnp.sum(p.reshape(tk // sub, sub, tq), axis=0))
        return l8 + jnp.concatenate(parts, axis=1)

    def accumulate(j, p_ref):
        vc = vt_ref[0:HEAD_DIM, pl.ds(pl.multiple_of(j * tk, tk), tk)]
        acc_ref[...] += _dot(vc, p_ref[...])

    acc_ref[...] = jnp.zeros_like(acc_ref)
    l8 = probs(0, p0_ref, jnp.zeros((sub, mq), _F32))

    def body(i, l8):
        j = 2 * i
        l8 = probs(j + 1, p1_ref, l8)
        accumulate(j, p0_ref)
        l8 = probs(j + 2, p0_ref, l8)
        accumulate(j + 1, p1_ref)
        return l8

    l8 = lax.fori_loop(0, n_chunks // 2 - 1, body, l8)
    l8 = probs(n_chunks - 1, p1_ref, l8)
    accumulate(n_chunks - 2, p0_ref)
    accumulate(n_chunks - 1, p1_ref)
    out = acc_ref[...] / jnp.sum(l8, axis=0, keepdims=True)
    for g in range(GROUP):
        o_ref[:, g * HEAD_DIM:(g + 1) * HEAD_DIM] = out[:, g * tq:(g + 1) * tq].T.astype(_BF16)


def _attn(qt, k, vt, score_bound, tq, tk, tk_bounded):
    s = k.shape[0]
    assert s % (2 * tk) == 0 and s % (2 * tk_bounded) == 0 and s % tq == 0
    mq = GROUP * tq
    bufs = lambda rows, dt: [pltpu.VMEM((rows, mq), dt), pltpu.VMEM((rows, mq), dt)]
    common = dict(
        grid=(N_KV_HEADS, s // tq),
        in_specs=[pl.BlockSpec((GROUP * HEAD_DIM, tq), lambda h, i: (h, i)),
                  pl.BlockSpec((s, HEAD_DIM), lambda h, i: (0, h)),
                  pl.BlockSpec((_VT_ROWS, s), lambda h, i: (h, 0))],
        out_specs=pl.BlockSpec((tq, GROUP * HEAD_DIM), lambda h, i: (i, h)),
        out_shape=jax.ShapeDtypeStruct((s, D_ATTN), _BF16),
        compiler_params=_params(("arbitrary", "arbitrary")),
    )
    bounded = pl.pallas_call(
        functools.partial(_attn_bounded_kernel, tk=tk_bounded),
        scratch_shapes=bufs(tk_bounded, _BF16) + [pltpu.VMEM((HEAD_DIM, mq), _F32)],
        name="attn_bounded", **common)
    online = pl.pallas_call(
        functools.partial(_attn_kernel, tk=tk),
        scratch_shapes=bufs(tk, _F32) + bufs(tk, _BF16) + [pltpu.VMEM((_VT_ROWS, mq), _F32)],
        name="attn", **common)
    return lax.cond(score_bound <= _SCORE_BOUND, bounded, online, qt, k, vt)


def _merge_kernel(xb_ref, xf_ref, a_ref, o_ref, fr_ref, wg0_ref, wg1_ref, wg2_ref,
                  wco_ref, wao_ref, wfo_ref, wo_ref, g_ref, b_ref, outf_ref, outb_ref):
    c = pl.program_id(1)

    @pl.when(c == 0)
    def _():
        outf_ref[...] = DEEPNORM_ALPHA * xf_ref[...]

    xb = xb_ref[...]
    m = jax.nn.sigmoid(_dot(xb, wg0_ref[...])) * _dot(a_ref[...], wco_ref[...])
    m = m + jax.nn.sigmoid(_dot(xb, wg1_ref[...])) * _dot(o_ref[...], wao_ref[...])
    m = m + jax.nn.sigmoid(_dot(xb, wg2_ref[...])) * _dot(fr_ref[...].astype(_BF16), wfo_ref[...])
    outf_ref[...] += _dot(m.astype(_BF16), wo_ref[...])

    @pl.when(c == pl.num_programs(1) - 1)
    def _():
        y = _layer_norm(outf_ref[...], g_ref[...], b_ref[...])
        outf_ref[...] = y
        outb_ref[...] = y.astype(_BF16)


def _merge(xb, xf, a, o, fr, w_in, w_co, w_ao, w_fo, w_o, ln_g, ln_b, l, tm, tn):
    s = xb.shape[0]
    nc = D_MODEL // tn
    gate0 = _C_GATE // tn
    row = lambda w: pl.BlockSpec((tm, w), lambda i, c: (i, 0))
    colw = lambda k: pl.BlockSpec((None, k, tn), lambda i, c: (l, 0, c))
    gate = lambda b: pl.BlockSpec((None, D_MODEL, tn), lambda i, c: (l, 0, gate0 + b * nc + c))
    vec = pl.BlockSpec((None, 1, D_MODEL), lambda i, c: (l, 0, 0))
    return pl.pallas_call(
        _merge_kernel,
        grid=(s // tm, nc),
        in_specs=[row(D_MODEL), row(D_MODEL), row(D_CONV), row(D_ATTN), row(D_FOURIER),
                  gate(0), gate(1), gate(2),
                  colw(D_CONV), colw(D_ATTN), colw(D_FOURIER),
                  pl.BlockSpec((None, tn, D_MODEL), lambda i, c: (l, c, 0)), vec, vec],
        out_specs=[row(D_MODEL), row(D_MODEL)],
        out_shape=[jax.ShapeDtypeStruct((s, D_MODEL), _F32), jax.ShapeDtypeStruct((s, D_MODEL), _BF16)],
        compiler_params=_params(("arbitrary", "arbitrary")),
        name="merge",
    )(xb, xf, a, o, fr, w_in, w_in, w_in, w_co, w_ao, w_fo, w_o, ln_g, ln_b)


def _ffn_kernel(xm_ref, xp_ref, xn_ref, xf_ref, wg_ref, wv_ref, cw_ref, wd_ref, g_ref, b_ref,
                outf_ref, outb_ref, xs_ref):
    i = pl.program_id(0)
    j = pl.program_id(1)
    tm = xm_ref.shape[0]

    @pl.when(j == 0)
    def _():
        _fill_halo_lhs(xs_ref, xm_ref, xp_ref, xn_ref, i == 0, i == pl.num_programs(0) - 1)
        outf_ref[...] = DEEPNORM_ALPHA * xf_ref[...]

    hg = _dot(xs_ref[...], wg_ref[...])
    hv = _dot(xs_ref[0:tm, :], wv_ref[...])
    h = jax.nn.silu(_dwconv3_rows(hg, cw_ref[...], tm)) * hv
    outf_ref[...] += _dot(h.astype(_BF16), wd_ref[...])

    @pl.when(j == pl.num_programs(1) - 1)
    def _():
        y = _layer_norm(outf_ref[...], g_ref[...], b_ref[...])
        outf_ref[...] = y
        outb_ref[...] = y.astype(_BF16)


def _ffn(xb, xf, w_up, ffn_conv_w, w_down, ln_g, ln_b, l, tm, tn):
    s = xb.shape[0]
    nj = D_FF // tn
    row = pl.BlockSpec((tm, D_MODEL), lambda i, j: (i, 0))
    vec = pl.BlockSpec((None, 1, D_MODEL), lambda i, j: (l, 0, 0))
    return pl.pallas_call(
        _ffn_kernel,
        grid=(s // tm, nj),
        in_specs=_halo_specs(tm, s, D_MODEL, 2) + [
            row,
            pl.BlockSpec((None, D_MODEL, tn), lambda i, j: (l, 0, j)),
            pl.BlockSpec((None, D_MODEL, tn), lambda i, j: (l, 0, nj + j)),
            pl.BlockSpec((None, 3, tn), lambda i, j: (l, 0, j)),
            pl.BlockSpec((None, tn, D_MODEL), lambda i, j: (l, j, 0)),
            vec, vec],
        out_specs=[row, row],
        out_shape=[jax.ShapeDtypeStruct((s, D_MODEL), _F32), jax.ShapeDtypeStruct((s, D_MODEL), _BF16)],
        scratch_shapes=[pltpu.VMEM((tm + _HALO, D_MODEL), _BF16)],
        compiler_params=_params(("arbitrary", "arbitrary")),
        name="ffn",
    )(xb, xb, xb, xf, w_up, w_up, ffn_conv_w, w_down, ln_g, ln_b)


def _prep_weights(w_in, conv_w, q_gain, k_gain, w_conv_out, w_attn_out, w_fourier_out, w_o,
                  ln1_g, ln1_b, w_up, ffn_conv_w, w_down, ln2_g, ln2_b):
    depth = w_in.shape[0]
    vec = lambda v: v.reshape(depth, 1, v.shape[-1])
    qk0, qk1 = _C_QKVF, _C_QKVF + D_ATTN + D_KV
    return dict(
        depth=depth,
        w_in=w_in.astype(_BF16),
        w_qk=_pair_major(w_in[:, :, qk0:qk1]).astype(_BF16),
        conv_w=conv_w,
        q_gain=vec(_pair_major(q_gain)), k_gain=vec(_pair_major(k_gain)),
        w_co=w_conv_out.astype(_BF16), w_ao=w_attn_out.astype(_BF16), w_fo=w_fourier_out.astype(_BF16),
        w_o=w_o.astype(_BF16),
        ln1_g=vec(ln1_g), ln1_b=vec(ln1_b),
        w_up=w_up.astype(_BF16), ffn_conv_w=ffn_conv_w, w_down=w_down.astype(_BF16),
        ln2_g=vec(ln2_g), ln2_b=vec(ln2_b),
    )


def _trunk(x, w):
    s = x.shape[0]
    tm = min(512, s)
    n1 = _DFT_N1
    n2 = s // n1
    cos, sin = _rope_tables(s)
    wa, mcs = _dft_tables(s)
    dft = _channel_dft_matrix()
    xf = x
    xb = x.astype(_BF16)
    for l in range(w["depth"]):
        a = _conv_in(xb, w["w_in"], w["conv_w"], l, tm)
        qt, k, vt, fr, fi = _qkvf_in(xb, w["w_qk"], w["w_in"], w["q_gain"], w["k_gain"], cos, sin, dft, l, tm)
        yr, yi = _four_a(fr.reshape(n2, n1, D_FOURIER), fi.reshape(n2, n1, D_FOURIER), wa, 8)
        fmix = _four_c(yr, yi, mcs, min(8, n2)).reshape(s, D_FOURIER)
        score_bound = ((1.02 * HEAD_DIM * _Q_SCALE) * jnp.max(jnp.abs(w["q_gain"][l]))
                       * jnp.max(jnp.abs(w["k_gain"][l])))
        o = _attn(qt, k, vt, score_bound, min(512, s), min(1024, s // 2), min(2048, s // 2))
        xf, xb = _merge(xb, xf, a, o, fmix, w["w_in"], w["w_co"], w["w_ao"], w["w_fo"], w["w_o"],
                        w["ln1_g"], w["ln1_b"], l, tm, 512)
        xf, xb = _ffn(xb, xf, w["w_up"], w["ffn_conv_w"], w["w_down"], w["ln2_g"], w["ln2_b"], l, tm, 512)
    return xf


def kernel(x_prompt, x_sample, w_in, conv_w, q_gain, k_gain, w_conv_out, w_attn_out, w_fourier_out,
           w_o, ln1_g, ln1_b, w_up, ffn_conv_w, w_down, ln2_g, ln2_b):
    layers = _prep_weights(w_in, conv_w, q_gain, k_gain, w_conv_out, w_attn_out, w_fourier_out, w_o,
                           ln1_g, ln1_b, w_up, ffn_conv_w, w_down, ln2_g, ln2_b)
    outs = []
    for x in (x_prompt, x_sample):
        b, s, d = x.shape
        rows = x.reshape(b * s, d)
        ys = [_trunk(rows[bi * s:(bi + 1) * s], layers) for bi in range(b)]
        outs.append((ys[0] if b == 1 else jnp.concatenate(ys, axis=0)).reshape(b, s, d))
    return tuple(outs)
```

```python
import functools
import math

import jax
import jax.numpy as jnp
from jax import lax
from jax.experimental import pallas as pl
from jax.experimental.pallas import tpu as pltpu

D_MODEL = 2048
DEPTH = 4
GRID_W = 64
D_CONV = 512
N_HEADS = 8
N_KV_HEADS = 2
HEAD_DIM = 128
GROUP = N_HEADS // N_KV_HEADS
D_ATTN = N_HEADS * HEAD_DIM
D_KV = N_KV_HEADS * HEAD_DIM
ROPE_THETA = 10000.0
N_FOURIER_GROUPS = 4
FOURIER_GROUP = 128
D_FOURIER = N_FOURIER_GROUPS * FOURIER_GROUP
D_FF = 5632
LN_EPS = 1e-5
QK_EPS = 1e-6
DEEPNORM_ALPHA = (2 * DEPTH) ** 0.25

_C_CONV = 0
_C_QKVF = 3 * D_CONV
_C_GATE = _C_QKVF + D_ATTN + 2 * D_KV + D_FOURIER

_HALO = 16
_ROW_TILE = 512
_ROW_TILE_CONV = 1024
_COL_CHUNK = 512
_Q_TILE = 512
_K_CHUNK_ONLINE = 1024
_K_CHUNK_BOUNDED = 2048
_DFT_ROWS = 8
_DFT_N1 = 128
_VT_ROWS = HEAD_DIM + _HALO
_Q_SCALE = HEAD_DIM ** -0.5 * math.log2(math.e)
_SCORE_BOUND = 60.0
_VMEM_LIMIT = 58 * 1024 * 1024

_F32 = jnp.float32
_BF16 = jnp.bfloat16


def _dot(a, b):
    return jnp.dot(a, b, preferred_element_type=_F32)


def _layer_norm(y, g, b):
    mu = jnp.mean(y, axis=-1, keepdims=True)
    yc = y - mu
    var = jnp.mean(yc * yc, axis=-1, keepdims=True)
    return yc * lax.rsqrt(var + LN_EPS) * g + b


def _params(sem):
    return pltpu.CompilerParams(dimension_semantics=sem, vmem_limit_bytes=_VMEM_LIMIT)


def _fill_halo_lhs(xs_ref, xm_ref, xp_ref, xn_ref, first, last):
    tm = xm_ref.shape[0]
    half = _HALO // 2
    pv = jnp.where(first, 0.0, xp_ref[...].astype(_F32))
    nx = jnp.where(last, 0.0, xn_ref[...].astype(_F32))
    xs_ref[0:tm, :] = xm_ref[...]
    xs_ref[tm:tm + _HALO, :] = jnp.concatenate([nx[0:half], pv[half:_HALO]], axis=0).astype(_BF16)


def _dwconv3_rows(p_ext, cw, tm):
    n = p_ext.shape[0]
    prev = pltpu.roll(p_ext, 1, 0)[0:tm]
    nxt = pltpu.roll(p_ext, n - 1, 0)[0:tm]
    return prev * cw[0:1, :] + p_ext[0:tm] * cw[1:2, :] + nxt * cw[2:3, :]


def _halo_specs(tm, n_rows, width, ngrid):
    hb = tm // _HALO
    last_blk = n_rows // _HALO - 1
    if ngrid == 1:
        return [
            pl.BlockSpec((tm, width), lambda i: (i, 0)),
            pl.BlockSpec((_HALO, width), lambda i: (jnp.maximum(i * hb - 1, 0), 0)),
            pl.BlockSpec((_HALO, width), lambda i: (jnp.minimum((i + 1) * hb, last_blk), 0)),
        ]
    return [
        pl.BlockSpec((tm, width), lambda i, j: (i, 0)),
        pl.BlockSpec((_HALO, width), lambda i, j: (jnp.maximum(i * hb - 1, 0), 0)),
        pl.BlockSpec((_HALO, width), lambda i, j: (jnp.minimum((i + 1) * hb, last_blk), 0)),
    ]


def _conv_in_kernel(xm_ref, xp_ref, xn_ref, w_ref, cw_ref, a_ref, xs_ref):
    i = pl.program_id(0)
    tm = xm_ref.shape[0]
    _fill_halo_lhs(xs_ref, xm_ref, xp_ref, xn_ref, i == 0, i == pl.num_programs(0) - 1)
    u = _dot(xs_ref[...], w_ref[...])
    p = u[:, D_CONV:2 * D_CONV] * u[:, 2 * D_CONV:3 * D_CONV]
    conv = _dwconv3_rows(p, cw_ref[...], tm)
    a_ref[...] = (u[0:tm, 0:D_CONV] * conv).astype(_BF16)


def _conv_in(xb, w_in, conv_w, l, tm):
    s = xb.shape[0]
    return pl.pallas_call(
        _conv_in_kernel,
        grid=(s // tm,),
        in_specs=_halo_specs(tm, s, D_MODEL, 1) + [
            pl.BlockSpec((None, D_MODEL, 3 * D_CONV), lambda i: (l, 0, _C_CONV // (3 * D_CONV))),
            pl.BlockSpec((None, 3, D_CONV), lambda i: (l, 0, 0)),
        ],
        out_specs=pl.BlockSpec((tm, D_CONV), lambda i: (i, 0)),
        out_shape=jax.ShapeDtypeStruct((s, D_CONV), _BF16),
        scratch_shapes=[pltpu.VMEM((tm + _HALO, D_MODEL), _BF16)],
        compiler_params=_params(("arbitrary",)),
        name="conv_in",
    )(xb, xb, xb, w_in, conv_w)


def _qkvf_kernel(x_ref, wqk_ref, wv_ref, wf_ref, qg_ref, kg_ref, cos_ref, sin_ref, dft_ref,
                 qt_ref, k_ref, vt_ref, fr_ref, fi_ref):
    tm = x_ref.shape[0]
    x = x_ref[...]
    cos = cos_ref[...]
    sin = sin_ref[...]

    def norm_rope(xh, gain):
        ms = jnp.mean(xh * xh, axis=-1, keepdims=True)
        y = xh * lax.rsqrt(ms + QK_EPS) * gain
        return y * cos + pltpu.roll(y, HEAD_DIM // 2, 1) * sin

    u = _dot(x, wqk_ref[...])
    qg = qg_ref[...]
    kg = kg_ref[...]
    for h in range(N_HEADS):
        sl = slice(h * HEAD_DIM, (h + 1) * HEAD_DIM)
        qt_ref[sl, :] = (norm_rope(u[:, sl], qg) * _Q_SCALE).T.astype(_BF16)
    for h in range(N_KV_HEADS):
        sl = slice(h * HEAD_DIM, (h + 1) * HEAD_DIM)
        k_ref[:, sl] = norm_rope(u[:, D_ATTN + h * HEAD_DIM:D_ATTN + (h + 1) * HEAD_DIM], kg).astype(_BF16)
    v = _dot(x, wv_ref[...])
    for h in range(N_KV_HEADS):
        v0 = h * _VT_ROWS
        vt_ref[v0:v0 + HEAD_DIM, :] = v[:, h * HEAD_DIM:(h + 1) * HEAD_DIM].T.astype(_BF16)
        vt_ref[v0 + HEAD_DIM:v0 + _VT_ROWS, :] = jnp.ones((_HALO, tm), _BF16)
    f = _dot(x, wf_ref[...]).astype(_BF16)
    dft = dft_ref[...]
    for g in range(N_FOURIER_GROUPS):
        sl = slice(g * FOURIER_GROUP, (g + 1) * FOURIER_GROUP)
        z = _dot(f[:, sl], dft)
        fr_ref[:, sl] = z[:, 0:FOURIER_GROUP]
        fi_ref[:, sl] = z[:, FOURIER_GROUP:2 * FOURIER_GROUP]


def _qkvf_in(xb, w_qk, w_in, q_gain, k_gain, cos, sin, dft, l, tm):
    s = xb.shape[0]
    c_v = _C_QKVF + D_ATTN + D_KV
    c_f = c_v + D_KV
    row = lambda w: pl.BlockSpec((tm, w), lambda i: (i, 0))
    colt = lambda w: pl.BlockSpec((w, tm), lambda i: (0, i))
    layer = lambda a: pl.BlockSpec((None,) + a.shape[1:], lambda i: (l,) + (0,) * (a.ndim - 1))
    cols = lambda c0, w: pl.BlockSpec((None, D_MODEL, w), lambda i: (l, 0, c0 // w))
    full = lambda a: pl.BlockSpec(a.shape, lambda i: (0,) * a.ndim)
    assert c_v % D_KV == 0 and c_f % D_FOURIER == 0
    return pl.pallas_call(
        _qkvf_kernel,
        grid=(s // tm,),
        in_specs=[row(D_MODEL), layer(w_qk), cols(c_v, D_KV), cols(c_f, D_FOURIER),
                  layer(q_gain), layer(k_gain), row(HEAD_DIM), row(HEAD_DIM), full(dft)],
        out_specs=[colt(D_ATTN), row(D_KV), colt(N_KV_HEADS * _VT_ROWS), row(D_FOURIER), row(D_FOURIER)],
        out_shape=[jax.ShapeDtypeStruct((D_ATTN, s), _BF16),
                   jax.ShapeDtypeStruct((s, D_KV), _BF16),
                   jax.ShapeDtypeStruct((N_KV_HEADS * _VT_ROWS, s), _BF16),
                   jax.ShapeDtypeStruct((s, D_FOURIER), _F32),
                   jax.ShapeDtypeStruct((s, D_FOURIER), _F32)],
        compiler_params=_params(("arbitrary",)),
        name="qkvf_in",
    )(xb, w_qk, w_in, w_in, q_gain, k_gain, cos, sin, dft)


def _four_a_kernel(fr_ref, fi_ref, w_ref, yr_ref, yi_ref):
    n2, tb, _ = fr_ref.shape
    w = w_ref[...]
    for t in range(tb):
        f = jnp.concatenate([fr_ref[:, t, :], fi_ref[:, t, :]], axis=0).astype(_BF16)
        y = _dot(w, f)
        yr_ref[:, t, :] = y[0:n2]
        yi_ref[:, t, :] = y[n2:2 * n2]


def _four_a(fr3, fi3, wa, tb):
    n2, n1, _ = fr3.shape
    dat = pl.BlockSpec((n2, tb, D_FOURIER), lambda j: (0, j, 0))
    return pl.pallas_call(
        _four_a_kernel,
        grid=(n1 // tb,),
        in_specs=[dat, dat, pl.BlockSpec((2 * n2, 2 * n2), lambda j: (0, 0))],
        out_specs=[dat, dat],
        out_shape=[jax.ShapeDtypeStruct((n2, n1, D_FOURIER), _F32)] * 2,
        compiler_params=_params(("arbitrary",)),
        name="four_a",
    )(fr3, fi3, wa)


def _four_c_kernel(yr_ref, yi_ref, m_ref, o_ref):
    for b in range(yr_ref.shape[0]):
        y = jnp.concatenate([yr_ref[b], yi_ref[b]], axis=0).astype(_BF16)
        o_ref[:, b, :] = _dot(m_ref[b], y)


def _four_c(yr3, yi3, mcs, sb):
    n2, n1, _ = yr3.shape
    dat = pl.BlockSpec((sb, n1, D_FOURIER), lambda j: (j, 0, 0))
    return pl.pallas_call(
        _four_c_kernel,
        grid=(n2 // sb,),
        in_specs=[dat, dat, pl.BlockSpec((sb, n1, 2 * n1), lambda j: (j, 0, 0))],
        out_specs=pl.BlockSpec((n1, sb, D_FOURIER), lambda j: (0, j, 0)),
        out_shape=jax.ShapeDtypeStruct((n1, n2, D_FOURIER), _F32),
        compiler_params=_params(("arbitrary",)),
        name="four_c",
    )(yr3, yi3, mcs)


def _dft_tables(s):
    n1 = _DFT_N1
    n2 = s // n1
    two_pi = 2.0 * math.pi
    i2 = jnp.arange(n2, dtype=jnp.int32)
    ang2 = ((i2[:, None] * i2[None, :]) % n2).astype(_F32) * (two_pi / n2)
    c2 = jnp.cos(ang2) * (n2 ** -0.5)
    s2 = jnp.sin(ang2) * (n2 ** -0.5)
    wa = jnp.concatenate([jnp.concatenate([c2, s2], axis=1),
                          jnp.concatenate([-s2, c2], axis=1)], axis=0)
    i1 = jnp.arange(n1, dtype=jnp.int32)
    ang_a = ((i1[:, None] * i1[None, :]) % n1).astype(_F32) * (two_pi / n1)
    ang_b = (i2[:, None] * i1[None, :]).astype(_F32) * (two_pi / s)
    ca, sa = jnp.cos(ang_a)[None], jnp.sin(ang_a)[None]
    cb, sb = jnp.cos(ang_b)[:, None, :], jnp.sin(ang_b)[:, None, :]
    mcs = jnp.concatenate([ca * cb - sa * sb, sa * cb + ca * sb], axis=2) * (n1 ** -0.5)
    return wa.astype(_BF16), mcs.astype(_BF16)


def _channel_dft_matrix():
    i = jnp.arange(FOURIER_GROUP, dtype=jnp.int32)
    ang = ((i[:, None] * i[None, :]) % FOURIER_GROUP).astype(_F32) * (2.0 * math.pi / FOURIER_GROUP)
    scale = FOURIER_GROUP ** -0.5
    return jnp.concatenate([jnp.cos(ang) * scale, -jnp.sin(ang) * scale], axis=1).astype(_BF16)


def _rope_tables(s):
    rows = s // GRID_W
    half = HEAD_DIM // 2
    inv_freq = ROPE_THETA ** (-jnp.arange(0, half, 2, dtype=_F32) / half)
    ar = jnp.arange(rows, dtype=_F32)[:, None] * inv_freq
    ac = jnp.arange(GRID_W, dtype=_F32)[:, None] * inv_freq
    grid = (rows, GRID_W, HEAD_DIM // 4)
    by_row = lambda v: jnp.broadcast_to(v[:, None, :], grid)
    by_col = lambda v: jnp.broadcast_to(v[None, :, :], grid)
    cr, sr, cc, sc = by_row(jnp.cos(ar)), by_row(jnp.sin(ar)), by_col(jnp.cos(ac)), by_col(jnp.sin(ac))
    cos = jnp.concatenate([cr, cc, cr, cc], axis=2).reshape(s, HEAD_DIM)
    sin = jnp.concatenate([-sr, -sc, sr, sc], axis=2).reshape(s, HEAD_DIM)
    return cos, sin


def _pair_major(w):
    lead = w.shape[:-1]
    heads = w.shape[-1] // HEAD_DIM
    w = w.reshape(lead + (heads, 2, 2, HEAD_DIM // 4))
    return jnp.swapaxes(w, -3, -2).reshape(lead + (heads * HEAD_DIM,))


def _attn_kernel(qt_ref, k_ref, vt_ref, o_ref, s0_ref, s1_ref, p0_ref, p1_ref, acc_ref, *, tk):
    tq = qt_ref.shape[1]
    n_chunks = k_ref.shape[0] // tk

    def scores(j, s_ref):
        kc = k_ref[pl.ds(pl.multiple_of(j * tk, tk), tk), :]
        cmax = []
        for g in range(GROUP):
            st = _dot(kc, qt_ref[g * HEAD_DIM:(g + 1) * HEAD_DIM, :])
            s_ref[:, g * tq:(g + 1) * tq] = st
            cmax.append(jnp.max(st, axis=0, keepdims=True))
        return jnp.concatenate(cmax, axis=1)

    def softmax(s_ref, p_ref, m, cmax):
        m_new = jnp.maximum(m, cmax)
        alpha = jnp.exp2(m - m_new)
        p_ref[...] = jnp.exp2((s_ref[...] - m_new).astype(_BF16))
        return m_new, alpha

    def accumulate(j, p_ref, alpha):
        vc = vt_ref[:, pl.ds(pl.multiple_of(j * tk, tk), tk)]
        acc_ref[...] = alpha * acc_ref[...] + _dot(vc, p_ref[...])

    cmax0 = scores(0, s0_ref)
    p1_ref[...] = jnp.zeros_like(p1_ref)
    acc_ref[...] = jnp.zeros_like(acc_ref)

    def body(i, carry):
        m, alpha_prev, cmax_even = carry
        j = 2 * i
        cmax_odd = scores(j + 1, s1_ref)
        m, alpha_even = softmax(s0_ref, p0_ref, m, cmax_even)
        accumulate(jnp.maximum(j - 1, 0), p1_ref, alpha_prev)
        cmax_even = scores(jnp.minimum(j + 2, n_chunks - 1), s0_ref)
        m, alpha_odd = softmax(s1_ref, p1_ref, m, cmax_odd)
        accumulate(j, p0_ref, alpha_even)
        return m, alpha_odd, cmax_even

    m0 = jnp.full((1, GROUP * tq), -jnp.inf, _F32)
    a0 = jnp.ones((1, GROUP * tq), _F32)
    _, alpha_last, _ = lax.fori_loop(0, n_chunks // 2, body, (m0, a0, cmax0))
    accumulate(n_chunks - 1, p1_ref, alpha_last)
    out = acc_ref[0:HEAD_DIM, :] / acc_ref[HEAD_DIM:HEAD_DIM + 1, :]
    for g in range(GROUP):
        o_ref[:, g * HEAD_DIM:(g + 1) * HEAD_DIM] = out[:, g * tq:(g + 1) * tq].T.astype(_BF16)


def _attn_bounded_kernel(qt_ref, k_ref, vt_ref, o_ref, p0_ref, p1_ref, acc_ref, *, tk):
    tq = qt_ref.shape[1]
    mq = GROUP * tq
    n_chunks = k_ref.shape[0] // tk
    sub = 8

    def probs(j, p_ref, l8):
        kc = k_ref[pl.ds(pl.multiple_of(j * tk, tk), tk), :]
        parts = []
        for g in range(GROUP):
            p = jnp.exp2(_dot(kc, qt_ref[g * HEAD_DIM:(g + 1) * HEAD_DIM, :]))
            p_ref[:, g * tq:(g + 1) * tq] = p.astype(_BF16)
            parts.append(jnp.sum(p.reshape(tk // sub, sub, tq), axis=0))
        return l8 + jnp.concatenate(parts, axis=1)

    def accumulate(j, p_ref):
        vc = vt_ref[0:HEAD_DIM, pl.ds(pl.multiple_of(j * tk, tk), tk)]
        acc_ref[...] += _dot(vc, p_ref[...])

    acc_ref[...] = jnp.zeros_like(acc_ref)
    l8 = probs(0, p0_ref, jnp.zeros((sub, mq), _F32))

    def body(i, l8):
        j = 2 * i
        l8 = probs(j + 1, p1_ref, l8)
        accumulate(j, p0_ref)
        l8 = probs(j + 2, p0_ref, l8)
        accumulate(j + 1, p1_ref)
        return l8

    l8 = lax.fori_loop(0, n_chunks // 2 - 1, body, l8)
    l8 = probs(n_chunks - 1, p1_ref, l8)
    accumulate(n_chunks - 2, p0_ref)
    accumulate(n_chunks - 1, p1_ref)
    out = acc_ref[...] / jnp.sum(l8, axis=0, keepdims=True)
    for g in range(GROUP):
        o_ref[:, g * HEAD_DIM:(g + 1) * HEAD_DIM] = out[:, g * tq:(g + 1) * tq].T.astype(_BF16)


def _attn(qt, k, vt, score_bound, tq, tk, tk_bounded):
    s = k.shape[0]
    assert s % (2 * tk) == 0 and s % (2 * tk_bounded) == 0 and s % tq == 0
    mq = GROUP * tq
    bufs = lambda rows, dt: [pltpu.VMEM((rows, mq), dt), pltpu.VMEM((rows, mq), dt)]
    common = dict(
        grid=(N_KV_HEADS, s // tq),
        in_specs=[pl.BlockSpec((GROUP * HEAD_DIM, tq), lambda h, i: (h, i)),
                  pl.BlockSpec((s, HEAD_DIM), lambda h, i: (0, h)),
                  pl.BlockSpec((_VT_ROWS, s), lambda h, i: (h, 0))],
        out_specs=pl.BlockSpec((tq, GROUP * HEAD_DIM), lambda h, i: (i, h)),
        out_shape=jax.ShapeDtypeStruct((s, D_ATTN), _BF16),
        compiler_params=_params(("arbitrary", "arbitrary")),
    )
    bounded = pl.pallas_call(
        functools.partial(_attn_bounded_kernel, tk=tk_bounded),
        scratch_shapes=bufs(tk_bounded, _BF16) + [pltpu.VMEM((HEAD_DIM, mq), _F32)],
        name="attn_bounded", **common)
    online = pl.pallas_call(
        functools.partial(_attn_kernel, tk=tk),
        scratch_shapes=bufs(tk, _F32) + bufs(tk, _BF16) + [pltpu.VMEM((_VT_ROWS, mq), _F32)],
        name="attn", **common)
    return lax.cond(score_bound <= _SCORE_BOUND, bounded, online, qt, k, vt)


def _merge_kernel(xb_ref, xf_ref, a_ref, o_ref, fr_ref, wg0_ref, wg1_ref, wg2_ref,
                  wco_ref, wao_ref, wfo_ref, wo_ref, g_ref, b_ref, outf_ref, outb_ref):
    c = pl.program_id(1)

    @pl.when(c == 0)
    def _():
        outf_ref[...] = DEEPNORM_ALPHA * xf_ref[...]

    xb = xb_ref[...]
    m = jax.nn.sigmoid(_dot(xb, wg0_ref[...])) * _dot(a_ref[...], wco_ref[...])
    m = m + jax.nn.sigmoid(_dot(xb, wg1_ref[...])) * _dot(o_ref[...], wao_ref[...])
    m = m + jax.nn.sigmoid(_dot(xb, wg2_ref[...])) * _dot(fr_ref[...].astype(_BF16), wfo_ref[...])
    outf_ref[...] += _dot(m.astype(_BF16), wo_ref[...])

    @pl.when(c == pl.num_programs(1) - 1)
    def _():
        y = _layer_norm(outf_ref[...], g_ref[...], b_ref[...])
        outf_ref[...] = y
        outb_ref[...] = y.astype(_BF16)


def _merge(xb, xf, a, o, fr, w_in, w_co, w_ao, w_fo, w_o, ln_g, ln_b, l, tm, tn):
    s = xb.shape[0]
    nc = D_MODEL // tn
    gate0 = _C_GATE // tn
    row = lambda w: pl.BlockSpec((tm, w), lambda i, c: (i, 0))
    colw = lambda k: pl.BlockSpec((None, k, tn), lambda i, c: (l, 0, c))
    gate = lambda b: pl.BlockSpec((None, D_MODEL, tn), lambda i, c: (l, 0, gate0 + b * nc + c))
    vec = pl.BlockSpec((None, 1, D_MODEL), lambda i, c: (l, 0, 0))
    return pl.pallas_call(
        _merge_kernel,
        grid=(s // tm, nc),
        in_specs=[row(D_MODEL), row(D_MODEL), row(D_CONV), row(D_ATTN), row(D_FOURIER),
                  gate(0), gate(1), gate(2),
                  colw(D_CONV), colw(D_ATTN), colw(D_FOURIER),
                  pl.BlockSpec((None, tn, D_MODEL), lambda i, c: (l, c, 0)), vec, vec],
        out_specs=[row(D_MODEL), row(D_MODEL)],
        out_shape=[jax.ShapeDtypeStruct((s, D_MODEL), _F32), jax.ShapeDtypeStruct((s, D_MODEL), _BF16)],
        compiler_params=_params(("arbitrary", "arbitrary")),
        name="merge",
    )(xb, xf, a, o, fr, w_in, w_in, w_in, w_co, w_ao, w_fo, w_o, ln_g, ln_b)


def _ffn_kernel(xm_ref, xp_ref, xn_ref, xf_ref, wg_ref, wv_ref, cw_ref, wd_ref, g_ref, b_ref,
                outf_ref, outb_ref, xs_ref):
    i = pl.program_id(0)
    j = pl.program_id(1)
    tm = xm_ref.shape[0]

    @pl.when(j == 0)
    def _():
        _fill_halo_lhs(xs_ref, xm_ref, xp_ref, xn_ref, i == 0, i == pl.num_programs(0) - 1)
        outf_ref[...] = DEEPNORM_ALPHA * xf_ref[...]

    hg = _dot(xs_ref[...], wg_ref[...])
    hv = _dot(xs_ref[0:tm, :], wv_ref[...])
    h = jax.nn.silu(_dwconv3_rows(hg, cw_ref[...], tm)) * hv
    outf_ref[...] += _dot(h.astype(_BF16), wd_ref[...])

    @pl.when(j == pl.num_programs(1) - 1)
    def _():
        y = _layer_norm(outf_ref[...], g_ref[...], b_ref[...])
        outf_ref[...] = y
        outb_ref[...] = y.astype(_BF16)


def _ffn(xb, xf, w_up, ffn_conv_w, w_down, ln_g, ln_b, l, tm, tn):
    s = xb.shape[0]
    nj = D_FF // tn
    row = pl.BlockSpec((tm, D_MODEL), lambda i, j: (i, 0))
    vec = pl.BlockSpec((None, 1, D_MODEL), lambda i, j: (l, 0, 0))
    return pl.pallas_call(
        _ffn_kernel,
        grid=(s // tm, nj),
        in_specs=_halo_specs(tm, s, D_MODEL, 2) + [
            row,
            pl.BlockSpec((None, D_MODEL, tn), lambda i, j: (l, 0, j)),
            pl.BlockSpec((None, D_MODEL, tn), lambda i, j: (l, 0, nj + j)),
            pl.BlockSpec((None, 3, tn), lambda i, j: (l, 0, j)),
            pl.BlockSpec((None, tn, D_MODEL), lambda i, j: (l, j, 0)),
            vec, vec],
        out_specs=[row, row],
        out_shape=[jax.ShapeDtypeStruct((s, D_MODEL), _F32), jax.ShapeDtypeStruct((s, D_MODEL), _BF16)],
        scratch_shapes=[pltpu.VMEM((tm + _HALO, D_MODEL), _BF16)],
        compiler_params=_params(("arbitrary", "arbitrary")),
        name="ffn",
    )(xb, xb, xb, xf, w_up, w_up, ffn_conv_w, w_down, ln_g, ln_b)


def _prep_weights(w_in, conv_w, q_gain, k_gain, w_conv_out, w_attn_out, w_fourier_out, w_o,
                  ln1_g, ln1_b, w_up, ffn_conv_w, w_down, ln2_g, ln2_b):
    depth = w_in.shape[0]
    vec = lambda v: v.reshape(depth, 1, v.shape[-1])
    qk0, qk1 = _C_QKVF, _C_QKVF + D_ATTN + D_KV
    return dict(
        depth=depth,
        w_in=w_in.astype(_BF16),
        w_qk=_pair_major(w_in[:, :, qk0:qk1]).astype(_BF16),
        conv_w=conv_w,
        q_gain=vec(_pair_major(q_gain)), k_gain=vec(_pair_major(k_gain)),
        w_co=w_conv_out.astype(_BF16), w_ao=w_attn_out.astype(_BF16), w_fo=w_fourier_out.astype(_BF16),
        w_o=w_o.astype(_BF16),
        ln1_g=vec(ln1_g), ln1_b=vec(ln1_b),
        w_up=w_up.astype(_BF16), ffn_conv_w=ffn_conv_w, w_down=w_down.astype(_BF16),
        ln2_g=vec(ln2_g), ln2_b=vec(ln2_b),
    )


def _trunk(x, w):
    s = x.shape[0]
    tm = min(_ROW_TILE, s)
    n1 = _DFT_N1
    n2 = s // n1
    cos, sin = _rope_tables(s)
    wa, mcs = _dft_tables(s)
    dft = _channel_dft_matrix()
    xf = x
    xb = x.astype(_BF16)
    for l in range(w["depth"]):
        a = _conv_in(xb, w["w_in"], w["conv_w"], l, min(_ROW_TILE_CONV, s))
        qt, k, vt, fr, fi = _qkvf_in(xb, w["w_qk"], w["w_in"], w["q_gain"], w["k_gain"], cos, sin, dft, l, tm)
        yr, yi = _four_a(fr.reshape(n2, n1, D_FOURIER), fi.reshape(n2, n1, D_FOURIER), wa, _DFT_ROWS)
        fmix = _four_c(yr, yi, mcs, min(_DFT_ROWS, n2)).reshape(s, D_FOURIER)
        score_bound = ((1.02 * HEAD_DIM * _Q_SCALE) * jnp.max(jnp.abs(w["q_gain"][l]))
                       * jnp.max(jnp.abs(w["k_gain"][l])))
        o = _attn(qt, k, vt, score_bound, min(_Q_TILE, s), min(_K_CHUNK_ONLINE, s // 2),
                  min(_K_CHUNK_BOUNDED, s // 2))
        xf, xb = _merge(xb, xf, a, o, fmix, w["w_in"], w["w_co"], w["w_ao"], w["w_fo"], w["w_o"],
                        w["ln1_g"], w["ln1_b"], l, tm, _COL_CHUNK)
        xf, xb = _ffn(xb, xf, w["w_up"], w["ffn_conv_w"], w["w_down"], w["ln2_g"], w["ln2_b"], l, tm,
                      _COL_CHUNK)
    return xf


def kernel(x_prompt, x_sample, w_in, conv_w, q_gain, k_gain, w_conv_out, w_attn_out, w_fourier_out,
           w_o, ln1_g, ln1_b, w_up, ffn_conv_w, w_down, ln2_g, ln2_b):
    layers = _prep_weights(w_in, conv_w, q_gain, k_gain, w_conv_out, w_attn_out, w_fourier_out, w_o,
                           ln1_g, ln1_b, w_up, ffn_conv_w, w_down, ln2_g, ln2_b)
    outs = []
    for x in (x_prompt, x_sample):
        b, s, d = x.shape
        rows = x.reshape(b * s, d)
        ys = [_trunk(rows[bi * s:(bi + 1) * s], layers) for bi in range(b)]
        outs.append((ys[0] if b == 1 else jnp.concatenate(ys, axis=0)).reshape(b, s, d))
    return tuple(outs)
```

```python
import functools
import math

import jax
import jax.numpy as jnp
from jax import lax
from jax.experimental import pallas as pl
from jax.experimental.pallas import tpu as pltpu

D_MODEL = 2048
DEPTH = 4
GRID_W = 64
D_CONV = 512
N_HEADS = 8
N_KV_HEADS = 2
HEAD_DIM = 128
GROUP = N_HEADS // N_KV_HEADS
D_ATTN = N_HEADS * HEAD_DIM
D_KV = N_KV_HEADS * HEAD_DIM
ROPE_THETA = 10000.0
N_FOURIER_GROUPS = 4
FOURIER_GROUP = 128
D_FOURIER = N_FOURIER_GROUPS * FOURIER_GROUP
D_FF = 5632
LN_EPS = 1e-5
QK_EPS = 1e-6
DEEPNORM_ALPHA = (2 * DEPTH) ** 0.25

_C_CONV = 0
_C_QKVF = 3 * D_CONV
_C_GATE = _C_QKVF + D_ATTN + 2 * D_KV + D_FOURIER

_HALO = 16
_ROW_TILE = 512
_ROW_TILE_QKVF = 256
_ROW_TILE_CONV = 1024
_COL_CHUNK = 512
_Q_TILE = 512
_K_CHUNK_ONLINE = 1024
_K_CHUNK_BOUNDED = 2048
_DFT_ROWS = 8
_DFT_N1 = 128
_VT_ROWS = HEAD_DIM + _HALO
_Q_SCALE = HEAD_DIM ** -0.5 * math.log2(math.e)
_SCORE_BOUND = 60.0
_VMEM_LIMIT = 58 * 1024 * 1024

_F32 = jnp.float32
_BF16 = jnp.bfloat16


def _dot(a, b):
    return jnp.dot(a, b, preferred_element_type=_F32)


def _layer_norm(y, g, b):
    mu = jnp.mean(y, axis=-1, keepdims=True)
    yc = y - mu
    var = jnp.mean(yc * yc, axis=-1, keepdims=True)
    return yc * lax.rsqrt(var + LN_EPS) * g + b


def _params(sem):
    return pltpu.CompilerParams(dimension_semantics=sem, vmem_limit_bytes=_VMEM_LIMIT)


def _fill_halo_lhs(xs_ref, xm_ref, xp_ref, xn_ref, first, last):
    tm = xm_ref.shape[0]
    half = _HALO // 2
    pv = jnp.where(first, 0.0, xp_ref[...].astype(_F32))
    nx = jnp.where(last, 0.0, xn_ref[...].astype(_F32))
    xs_ref[0:tm, :] = xm_ref[...]
    xs_ref[tm:tm + _HALO, :] = jnp.concatenate([nx[0:half], pv[half:_HALO]], axis=0).astype(_BF16)


def _dwconv3_rows(p_ext, cw, tm):
    n = p_ext.shape[0]
    prev = pltpu.roll(p_ext, 1, 0)[0:tm]
    nxt = pltpu.roll(p_ext, n - 1, 0)[0:tm]
    return prev * cw[0:1, :] + p_ext[0:tm] * cw[1:2, :] + nxt * cw[2:3, :]


def _halo_specs(tm, n_rows, width, ngrid):
    hb = tm // _HALO
    last_blk = n_rows // _HALO - 1
    if ngrid == 1:
        return [
            pl.BlockSpec((tm, width), lambda i: (i, 0)),
            pl.BlockSpec((_HALO, width), lambda i: (jnp.maximum(i * hb - 1, 0), 0)),
            pl.BlockSpec((_HALO, width), lambda i: (jnp.minimum((i + 1) * hb, last_blk), 0)),
        ]
    return [
        pl.BlockSpec((tm, width), lambda i, j: (i, 0)),
        pl.BlockSpec((_HALO, width), lambda i, j: (jnp.maximum(i * hb - 1, 0), 0)),
        pl.BlockSpec((_HALO, width), lambda i, j: (jnp.minimum((i + 1) * hb, last_blk), 0)),
    ]


def _conv_in_kernel(xm_ref, xp_ref, xn_ref, w_ref, cw_ref, a_ref, xs_ref):
    i = pl.program_id(0)
    tm = xm_ref.shape[0]
    _fill_halo_lhs(xs_ref, xm_ref, xp_ref, xn_ref, i == 0, i == pl.num_programs(0) - 1)
    u = _dot(xs_ref[...], w_ref[...])
    p = u[:, D_CONV:2 * D_CONV] * u[:, 2 * D_CONV:3 * D_CONV]
    conv = _dwconv3_rows(p, cw_ref[...], tm)
    a_ref[...] = (u[0:tm, 0:D_CONV] * conv).astype(_BF16)


def _conv_in(xb, w_in, conv_w, l, tm):
    s = xb.shape[0]
    return pl.pallas_call(
        _conv_in_kernel,
        grid=(s // tm,),
        in_specs=_halo_specs(tm, s, D_MODEL, 1) + [
            pl.BlockSpec((None, D_MODEL, 3 * D_CONV), lambda i: (l, 0, _C_CONV // (3 * D_CONV))),
            pl.BlockSpec((None, 3, D_CONV), lambda i: (l, 0, 0)),
        ],
        out_specs=pl.BlockSpec((tm, D_CONV), lambda i: (i, 0)),
        out_shape=jax.ShapeDtypeStruct((s, D_CONV), _BF16),
        scratch_shapes=[pltpu.VMEM((tm + _HALO, D_MODEL), _BF16)],
        compiler_params=_params(("arbitrary",)),
        name="conv_in",
    )(xb, xb, xb, w_in, conv_w)


def _qkvf_kernel(x_ref, wqk_ref, wv_ref, wf_ref, qg_ref, kg_ref, cos_ref, sin_ref, dft_ref,
                 qt_ref, k_ref, vt_ref, fr_ref, fi_ref):
    tm = x_ref.shape[0]
    x = x_ref[...]
    cos = cos_ref[...]
    sin = sin_ref[...]

    def norm_rope(xh, gain):
        ms = jnp.mean(xh * xh, axis=-1, keepdims=True)
        y = xh * lax.rsqrt(ms + QK_EPS) * gain
        return y * cos + pltpu.roll(y, HEAD_DIM // 2, 1) * sin

    u = _dot(x, wqk_ref[...])
    qg = qg_ref[...]
    kg = kg_ref[...]
    for h in range(N_HEADS):
        sl = slice(h * HEAD_DIM, (h + 1) * HEAD_DIM)
        qt_ref[sl, :] = (norm_rope(u[:, sl], qg) * _Q_SCALE).T.astype(_BF16)
    for h in range(N_KV_HEADS):
        sl = slice(h * HEAD_DIM, (h + 1) * HEAD_DIM)
        k_ref[:, sl] = norm_rope(u[:, D_ATTN + h * HEAD_DIM:D_ATTN + (h + 1) * HEAD_DIM], kg).astype(_BF16)
    v = _dot(x, wv_ref[...])
    for h in range(N_KV_HEADS):
        v0 = h * _VT_ROWS
        vt_ref[v0:v0 + HEAD_DIM, :] = v[:, h * HEAD_DIM:(h + 1) * HEAD_DIM].T.astype(_BF16)
        vt_ref[v0 + HEAD_DIM:v0 + _VT_ROWS, :] = jnp.ones((_HALO, tm), _BF16)
    f = _dot(x, wf_ref[...]).astype(_BF16)
    dft = dft_ref[...]
    for g in range(N_FOURIER_GROUPS):
        sl = slice(g * FOURIER_GROUP, (g + 1) * FOURIER_GROUP)
        z = _dot(f[:, sl], dft)
        fr_ref[:, sl] = z[:, 0:FOURIER_GROUP]
        fi_ref[:, sl] = z[:, FOURIER_GROUP:2 * FOURIER_GROUP]


def _qkvf_in(xb, w_qk, w_in, q_gain, k_gain, cos, sin, dft, l, tm):
    s = xb.shape[0]
    c_v = _C_QKVF + D_ATTN + D_KV
    c_f = c_v + D_KV
    row = lambda w: pl.BlockSpec((tm, w), lambda i: (i, 0))
    colt = lambda w: pl.BlockSpec((w, tm), lambda i: (0, i))
    layer = lambda a: pl.BlockSpec((None,) + a.shape[1:], lambda i: (l,) + (0,) * (a.ndim - 1))
    cols = lambda c0, w: pl.BlockSpec((None, D_MODEL, w), lambda i: (l, 0, c0 // w))
    full = lambda a: pl.BlockSpec(a.shape, lambda i: (0,) * a.ndim)
    assert c_v % D_KV == 0 and c_f % D_FOURIER == 0
    return pl.pallas_call(
        _qkvf_kernel,
        grid=(s // tm,),
        in_specs=[row(D_MODEL), layer(w_qk), cols(c_v, D_KV), cols(c_f, D_FOURIER),
                  layer(q_gain), layer(k_gain), row(HEAD_DIM), row(HEAD_DIM), full(dft)],
        out_specs=[colt(D_ATTN), row(D_KV), colt(N_KV_HEADS * _VT_ROWS), row(D_FOURIER), row(D_FOURIER)],
        out_shape=[jax.ShapeDtypeStruct((D_ATTN, s), _BF16),
                   jax.ShapeDtypeStruct((s, D_KV), _BF16),
                   jax.ShapeDtypeStruct((N_KV_HEADS * _VT_ROWS, s), _BF16),
                   jax.ShapeDtypeStruct((s, D_FOURIER), _F32),
                   jax.ShapeDtypeStruct((s, D_FOURIER), _F32)],
        compiler_params=_params(("arbitrary",)),
        name="qkvf_in",
    )(xb, w_qk, w_in, w_in, q_gain, k_gain, cos, sin, dft)


def _four_a_kernel(fr_ref, fi_ref, w_ref, yr_ref, yi_ref):
    n2, tb, _ = fr_ref.shape
    w = w_ref[...]
    for t in range(tb):
        f = jnp.concatenate([fr_ref[:, t, :], fi_ref[:, t, :]], axis=0).astype(_BF16)
        y = _dot(w, f)
        yr_ref[:, t, :] = y[0:n2]
        yi_ref[:, t, :] = y[n2:2 * n2]


def _four_a(fr3, fi3, wa, tb):
    n2, n1, _ = fr3.shape
    dat = pl.BlockSpec((n2, tb, D_FOURIER), lambda j: (0, j, 0))
    return pl.pallas_call(
        _four_a_kernel,
        grid=(n1 // tb,),
        in_specs=[dat, dat, pl.BlockSpec((2 * n2, 2 * n2), lambda j: (0, 0))],
        out_specs=[dat, dat],
        out_shape=[jax.ShapeDtypeStruct((n2, n1, D_FOURIER), _F32)] * 2,
        compiler_params=_params(("arbitrary",)),
        name="four_a",
    )(fr3, fi3, wa)


def _four_c_kernel(yr_ref, yi_ref, m_ref, o_ref):
    for b in range(yr_ref.shape[0]):
        y = jnp.concatenate([yr_ref[b], yi_ref[b]], axis=0).astype(_BF16)
        o_ref[:, b, :] = _dot(m_ref[b], y)


def _four_c(yr3, yi3, mcs, sb):
    n2, n1, _ = yr3.shape
    dat = pl.BlockSpec((sb, n1, D_FOURIER), lambda j: (j, 0, 0))
    return pl.pallas_call(
        _four_c_kernel,
        grid=(n2 // sb,),
        in_specs=[dat, dat, pl.BlockSpec((sb, n1, 2 * n1), lambda j: (j, 0, 0))],
        out_specs=pl.BlockSpec((n1, sb, D_FOURIER), lambda j: (0, j, 0)),
        out_shape=jax.ShapeDtypeStruct((n1, n2, D_FOURIER), _F32),
        compiler_params=_params(("arbitrary",)),
        name="four_c",
    )(yr3, yi3, mcs)


def _dft_tables(s):
    n1 = _DFT_N1
    n2 = s // n1
    two_pi = 2.0 * math.pi
    i2 = jnp.arange(n2, dtype=jnp.int32)
    ang2 = ((i2[:, None] * i2[None, :]) % n2).astype(_F32) * (two_pi / n2)
    c2 = jnp.cos(ang2) * (n2 ** -0.5)
    s2 = jnp.sin(ang2) * (n2 ** -0.5)
    wa = jnp.concatenate([jnp.concatenate([c2, s2], axis=1),
                          jnp.concatenate([-s2, c2], axis=1)], axis=0)
    i1 = jnp.arange(n1, dtype=jnp.int32)
    ang_a = ((i1[:, None] * i1[None, :]) % n1).astype(_F32) * (two_pi / n1)
    ang_b = (i2[:, None] * i1[None, :]).astype(_F32) * (two_pi / s)
    ca, sa = jnp.cos(ang_a)[None], jnp.sin(ang_a)[None]
    cb, sb = jnp.cos(ang_b)[:, None, :], jnp.sin(ang_b)[:, None, :]
    mcs = jnp.concatenate([ca * cb - sa * sb, sa * cb + ca * sb], axis=2) * (n1 ** -0.5)
    return wa.astype(_BF16), mcs.astype(_BF16)


def _channel_dft_matrix():
    i = jnp.arange(FOURIER_GROUP, dtype=jnp.int32)
    ang = ((i[:, None] * i[None, :]) % FOURIER_GROUP).astype(_F32) * (2.0 * math.pi / FOURIER_GROUP)
    scale = FOURIER_GROUP ** -0.5
    return jnp.concatenate([jnp.cos(ang) * scale, -jnp.sin(ang) * scale], axis=1).astype(_BF16)


def _rope_tables(s):
    rows = s // GRID_W
    half = HEAD_DIM // 2
    inv_freq = ROPE_THETA ** (-jnp.arange(0, half, 2, dtype=_F32) / half)
    ar = jnp.arange(rows, dtype=_F32)[:, None] * inv_freq
    ac = jnp.arange(GRID_W, dtype=_F32)[:, None] * inv_freq
    grid = (rows, GRID_W, HEAD_DIM // 4)
    by_row = lambda v: jnp.broadcast_to(v[:, None, :], grid)
    by_col = lambda v: jnp.broadcast_to(v[None, :, :], grid)
    cr, sr, cc, sc = by_row(jnp.cos(ar)), by_row(jnp.sin(ar)), by_col(jnp.cos(ac)), by_col(jnp.sin(ac))
    cos = jnp.concatenate([cr, cc, cr, cc], axis=2).reshape(s, HEAD_DIM)
    sin = jnp.concatenate([-sr, -sc, sr, sc], axis=2).reshape(s, HEAD_DIM)
    return cos, sin


def _pair_major(w):
    lead = w.shape[:-1]
    heads = w.shape[-1] // HEAD_DIM
    w = w.reshape(lead + (heads, 2, 2, HEAD_DIM // 4))
    return jnp.swapaxes(w, -3, -2).reshape(lead + (heads * HEAD_DIM,))


def _attn_kernel(qt_ref, k_ref, vt_ref, o_ref, s0_ref, s1_ref, p0_ref, p1_ref, acc_ref, *, tk):
    tq = qt_ref.shape[1]
    n_chunks = k_ref.shape[0] // tk

    def scores(j, s_ref):
        kc = k_ref[pl.ds(pl.multiple_of(j * tk, tk), tk), :]
        cmax = []
        for g in range(GROUP):
            st = _dot(kc, qt_ref[g * HEAD_DIM:(g + 1) * HEAD_DIM, :])
            s_ref[:, g * tq:(g + 1) * tq] = st
            cmax.append(jnp.max(st, axis=0, keepdims=True))
        return jnp.concatenate(cmax, axis=1)

    def softmax(s_ref, p_ref, m, cmax):
        m_new = jnp.maximum(m, cmax)
        alpha = jnp.exp2(m - m_new)
        p_ref[...] = jnp.exp2((s_ref[...] - m_new).astype(_BF16))
        return m_new, alpha

    def accumulate(j, p_ref, alpha):
        vc = vt_ref[:, pl.ds(pl.multiple_of(j * tk, tk), tk)]
        acc_ref[...] = alpha * acc_ref[...] + _dot(vc, p_ref[...])

    cmax0 = scores(0, s0_ref)
    p1_ref[...] = jnp.zeros_like(p1_ref)
    acc_ref[...] = jnp.zeros_like(acc_ref)

    def body(i, carry):
        m, alpha_prev, cmax_even = carry
        j = 2 * i
        cmax_odd = scores(j + 1, s1_ref)
        m, alpha_even = softmax(s0_ref, p0_ref, m, cmax_even)
        accumulate(jnp.maximum(j - 1, 0), p1_ref, alpha_prev)
        cmax_even = scores(jnp.minimum(j + 2, n_chunks - 1), s0_ref)
        m, alpha_odd = softmax(s1_ref, p1_ref, m, cmax_odd)
        accumulate(j, p0_ref, alpha_even)
        return m, alpha_odd, cmax_even

    m0 = jnp.full((1, GROUP * tq), -jnp.inf, _F32)
    a0 = jnp.ones((1, GROUP * tq), _F32)
    _, alpha_last, _ = lax.fori_loop(0, n_chunks // 2, body, (m0, a0, cmax0))
    accumulate(n_chunks - 1, p1_ref, alpha_last)
    out = acc_ref[0:HEAD_DIM, :] / acc_ref[HEAD_DIM:HEAD_DIM + 1, :]
    for g in range(GROUP):
        o_ref[:, g * HEAD_DIM:(g + 1) * HEAD_DIM] = out[:, g * tq:(g + 1) * tq].T.astype(_BF16)


def _attn_bounded_kernel(qt_ref, k_ref, vt_ref, o_ref, p0_ref, p1_ref, acc_ref, *, tk):
    tq = qt_ref.shape[1]
    mq = GROUP * tq
    n_chunks = k_ref.shape[0] // tk
    sub = 8

    def probs(j, p_ref, l8):
        kc = k_ref[pl.ds(pl.multiple_of(j * tk, tk), tk), :]
        parts = []
        for g in range(GROUP):
            p = jnp.exp2(_dot(kc, qt_ref[g * HEAD_DIM:(g + 1) * HEAD_DIM, :]))
            p_ref[:, g * tq:(g + 1) * tq] = p.astype(_BF16)
            parts.append(jnp.sum(p.reshape(tk // sub, sub, tq), axis=0))
        return l8 + jnp.concatenate(parts, axis=1)

    def accumulate(j, p_ref):
        vc = vt_ref[0:HEAD_DIM, pl.ds(pl.multiple_of(j * tk, tk), tk)]
        acc_ref[...] += _dot(vc, p_ref[...])

    acc_ref[...] = jnp.zeros_like(acc_ref)
    l8 = probs(0, p0_ref, jnp.zeros((sub, mq), _F32))

    def body(i, l8):
        j = 2 * i
        l8 = probs(j + 1, p1_ref, l8)
        accumulate(j, p0_ref)
        l8 = probs(j + 2, p0_ref, l8)
        accumulate(j + 1, p1_ref)
        return l8

    l8 = lax.fori_loop(0, n_chunks // 2 - 1, body, l8)
    l8 = probs(n_chunks - 1, p1_ref, l8)
    accumulate(n_chunks - 2, p0_ref)
    accumulate(n_chunks - 1, p1_ref)
    out = acc_ref[...] / jnp.sum(l8, axis=0, keepdims=True)
    for g in range(GROUP):
        o_ref[:, g * HEAD_DIM:(g + 1) * HEAD_DIM] = out[:, g * tq:(g + 1) * tq].T.astype(_BF16)


def _attn(qt, k, vt, score_bound, tq, tk, tk_bounded):
    s = k.shape[0]
    assert s % (2 * tk) == 0 and s % (2 * tk_bounded) == 0 and s % tq == 0
    mq = GROUP * tq
    bufs = lambda rows, dt: [pltpu.VMEM((rows, mq), dt), pltpu.VMEM((rows, mq), dt)]
    common = dict(
        grid=(N_KV_HEADS, s // tq),
        in_specs=[pl.BlockSpec((GROUP * HEAD_DIM, tq), lambda h, i: (h, i)),
                  pl.BlockSpec((s, HEAD_DIM), lambda h, i: (0, h)),
                  pl.BlockSpec((_VT_ROWS, s), lambda h, i: (h, 0))],
        out_specs=pl.BlockSpec((tq, GROUP * HEAD_DIM), lambda h, i: (i, h)),
        out_shape=jax.ShapeDtypeStruct((s, D_ATTN), _BF16),
        compiler_params=_params(("arbitrary", "arbitrary")),
    )
    bounded = pl.pallas_call(
        functools.partial(_attn_bounded_kernel, tk=tk_bounded),
        scratch_shapes=bufs(tk_bounded, _BF16) + [pltpu.VMEM((HEAD_DIM, mq), _F32)],
        name="attn_bounded", **common)
    online = pl.pallas_call(
        functools.partial(_attn_kernel, tk=tk),
        scratch_shapes=bufs(tk, _F32) + bufs(tk, _BF16) + [pltpu.VMEM((_VT_ROWS, mq), _F32)],
        name="attn", **common)
    return lax.cond(score_bound <= _SCORE_BOUND, bounded, online, qt, k, vt)


def _merge_kernel(xb_ref, xf_ref, a_ref, o_ref, fr_ref, wg0_ref, wg1_ref, wg2_ref,
                  wco_ref, wao_ref, wfo_ref, wo_ref, g_ref, b_ref, outf_ref, outb_ref):
    c = pl.program_id(1)

    @pl.when(c == 0)
    def _():
        outf_ref[...] = DEEPNORM_ALPHA * xf_ref[...]

    xb = xb_ref[...]
    m = jax.nn.sigmoid(_dot(xb, wg0_ref[...])) * _dot(a_ref[...], wco_ref[...])
    m = m + jax.nn.sigmoid(_dot(xb, wg1_ref[...])) * _dot(o_ref[...], wao_ref[...])
    m = m + jax.nn.sigmoid(_dot(xb, wg2_ref[...])) * _dot(fr_ref[...].astype(_BF16), wfo_ref[...])
    outf_ref[...] += _dot(m.astype(_BF16), wo_ref[...])

    @pl.when(c == pl.num_programs(1) - 1)
    def _():
        y = _layer_norm(outf_ref[...], g_ref[...], b_ref[...])
        outf_ref[...] = y
        outb_ref[...] = y.astype(_BF16)


def _merge(xb, xf, a, o, fr, w_in, w_co, w_ao, w_fo, w_o, ln_g, ln_b, l, tm, tn):
    s = xb.shape[0]
    nc = D_MODEL // tn
    gate0 = _C_GATE // tn
    row = lambda w: pl.BlockSpec((tm, w), lambda i, c: (i, 0))
    colw = lambda k: pl.BlockSpec((None, k, tn), lambda i, c: (l, 0, c))
    gate = lambda b: pl.BlockSpec((None, D_MODEL, tn), lambda i, c: (l, 0, gate0 + b * nc + c))
    vec = pl.BlockSpec((None, 1, D_MODEL), lambda i, c: (l, 0, 0))
    return pl.pallas_call(
        _merge_kernel,
        grid=(s // tm, nc),
        in_specs=[row(D_MODEL), row(D_MODEL), row(D_CONV), row(D_ATTN), row(D_FOURIER),
                  gate(0), gate(1), gate(2),
                  colw(D_CONV), colw(D_ATTN), colw(D_FOURIER),
                  pl.BlockSpec((None, tn, D_MODEL), lambda i, c: (l, c, 0)), vec, vec],
        out_specs=[row(D_MODEL), row(D_MODEL)],
        out_shape=[jax.ShapeDtypeStruct((s, D_MODEL), _F32), jax.ShapeDtypeStruct((s, D_MODEL), _BF16)],
        compiler_params=_params(("arbitrary", "arbitrary")),
        name="merge",
    )(xb, xf, a, o, fr, w_in, w_in, w_in, w_co, w_ao, w_fo, w_o, ln_g, ln_b)


def _ffn_kernel(xm_ref, xp_ref, xn_ref, xf_ref, wg_ref, wv_ref, cw_ref, wd_ref, g_ref, b_ref,
                outf_ref, outb_ref, xs_ref):
    i = pl.program_id(0)
    j = pl.program_id(1)
    tm = xm_ref.shape[0]

    @pl.when(j == 0)
    def _():
        _fill_halo_lhs(xs_ref, xm_ref, xp_ref, xn_ref, i == 0, i == pl.num_programs(0) - 1)
        outf_ref[...] = DEEPNORM_ALPHA * xf_ref[...]

    hg = _dot(xs_ref[...], wg_ref[...])
    hv = _dot(xs_ref[0:tm, :], wv_ref[...])
    h = jax.nn.silu(_dwconv3_rows(hg, cw_ref[...], tm)) * hv
    outf_ref[...] += _dot(h.astype(_BF16), wd_ref[...])

    @pl.when(j == pl.num_programs(1) - 1)
    def _():
        y = _layer_norm(outf_ref[...], g_ref[...], b_ref[...])
        outf_ref[...] = y
        outb_ref[...] = y.astype(_BF16)


def _ffn(xb, xf, w_up, ffn_conv_w, w_down, ln_g, ln_b, l, tm, tn):
    s = xb.shape[0]
    nj = D_FF // tn
    row = pl.BlockSpec((tm, D_MODEL), lambda i, j: (i, 0))
    vec = pl.BlockSpec((None, 1, D_MODEL), lambda i, j: (l, 0, 0))
    return pl.pallas_call(
        _ffn_kernel,
        grid=(s // tm, nj),
        in_specs=_halo_specs(tm, s, D_MODEL, 2) + [
            row,
            pl.BlockSpec((None, D_MODEL, tn), lambda i, j: (l, 0, j)),
            pl.BlockSpec((None, D_MODEL, tn), lambda i, j: (l, 0, nj + j)),
            pl.BlockSpec((None, 3, tn), lambda i, j: (l, 0, j)),
            pl.BlockSpec((None, tn, D_MODEL), lambda i, j: (l, j, 0)),
            vec, vec],
        out_specs=[row, row],
        out_shape=[jax.ShapeDtypeStruct((s, D_MODEL), _F32), jax.ShapeDtypeStruct((s, D_MODEL), _BF16)],
        scratch_shapes=[pltpu.VMEM((tm + _HALO, D_MODEL), _BF16)],
        compiler_params=_params(("arbitrary", "arbitrary")),
        name="ffn",
    )(xb, xb, xb, xf, w_up, w_up, ffn_conv_w, w_down, ln_g, ln_b)


def _prep_weights(w_in, conv_w, q_gain, k_gain, w_conv_out, w_attn_out, w_fourier_out, w_o,
                  ln1_g, ln1_b, w_up, ffn_conv_w, w_down, ln2_g, ln2_b):
    depth = w_in.shape[0]
    vec = lambda v: v.reshape(depth, 1, v.shape[-1])
    qk0, qk1 = _C_QKVF, _C_QKVF + D_ATTN + D_KV
    return dict(
        depth=depth,
        w_in=w_in.astype(_BF16),
        w_qk=_pair_major(w_in[:, :, qk0:qk1]).astype(_BF16),
        conv_w=conv_w,
        q_gain=vec(_pair_major(q_gain)), k_gain=vec(_pair_major(k_gain)),
        w_co=w_conv_out.astype(_BF16), w_ao=w_attn_out.astype(_BF16), w_fo=w_fourier_out.astype(_BF16),
        w_o=w_o.astype(_BF16),
        ln1_g=vec(ln1_g), ln1_b=vec(ln1_b),
        w_up=w_up.astype(_BF16), ffn_conv_w=ffn_conv_w, w_down=w_down.astype(_BF16),
        ln2_g=vec(ln2_g), ln2_b=vec(ln2_b),
    )


def _trunk(x, w):
    s = x.shape[0]
    tm = min(_ROW_TILE, s)
    n1 = _DFT_N1
    n2 = s // n1
    cos, sin = _rope_tables(s)
    wa, mcs = _dft_tables(s)
    dft = _channel_dft_matrix()
    xf = x
    xb = x.astype(_BF16)
    for l in range(w["depth"]):
        a = _conv_in(xb, w["w_in"], w["conv_w"], l, min(_ROW_TILE_CONV, s))
        qt, k, vt, fr, fi = _qkvf_in(xb, w["w_qk"], w["w_in"], w["q_gain"], w["k_gain"], cos, sin, dft, l,
                                     min(_ROW_TILE_QKVF, s))
        yr, yi = _four_a(fr.reshape(n2, n1, D_FOURIER), fi.reshape(n2, n1, D_FOURIER), wa, _DFT_ROWS)
        fmix = _four_c(yr, yi, mcs, min(_DFT_ROWS, n2)).reshape(s, D_FOURIER)
        score_bound = ((1.02 * HEAD_DIM * _Q_SCALE) * jnp.max(jnp.abs(w["q_gain"][l]))
                       * jnp.max(jnp.abs(w["k_gain"][l])))
        o = _attn(qt, k, vt, score_bound, min(_Q_TILE, s), min(_K_CHUNK_ONLINE, s // 2),
                  min(_K_CHUNK_BOUNDED, s // 2))
        xf, xb = _merge(xb, xf, a, o, fmix, w["w_in"], w["w_co"], w["w_ao"], w["w_fo"], w["w_o"],
                        w["ln1_g"], w["ln1_b"], l, tm, _COL_CHUNK)
        xf, xb = _ffn(xb, xf, w["w_up"], w["ffn_conv_w"], w["w_down"], w["ln2_g"], w["ln2_b"], l, tm,
                      _COL_CHUNK)
    return xf


def kernel(x_prompt, x_sample, w_in, conv_w, q_gain, k_gain, w_conv_out, w_attn_out, w_fourier_out,
           w_o, ln1_g, ln1_b, w_up, ffn_conv_w, w_down, ln2_g, ln2_b):
    layers = _prep_weights(w_in, conv_w, q_gain, k_gain, w_conv_out, w_attn_out, w_fourier_out, w_o,
                           ln1_g, ln1_b, w_up, ffn_conv_w, w_down, ln2_g, ln2_b)
    outs = []
    for x in (x_prompt, x_sample):
        b, s, d = x.shape
        rows = x.reshape(b * s, d)
        ys = [_trunk(rows[bi * s:(bi + 1) * s], layers) for bi in range(b)]
        outs.append((ys[0] if b == 1 else jnp.concatenate(ys, axis=0)).reshape(b, s, d))
    return tuple(outs)
```

```python
import functools
import math

import jax
import jax.numpy as jnp
from jax import lax
from jax.experimental import pallas as pl
from jax.experimental.pallas import tpu as pltpu

D_MODEL = 2048
DEPTH = 4
GRID_W = 64
D_CONV = 512
N_HEADS = 8
N_KV_HEADS = 2
HEAD_DIM = 128
GROUP = N_HEADS // N_KV_HEADS
D_ATTN = N_HEADS * HEAD_DIM
D_KV = N_KV_HEADS * HEAD_DIM
ROPE_THETA = 10000.0
N_FOURIER_GROUPS = 4
FOURIER_GROUP = 128
D_FOURIER = N_FOURIER_GROUPS * FOURIER_GROUP
D_FF = 5632
LN_EPS = 1e-5
QK_EPS = 1e-6
DEEPNORM_ALPHA = (2 * DEPTH) ** 0.25

_C_CONV = 0
_C_QKVF = 3 * D_CONV
_C_GATE = _C_QKVF + D_ATTN + 2 * D_KV + D_FOURIER

_HALO = 16
_ROW_TILE = 512
_ROW_TILE_CONV = 1024
_COL_CHUNK = 512
_Q_TILE = 512
_K_CHUNK_ONLINE = 1024
_K_CHUNK_BOUNDED = 2048
_DFT_ROWS = 8
_DFT_N1 = 128
_VT_ROWS = HEAD_DIM + _HALO
_Q_SCALE = HEAD_DIM ** -0.5 * math.log2(math.e)
_SCORE_BOUND = 60.0
_VMEM_LIMIT = 58 * 1024 * 1024

_F32 = jnp.float32
_BF16 = jnp.bfloat16


def _dot(a, b):
    return jnp.dot(a, b, preferred_element_type=_F32)


def _layer_norm(y, g, b):
    mu = jnp.mean(y, axis=-1, keepdims=True)
    yc = y - mu
    var = jnp.mean(yc * yc, axis=-1, keepdims=True)
    return yc * lax.rsqrt(var + LN_EPS) * g + b


def _params(sem):
    return pltpu.CompilerParams(dimension_semantics=sem, vmem_limit_bytes=_VMEM_LIMIT)


def _fill_halo_lhs(xs_ref, xm_ref, xp_ref, xn_ref, first, last):
    tm = xm_ref.shape[0]
    half = _HALO // 2
    pv = jnp.where(first, 0.0, xp_ref[...].astype(_F32))
    nx = jnp.where(last, 0.0, xn_ref[...].astype(_F32))
    xs_ref[0:tm, :] = xm_ref[...]
    xs_ref[tm:tm + _HALO, :] = jnp.concatenate([nx[0:half], pv[half:_HALO]], axis=0).astype(_BF16)


def _dwconv3_rows(p_ext, cw, tm):
    n = p_ext.shape[0]
    prev = pltpu.roll(p_ext, 1, 0)[0:tm]
    nxt = pltpu.roll(p_ext, n - 1, 0)[0:tm]
    return prev * cw[0:1, :] + p_ext[0:tm] * cw[1:2, :] + nxt * cw[2:3, :]


def _halo_specs(tm, n_rows, width, ngrid):
    hb = tm // _HALO
    last_blk = n_rows // _HALO - 1
    if ngrid == 1:
        return [
            pl.BlockSpec((tm, width), lambda i: (i, 0)),
            pl.BlockSpec((_HALO, width), lambda i: (jnp.maximum(i * hb - 1, 0), 0)),
            pl.BlockSpec((_HALO, width), lambda i: (jnp.minimum((i + 1) * hb, last_blk), 0)),
        ]
    return [
        pl.BlockSpec((tm, width), lambda i, j: (i, 0)),
        pl.BlockSpec((_HALO, width), lambda i, j: (jnp.maximum(i * hb - 1, 0), 0)),
        pl.BlockSpec((_HALO, width), lambda i, j: (jnp.minimum((i + 1) * hb, last_blk), 0)),
    ]


def _conv_in_kernel(xm_ref, xp_ref, xn_ref, w_ref, cw_ref, a_ref, xs_ref):
    i = pl.program_id(0)
    tm = xm_ref.shape[0]
    _fill_halo_lhs(xs_ref, xm_ref, xp_ref, xn_ref, i == 0, i == pl.num_programs(0) - 1)
    u = _dot(xs_ref[...], w_ref[...])
    p = u[:, D_CONV:2 * D_CONV] * u[:, 2 * D_CONV:3 * D_CONV]
    conv = _dwconv3_rows(p, cw_ref[...], tm)
    a_ref[...] = (u[0:tm, 0:D_CONV] * conv).astype(_BF16)


def _conv_in(xb, w_in, conv_w, l, tm):
    s = xb.shape[0]
    return pl.pallas_call(
        _conv_in_kernel,
        grid=(s // tm,),
        in_specs=_halo_specs(tm, s, D_MODEL, 1) + [
            pl.BlockSpec((None, D_MODEL, 3 * D_CONV), lambda i: (l, 0, _C_CONV // (3 * D_CONV))),
            pl.BlockSpec((None, 3, D_CONV), lambda i: (l, 0, 0)),
        ],
        out_specs=pl.BlockSpec((tm, D_CONV), lambda i: (i, 0)),
        out_shape=jax.ShapeDtypeStruct((s, D_CONV), _BF16),
        scratch_shapes=[pltpu.VMEM((tm + _HALO, D_MODEL), _BF16)],
        compiler_params=_params(("arbitrary",)),
        name="conv_in",
    )(xb, xb, xb, w_in, conv_w)


def _qkvf_kernel(x_ref, wqk_ref, wv_ref, wf_ref, qg_ref, kg_ref, cos_ref, sin_ref, dft_ref,
                 qt_ref, k_ref, vt_ref, fr_ref, fi_ref):
    tm = x_ref.shape[0]
    x = x_ref[...]
    cos = cos_ref[...]
    sin = sin_ref[...]

    def norm_rope(xh, gain):
        ms = jnp.mean(xh * xh, axis=-1, keepdims=True)
        y = xh * lax.rsqrt(ms + QK_EPS) * gain
        return y * cos + pltpu.roll(y, HEAD_DIM // 2, 1) * sin

    u = _dot(x, wqk_ref[...])
    qg = qg_ref[...]
    kg = kg_ref[...]
    for h in range(N_HEADS):
        sl = slice(h * HEAD_DIM, (h + 1) * HEAD_DIM)
        qt_ref[sl, :] = (norm_rope(u[:, sl], qg) * _Q_SCALE).T.astype(_BF16)
    for h in range(N_KV_HEADS):
        sl = slice(h * HEAD_DIM, (h + 1) * HEAD_DIM)
        k_ref[:, sl] = norm_rope(u[:, D_ATTN + h * HEAD_DIM:D_ATTN + (h + 1) * HEAD_DIM], kg).astype(_BF16)
    v = _dot(x, wv_ref[...])
    for h in range(N_KV_HEADS):
        v0 = h * _VT_ROWS
        vt_ref[v0:v0 + HEAD_DIM, :] = v[:, h * HEAD_DIM:(h + 1) * HEAD_DIM].T.astype(_BF16)
        vt_ref[v0 + HEAD_DIM:v0 + _VT_ROWS, :] = jnp.ones((_HALO, tm), _BF16)
    f = _dot(x, wf_ref[...]).astype(_BF16)
    dft = dft_ref[...]
    for g in range(N_FOURIER_GROUPS):
        sl = slice(g * FOURIER_GROUP, (g + 1) * FOURIER_GROUP)
        z = _dot(f[:, sl], dft)
        fr_ref[:, sl] = z[:, 0:FOURIER_GROUP]
        fi_ref[:, sl] = z[:, FOURIER_GROUP:2 * FOURIER_GROUP]


def _qkvf_in(xb, w_qk, w_in, q_gain, k_gain, cos, sin, dft, l, tm):
    s = xb.shape[0]
    c_v = _C_QKVF + D_ATTN + D_KV
    c_f = c_v + D_KV
    row = lambda w: pl.BlockSpec((tm, w), lambda i: (i, 0))
    colt = lambda w: pl.BlockSpec((w, tm), lambda i: (0, i))
    layer = lambda a: pl.BlockSpec((None,) + a.shape[1:], lambda i: (l,) + (0,) * (a.ndim - 1))
    cols = lambda c0, w: pl.BlockSpec((None, D_MODEL, w), lambda i: (l, 0, c0 // w))
    full = lambda a: pl.BlockSpec(a.shape, lambda i: (0,) * a.ndim)
    assert c_v % D_KV == 0 and c_f % D_FOURIER == 0
    return pl.pallas_call(
        _qkvf_kernel,
        grid=(s // tm,),
        in_specs=[row(D_MODEL), layer(w_qk), cols(c_v, D_KV), cols(c_f, D_FOURIER),
                  layer(q_gain), layer(k_gain), row(HEAD_DIM), row(HEAD_DIM), full(dft)],
        out_specs=[colt(D_ATTN), row(D_KV), colt(N_KV_HEADS * _VT_ROWS), row(D_FOURIER), row(D_FOURIER)],
        out_shape=[jax.ShapeDtypeStruct((D_ATTN, s), _BF16),
                   jax.ShapeDtypeStruct((s, D_KV), _BF16),
                   jax.ShapeDtypeStruct((N_KV_HEADS * _VT_ROWS, s), _BF16),
                   jax.ShapeDtypeStruct((s, D_FOURIER), _F32),
                   jax.ShapeDtypeStruct((s, D_FOURIER), _F32)],
        compiler_params=_params(("arbitrary",)),
        name="qkvf_in",
    )(xb, w_qk, w_in, w_in, q_gain, k_gain, cos, sin, dft)


def _in_proj_kernel(xm_ref, xp_ref, xn_ref, wc_ref, cw_ref, wqk_ref, wv_ref, wf_ref, qg_ref, kg_ref,
                    cos_ref, sin_ref, dft_ref, a_ref, qt_ref, k_ref, vt_ref, fr_ref, fi_ref, xs_ref):
    i = pl.program_id(0)
    tm = xm_ref.shape[0]
    _fill_halo_lhs(xs_ref, xm_ref, xp_ref, xn_ref, i == 0, i == pl.num_programs(0) - 1)
    uc = _dot(xs_ref[...], wc_ref[...])
    p = uc[:, D_CONV:2 * D_CONV] * uc[:, 2 * D_CONV:3 * D_CONV]
    a_ref[...] = (uc[0:tm, 0:D_CONV] * _dwconv3_rows(p, cw_ref[...], tm)).astype(_BF16)
    x = xs_ref[0:tm, :]
    cos = cos_ref[...]
    sin = sin_ref[...]

    def norm_rope(xh, gain):
        ms = jnp.mean(xh * xh, axis=-1, keepdims=True)
        y = xh * lax.rsqrt(ms + QK_EPS) * gain
        return y * cos + pltpu.roll(y, HEAD_DIM // 2, 1) * sin

    u = _dot(x, wqk_ref[...])
    qg = qg_ref[...]
    kg = kg_ref[...]
    for h in range(N_HEADS):
        sl = slice(h * HEAD_DIM, (h + 1) * HEAD_DIM)
        qt_ref[sl, :] = (norm_rope(u[:, sl], qg) * _Q_SCALE).T.astype(_BF16)
    for h in range(N_KV_HEADS):
        sl = slice(h * HEAD_DIM, (h + 1) * HEAD_DIM)
        k_ref[:, sl] = norm_rope(u[:, D_ATTN + h * HEAD_DIM:D_ATTN + (h + 1) * HEAD_DIM], kg).astype(_BF16)
    v = _dot(x, wv_ref[...])
    for h in range(N_KV_HEADS):
        v0 = h * _VT_ROWS
        vt_ref[v0:v0 + HEAD_DIM, :] = v[:, h * HEAD_DIM:(h + 1) * HEAD_DIM].T.astype(_BF16)
        vt_ref[v0 + HEAD_DIM:v0 + _VT_ROWS, :] = jnp.ones((_HALO, tm), _BF16)
    f = _dot(x, wf_ref[...]).astype(_BF16)
    dft = dft_ref[...]
    for g in range(N_FOURIER_GROUPS):
        sl = slice(g * FOURIER_GROUP, (g + 1) * FOURIER_GROUP)
        z = _dot(f[:, sl], dft)
        fr_ref[:, sl] = z[:, 0:FOURIER_GROUP]
        fi_ref[:, sl] = z[:, FOURIER_GROUP:2 * FOURIER_GROUP]


def _in_proj(xb, w_in, conv_w, w_qk, q_gain, k_gain, cos, sin, dft, l, tm):
    s = xb.shape[0]
    c_v = _C_QKVF + D_ATTN + D_KV
    c_f = c_v + D_KV
    row = lambda w: pl.BlockSpec((tm, w), lambda i: (i, 0))
    colt = lambda w: pl.BlockSpec((w, tm), lambda i: (0, i))
    layer = lambda a: pl.BlockSpec((None,) + a.shape[1:], lambda i: (l,) + (0,) * (a.ndim - 1))
    cols = lambda c0, w: pl.BlockSpec((None, D_MODEL, w), lambda i: (l, 0, c0 // w))
    full = lambda a: pl.BlockSpec(a.shape, lambda i: (0,) * a.ndim)
    return pl.pallas_call(
        _in_proj_kernel,
        grid=(s // tm,),
        in_specs=_halo_specs(tm, s, D_MODEL, 1) + [
            cols(_C_CONV, 3 * D_CONV), layer(conv_w), layer(w_qk), cols(c_v, D_KV), cols(c_f, D_FOURIER),
            layer(q_gain), layer(k_gain), row(HEAD_DIM), row(HEAD_DIM), full(dft)],
        out_specs=[row(D_CONV), colt(D_ATTN), row(D_KV), colt(N_KV_HEADS * _VT_ROWS), row(D_FOURIER),
                   row(D_FOURIER)],
        out_shape=[jax.ShapeDtypeStruct((s, D_CONV), _BF16),
                   jax.ShapeDtypeStruct((D_ATTN, s), _BF16),
                   jax.ShapeDtypeStruct((s, D_KV), _BF16),
                   jax.ShapeDtypeStruct((N_KV_HEADS * _VT_ROWS, s), _BF16),
                   jax.ShapeDtypeStruct((s, D_FOURIER), _F32),
                   jax.ShapeDtypeStruct((s, D_FOURIER), _F32)],
        scratch_shapes=[pltpu.VMEM((tm + _HALO, D_MODEL), _BF16)],
        compiler_params=_params(("arbitrary",)),
        name="in_proj",
    )(xb, xb, xb, w_in, conv_w, w_qk, w_in, w_in, q_gain, k_gain, cos, sin, dft)


def _four_a_kernel(fr_ref, fi_ref, w_ref, yr_ref, yi_ref):
    n2, tb, _ = fr_ref.shape
    w = w_ref[...]
    for t in range(tb):
        f = jnp.concatenate([fr_ref[:, t, :], fi_ref[:, t, :]], axis=0).astype(_BF16)
        y = _dot(w, f)
        yr_ref[:, t, :] = y[0:n2]
        yi_ref[:, t, :] = y[n2:2 * n2]


def _four_a(fr3, fi3, wa, tb):
    n2, n1, _ = fr3.shape
    dat = pl.BlockSpec((n2, tb, D_FOURIER), lambda j: (0, j, 0))
    return pl.pallas_call(
        _four_a_kernel,
        grid=(n1 // tb,),
        in_specs=[dat, dat, pl.BlockSpec((2 * n2, 2 * n2), lambda j: (0, 0))],
        out_specs=[dat, dat],
        out_shape=[jax.ShapeDtypeStruct((n2, n1, D_FOURIER), _F32)] * 2,
        compiler_params=_params(("arbitrary",)),
        name="four_a",
    )(fr3, fi3, wa)


def _four_c_kernel(yr_ref, yi_ref, m_ref, o_ref):
    for b in range(yr_ref.shape[0]):
        y = jnp.concatenate([yr_ref[b], yi_ref[b]], axis=0).astype(_BF16)
        o_ref[:, b, :] = _dot(m_ref[b], y)


def _four_c(yr3, yi3, mcs, sb):
    n2, n1, _ = yr3.shape
    dat = pl.BlockSpec((sb, n1, D_FOURIER), lambda j: (j, 0, 0))
    return pl.pallas_call(
        _four_c_kernel,
        grid=(n2 // sb,),
        in_specs=[dat, dat, pl.BlockSpec((sb, n1, 2 * n1), lambda j: (j, 0, 0))],
        out_specs=pl.BlockSpec((n1, sb, D_FOURIER), lambda j: (0, j, 0)),
        out_shape=jax.ShapeDtypeStruct((n1, n2, D_FOURIER), _F32),
        compiler_params=_params(("arbitrary",)),
        name="four_c",
    )(yr3, yi3, mcs)


def _dft_tables(s):
    n1 = _DFT_N1
    n2 = s // n1
    two_pi = 2.0 * math.pi
    i2 = jnp.arange(n2, dtype=jnp.int32)
    ang2 = ((i2[:, None] * i2[None, :]) % n2).astype(_F32) * (two_pi / n2)
    c2 = jnp.cos(ang2) * (n2 ** -0.5)
    s2 = jnp.sin(ang2) * (n2 ** -0.5)
    wa = jnp.concatenate([jnp.concatenate([c2, s2], axis=1),
                          jnp.concatenate([-s2, c2], axis=1)], axis=0)
    i1 = jnp.arange(n1, dtype=jnp.int32)
    ang_a = ((i1[:, None] * i1[None, :]) % n1).astype(_F32) * (two_pi / n1)
    ang_b = (i2[:, None] * i1[None, :]).astype(_F32) * (two_pi / s)
    ca, sa = jnp.cos(ang_a)[None], jnp.sin(ang_a)[None]
    cb, sb = jnp.cos(ang_b)[:, None, :], jnp.sin(ang_b)[:, None, :]
    mcs = jnp.concatenate([ca * cb - sa * sb, sa * cb + ca * sb], axis=2) * (n1 ** -0.5)
    return wa.astype(_BF16), mcs.astype(_BF16)


def _channel_dft_matrix():
    i = jnp.arange(FOURIER_GROUP, dtype=jnp.int32)
    ang = ((i[:, None] * i[None, :]) % FOURIER_GROUP).astype(_F32) * (2.0 * math.pi / FOURIER_GROUP)
    scale = FOURIER_GROUP ** -0.5
    return jnp.concatenate([jnp.cos(ang) * scale, -jnp.sin(ang) * scale], axis=1).astype(_BF16)


def _rope_tables(s):
    rows = s // GRID_W
    half = HEAD_DIM // 2
    inv_freq = ROPE_THETA ** (-jnp.arange(0, half, 2, dtype=_F32) / half)
    ar = jnp.arange(rows, dtype=_F32)[:, None] * inv_freq
    ac = jnp.arange(GRID_W, dtype=_F32)[:, None] * inv_freq
    grid = (rows, GRID_W, HEAD_DIM // 4)
    by_row = lambda v: jnp.broadcast_to(v[:, None, :], grid)
    by_col = lambda v: jnp.broadcast_to(v[None, :, :], grid)
    cr, sr, cc, sc = by_row(jnp.cos(ar)), by_row(jnp.sin(ar)), by_col(jnp.cos(ac)), by_col(jnp.sin(ac))
    cos = jnp.concatenate([cr, cc, cr, cc], axis=2).reshape(s, HEAD_DIM)
    sin = jnp.concatenate([-sr, -sc, sr, sc], axis=2).reshape(s, HEAD_DIM)
    return cos, sin


def _pair_major(w):
    lead = w.shape[:-1]
    heads = w.shape[-1] // HEAD_DIM
    w = w.reshape(lead + (heads, 2, 2, HEAD_DIM // 4))
    return jnp.swapaxes(w, -3, -2).reshape(lead + (heads * HEAD_DIM,))


def _attn_kernel(qt_ref, k_ref, vt_ref, o_ref, s0_ref, s1_ref, p0_ref, p1_ref, acc_ref, *, tk):
    tq = qt_ref.shape[1]
    n_chunks = k_ref.shape[0] // tk

    def scores(j, s_ref):
        kc = k_ref[pl.ds(pl.multiple_of(j * tk, tk), tk), :]
        cmax = []
        for g in range(GROUP):
            st = _dot(kc, qt_ref[g * HEAD_DIM:(g + 1) * HEAD_DIM, :])
            s_ref[:, g * tq:(g + 1) * tq] = st
            cmax.append(jnp.max(st, axis=0, keepdims=True))
        return jnp.concatenate(cmax, axis=1)

    def softmax(s_ref, p_ref, m, cmax):
        m_new = jnp.maximum(m, cmax)
        alpha = jnp.exp2(m - m_new)
        p_ref[...] = jnp.exp2((s_ref[...] - m_new).astype(_BF16))
        return m_new, alpha

    def accumulate(j, p_ref, alpha):
        vc = vt_ref[:, pl.ds(pl.multiple_of(j * tk, tk), tk)]
        acc_ref[...] = alpha * acc_ref[...] + _dot(vc, p_ref[...])

    cmax0 = scores(0, s0_ref)
    p1_ref[...] = jnp.zeros_like(p1_ref)
    acc_ref[...] = jnp.zeros_like(acc_ref)

    def body(i, carry):
        m, alpha_prev, cmax_even = carry
        j = 2 * i
        cmax_odd = scores(j + 1, s1_ref)
        m, alpha_even = softmax(s0_ref, p0_ref, m, cmax_even)
        accumulate(jnp.maximum(j - 1, 0), p1_ref, alpha_prev)
        cmax_even = scores(jnp.minimum(j + 2, n_chunks - 1), s0_ref)
        m, alpha_odd = softmax(s1_ref, p1_ref, m, cmax_odd)
        accumulate(j, p0_ref, alpha_even)
        return m, alpha_odd, cmax_even

    m0 = jnp.full((1, GROUP * tq), -jnp.inf, _F32)
    a0 = jnp.ones((1, GROUP * tq), _F32)
    _, alpha_last, _ = lax.fori_loop(0, n_chunks // 2, body, (m0, a0, cmax0))
    accumulate(n_chunks - 1, p1_ref, alpha_last)
    out = acc_ref[0:HEAD_DIM, :] / acc_ref[HEAD_DIM:HEAD_DIM + 1, :]
    for g in range(GROUP):
        o_ref[:, g * HEAD_DIM:(g + 1) * HEAD_DIM] = out[:, g * tq:(g + 1) * tq].T.astype(_BF16)


def _attn_bounded_kernel(qt_ref, k_ref, vt_ref, o_ref, p0_ref, p1_ref, acc_ref, *, tk):
    tq = qt_ref.shape[1]
    mq = GROUP * tq
    n_chunks = k_ref.shape[0] // tk
    sub = 8

    def probs(j, p_ref, l8):
        kc = k_ref[pl.ds(pl.multiple_of(j * tk, tk), tk), :]
        parts = []
        for g in range(GROUP):
            p = jnp.exp2(_dot(kc, qt_ref[g * HEAD_DIM:(g + 1) * HEAD_DIM, :]))
            p_ref[:, g * tq:(g + 1) * tq] = p.astype(_BF16)
            parts.append(jnp.sum(p.reshape(tk // sub, sub, tq), axis=0))
        return l8 + jnp.concatenate(parts, axis=1)

    def accumulate(j, p_ref):
        vc = vt_ref[0:HEAD_DIM, pl.ds(pl.multiple_of(j * tk, tk), tk)]
        acc_ref[...] += _dot(vc, p_ref[...])

    acc_ref[...] = jnp.zeros_like(acc_ref)
    l8 = probs(0, p0_ref, jnp.zeros((sub, mq), _F32))

    def body(i, l8):
        j = 2 * i
        l8 = probs(j + 1, p1_ref, l8)
        accumulate(j, p0_ref)
        l8 = probs(j + 2, p0_ref, l8)
        accumulate(j + 1, p1_ref)
        return l8

    l8 = lax.fori_loop(0, n_chunks // 2 - 1, body, l8)
    l8 = probs(n_chunks - 1, p1_ref, l8)
    accumulate(n_chunks - 2, p0_ref)
    accumulate(n_chunks - 1, p1_ref)
    out = acc_ref[...] / jnp.sum(l8, axis=0, keepdims=True)
    for g in range(GROUP):
        o_ref[:, g * HEAD_DIM:(g + 1) * HEAD_DIM] = out[:, g * tq:(g + 1) * tq].T.astype(_BF16)


def _attn(qt, k, vt, score_bound, tq, tk, tk_bounded):
    s = k.shape[0]
    assert s % (2 * tk) == 0 and s % (2 * tk_bounded) == 0 and s % tq == 0
    mq = GROUP * tq
    bufs = lambda rows, dt: [pltpu.VMEM((rows, mq), dt), pltpu.VMEM((rows, mq), dt)]
    common = dict(
        grid=(N_KV_HEADS, s // tq),
        in_specs=[pl.BlockSpec((GROUP * HEAD_DIM, tq), lambda h, i: (h, i)),
                  pl.BlockSpec((s, HEAD_DIM), lambda h, i: (0, h)),
                  pl.BlockSpec((_VT_ROWS, s), lambda h, i: (h, 0))],
        out_specs=pl.BlockSpec((tq, GROUP * HEAD_DIM), lambda h, i: (i, h)),
        out_shape=jax.ShapeDtypeStruct((s, D_ATTN), _BF16),
        compiler_params=_params(("arbitrary", "arbitrary")),
    )
    bounded = pl.pallas_call(
        functools.partial(_attn_bounded_kernel, tk=tk_bounded),
        scratch_shapes=bufs(tk_bounded, _BF16) + [pltpu.VMEM((HEAD_DIM, mq), _F32)],
        name="attn_bounded", **common)
    online = pl.pallas_call(
        functools.partial(_attn_kernel, tk=tk),
        scratch_shapes=bufs(tk, _F32) + bufs(tk, _BF16) + [pltpu.VMEM((_VT_ROWS, mq), _F32)],
        name="attn", **common)
    return lax.cond(score_bound <= _SCORE_BOUND, bounded, online, qt, k, vt)


def _merge_kernel(xb_ref, xf_ref, a_ref, o_ref, fr_ref, wg0_ref, wg1_ref, wg2_ref,
                  wco_ref, wao_ref, wfo_ref, wo_ref, g_ref, b_ref, outf_ref, outb_ref):
    c = pl.program_id(1)

    @pl.when(c == 0)
    def _():
        outf_ref[...] = DEEPNORM_ALPHA * xf_ref[...]

    xb = xb_ref[...]
    m = jax.nn.sigmoid(_dot(xb, wg0_ref[...])) * _dot(a_ref[...], wco_ref[...])
    m = m + jax.nn.sigmoid(_dot(xb, wg1_ref[...])) * _dot(o_ref[...], wao_ref[...])
    m = m + jax.nn.sigmoid(_dot(xb, wg2_ref[...])) * _dot(fr_ref[...].astype(_BF16), wfo_ref[...])
    outf_ref[...] += _dot(m.astype(_BF16), wo_ref[...])

    @pl.when(c == pl.num_programs(1) - 1)
    def _():
        y = _layer_norm(outf_ref[...], g_ref[...], b_ref[...])
        outf_ref[...] = y
        outb_ref[...] = y.astype(_BF16)


def _merge(xb, xf, a, o, fr, w_in, w_co, w_ao, w_fo, w_o, ln_g, ln_b, l, tm, tn):
    s = xb.shape[0]
    nc = D_MODEL // tn
    gate0 = _C_GATE // tn
    row = lambda w: pl.BlockSpec((tm, w), lambda i, c: (i, 0))
    colw = lambda k: pl.BlockSpec((None, k, tn), lambda i, c: (l, 0, c))
    gate = lambda b: pl.BlockSpec((None, D_MODEL, tn), lambda i, c: (l, 0, gate0 + b * nc + c))
    vec = pl.BlockSpec((None, 1, D_MODEL), lambda i, c: (l, 0, 0))
    return pl.pallas_call(
        _merge_kernel,
        grid=(s // tm, nc),
        in_specs=[row(D_MODEL), row(D_MODEL), row(D_CONV), row(D_ATTN), row(D_FOURIER),
                  gate(0), gate(1), gate(2),
                  colw(D_CONV), colw(D_ATTN), colw(D_FOURIER),
                  pl.BlockSpec((None, tn, D_MODEL), lambda i, c: (l, c, 0)), vec, vec],
        out_specs=[row(D_MODEL), row(D_MODEL)],
        out_shape=[jax.ShapeDtypeStruct((s, D_MODEL), _F32), jax.ShapeDtypeStruct((s, D_MODEL), _BF16)],
        compiler_params=_params(("arbitrary", "arbitrary")),
        name="merge",
    )(xb, xf, a, o, fr, w_in, w_in, w_in, w_co, w_ao, w_fo, w_o, ln_g, ln_b)


def _ffn_kernel(xm_ref, xp_ref, xn_ref, xf_ref, wg_ref, wv_ref, cw_ref, wd_ref, g_ref, b_ref,
                outf_ref, outb_ref, xs_ref):
    i = pl.program_id(0)
    j = pl.program_id(1)
    tm = xm_ref.shape[0]

    @pl.when(j == 0)
    def _():
        _fill_halo_lhs(xs_ref, xm_ref, xp_ref, xn_ref, i == 0, i == pl.num_programs(0) - 1)
        outf_ref[...] = DEEPNORM_ALPHA * xf_ref[...]

    hg = _dot(xs_ref[...], wg_ref[...])
    hv = _dot(xs_ref[0:tm, :], wv_ref[...])
    h = jax.nn.silu(_dwconv3_rows(hg, cw_ref[...], tm)) * hv
    outf_ref[...] += _dot(h.astype(_BF16), wd_ref[...])

    @pl.when(j == pl.num_programs(1) - 1)
    def _():
        y = _layer_norm(outf_ref[...], g_ref[...], b_ref[...])
        outf_ref[...] = y
        outb_ref[...] = y.astype(_BF16)


def _ffn(xb, xf, w_up, ffn_conv_w, w_down, ln_g, ln_b, l, tm, tn):
    s = xb.shape[0]
    nj = D_FF // tn
    row = pl.BlockSpec((tm, D_MODEL), lambda i, j: (i, 0))
    vec = pl.BlockSpec((None, 1, D_MODEL), lambda i, j: (l, 0, 0))
    return pl.pallas_call(
        _ffn_kernel,
        grid=(s // tm, nj),
        in_specs=_halo_specs(tm, s, D_MODEL, 2) + [
            row,
            pl.BlockSpec((None, D_MODEL, tn), lambda i, j: (l, 0, j)),
            pl.BlockSpec((None, D_MODEL, tn), lambda i, j: (l, 0, nj + j)),
            pl.BlockSpec((None, 3, tn), lambda i, j: (l, 0, j)),
            pl.BlockSpec((None, tn, D_MODEL), lambda i, j: (l, j, 0)),
            vec, vec],
        out_specs=[row, row],
        out_shape=[jax.ShapeDtypeStruct((s, D_MODEL), _F32), jax.ShapeDtypeStruct((s, D_MODEL), _BF16)],
        scratch_shapes=[pltpu.VMEM((tm + _HALO, D_MODEL), _BF16)],
        compiler_params=_params(("arbitrary", "arbitrary")),
        name="ffn",
    )(xb, xb, xb, xf, w_up, w_up, ffn_conv_w, w_down, ln_g, ln_b)


def _prep_weights(w_in, conv_w, q_gain, k_gain, w_conv_out, w_attn_out, w_fourier_out, w_o,
                  ln1_g, ln1_b, w_up, ffn_conv_w, w_down, ln2_g, ln2_b):
    depth = w_in.shape[0]
    vec = lambda v: v.reshape(depth, 1, v.shape[-1])
    qk0, qk1 = _C_QKVF, _C_QKVF + D_ATTN + D_KV
    return dict(
        depth=depth,
        w_in=w_in.astype(_BF16),
        w_qk=_pair_major(w_in[:, :, qk0:qk1]).astype(_BF16),
        conv_w=conv_w,
        q_gain=vec(_pair_major(q_gain)), k_gain=vec(_pair_major(k_gain)),
        w_co=w_conv_out.astype(_BF16), w_ao=w_attn_out.astype(_BF16), w_fo=w_fourier_out.astype(_BF16),
        w_o=w_o.astype(_BF16),
        ln1_g=vec(ln1_g), ln1_b=vec(ln1_b),
        w_up=w_up.astype(_BF16), ffn_conv_w=ffn_conv_w, w_down=w_down.astype(_BF16),
        ln2_g=vec(ln2_g), ln2_b=vec(ln2_b),
    )


def _trunk(x, w):
    s = x.shape[0]
    tm = min(_ROW_TILE, s)
    n1 = _DFT_N1
    n2 = s // n1
    cos, sin = _rope_tables(s)
    wa, mcs = _dft_tables(s)
    dft = _channel_dft_matrix()
    xf = x
    xb = x.astype(_BF16)
    for l in range(w["depth"]):
        a, qt, k, vt, fr, fi = _in_proj(xb, w["w_in"], w["conv_w"], w["w_qk"], w["q_gain"], w["k_gain"],
                                        cos, sin, dft, l, tm)
        yr, yi = _four_a(fr.reshape(n2, n1, D_FOURIER), fi.reshape(n2, n1, D_FOURIER), wa, _DFT_ROWS)
        fmix = _four_c(yr, yi, mcs, min(_DFT_ROWS, n2)).reshape(s, D_FOURIER)
        score_bound = ((1.02 * HEAD_DIM * _Q_SCALE) * jnp.max(jnp.abs(w["q_gain"][l]))
                       * jnp.max(jnp.abs(w["k_gain"][l])))
        o = _attn(qt, k, vt, score_bound, min(_Q_TILE, s), min(_K_CHUNK_ONLINE, s // 2),
                  min(_K_CHUNK_BOUNDED, s // 2))
        xf, xb = _merge(xb, xf, a, o, fmix, w["w_in"], w["w_co"], w["w_ao"], w["w_fo"], w["w_o"],
                        w["ln1_g"], w["ln1_b"], l, tm, _COL_CHUNK)
        xf, xb = _ffn(xb, xf, w["w_up"], w["ffn_conv_w"], w["w_down"], w["ln2_g"], w["ln2_b"], l, tm,
                      _COL_CHUNK)
    return xf


def kernel(x_prompt, x_sample, w_in, conv_w, q_gain, k_gain, w_conv_out, w_attn_out, w_fourier_out,
           w_o, ln1_g, ln1_b, w_up, ffn_conv_w, w_down, ln2_g, ln2_b):
    layers = _prep_weights(w_in, conv_w, q_gain, k_gain, w_conv_out, w_attn_out, w_fourier_out, w_o,
                           ln1_g, ln1_b, w_up, ffn_conv_w, w_down, ln2_g, ln2_b)
    outs = []
    for x in (x_prompt, x_sample):
        b, s, d = x.shape
        rows = x.reshape(b * s, d)
        ys = [_trunk(rows[bi * s:(bi + 1) * s], layers) for bi in range(b)]
        outs.append((ys[0] if b == 1 else jnp.concatenate(ys, axis=0)).reshape(b, s, d))
    return tuple(outs)
```
